```python
import math
import jax, jax.numpy as jnp
from jax import lax
import numpy as np

D_MODEL = 1024
BATCH = 8
SEQ = 2048
DEPTH = 4

N_MIXERS = 3
N_A = (DEPTH + 2) // 3
N_B = (DEPTH + 1) // 3
N_C = DEPTH // 3
DEEPNORM_ALPHA = (2.0 * DEPTH) ** 0.25
DEEPNORM_BETA = (8.0 * DEPTH) ** -0.25
LN_EPS = 1e-5

LRU_WIDTH = D_MODEL
LRU_HEADS = 8
LRU_BLOCK = LRU_WIDTH // LRU_HEADS
CONV_WIDTH = 4
LRU_C = 8.0

RET_HEADS = 4
RET_QK = D_MODEL // RET_HEADS
RET_V = 2 * RET_QK
RET_QK_WIDTH = RET_HEADS * RET_QK
RET_V_WIDTH = RET_HEADS * RET_V
RET_CHUNK = 128
RET_EPS = 1e-6

ATTN_PATTERNS = ((128, 1), (512, 4), (2048, 16))
ATTN_GROUPS = len(ATTN_PATTERNS)
ATTN_HEADS = 16
ATTN_HEAD_DIM = D_MODEL // ATTN_HEADS
ATTN_WIDTH = ATTN_HEADS * ATTN_HEAD_DIM

N_EXPERTS = 32
TOP_K = 4
D_FF = D_MODEL
SWIGLU_LIMIT = 7.0
SWIGLU_ALPHA = 1.702
MOE_BLOCK = 256

kernel_name = "hybrid_rglru_retention_dilated_moe_deepnorm"


def layer_norm(x, gain, bias):
    xf = x.astype(jnp.float32)
    mu = jnp.mean(xf, axis=-1, keepdims=True)
    var = jnp.mean(jnp.square(xf - mu), axis=-1, keepdims=True)
    return ((xf - mu) * lax.rsqrt(var + LN_EPS) * gain + bias).astype(x.dtype)


def _linear_combine(left, right):
    a_l, b_l = left
    a_r, b_r = right
    return a_l * a_r, a_r * b_l + b_r


def rglru_mixer(x, w_in, conv_w, conv_b, w_rgate, b_rgate, w_igate, b_igate, lam, w_out):
    bsz, seq, _ = x.shape
    u = x @ w_in
    gate_branch, rec = u[..., :LRU_WIDTH], u[..., LRU_WIDTH:]
    rec = lax.conv_general_dilated(
        rec, conv_w[:, None, :], window_strides=(1,), padding=((CONV_WIDTH - 1, 0),),
        dimension_numbers=('NWC', 'WIO', 'NWC'), feature_group_count=LRU_WIDTH) + conv_b
    xb = rec.reshape(bsz, seq, LRU_HEADS, LRU_BLOCK)
    r = jax.nn.sigmoid(jnp.einsum('bshi,hij->bshj', xb, w_rgate).reshape(bsz, seq, LRU_WIDTH) + b_rgate)
    i = jax.nn.sigmoid(jnp.einsum('bshi,hij->bshj', xb, w_igate).reshape(bsz, seq, LRU_WIDTH) + b_igate)
    log_a = -LRU_C * r.astype(jnp.float32) * jax.nn.softplus(-lam.astype(jnp.float32))
    a = jnp.exp(log_a)
    b = jnp.sqrt(-jnp.expm1(2.0 * log_a)) * (i * rec).astype(jnp.float32)
    _, h = lax.associative_scan(_linear_combine, (a, b), axis=1)
    y = jax.nn.gelu(gate_branch) * h.astype(x.dtype)
    return y @ w_out


def retention_mixer(x, w_in, w_out):
    bsz, seq, _ = x.shape
    f32 = jnp.float32
    u = x @ w_in
    q, k, v, g = jnp.split(u, [RET_QK_WIDTH, 2 * RET_QK_WIDTH, 2 * RET_QK_WIDTH + RET_V_WIDTH], axis=-1)
    nc = seq // RET_CHUNK

    def chunks(t, d):
        return t.reshape(bsz, nc, RET_CHUNK, RET_HEADS, d).transpose(1, 0, 3, 2, 4).astype(f32)

    qc = chunks(q, RET_QK)
    kc = chunks(k, RET_QK) * RET_QK ** -0.5
    vc = chunks(v, RET_V)
    log_gamma = jnp.log1p(-jnp.exp2(-5.0 - jnp.arange(RET_HEADS, dtype=f32)))
    pos = jnp.arange(RET_CHUNK, dtype=f32)
    rel = pos[:, None] - pos[None, :]
    intra_decay = jnp.where(rel >= 0, jnp.exp(log_gamma[:, None, None] * jnp.maximum(rel, 0.0)), 0.0)
    query_decay = jnp.exp(log_gamma[:, None] * (pos + 1.0))
    key_decay = jnp.exp(log_gamma[:, None] * (RET_CHUNK - 1.0 - pos))
    chunk_decay = jnp.exp(log_gamma * RET_CHUNK)

    def step(state, inp):
        qi, ki, vi = inp
        intra = jnp.einsum('bhnd,bhmd->bhnm', qi, ki) * intra_decay
        out = (jnp.einsum('bhnm,bhmv->bhnv', intra, vi)
               + jnp.einsum('bhnd,bhdv->bhnv', qi, state) * query_decay[..., None])
        state = (chunk_decay[:, None, None] * state
                 + jnp.einsum('bhmd,bhmv->bhdv', ki * key_decay[..., None], vi))
        return state, out

    state0 = jnp.zeros((bsz, RET_HEADS, RET_QK, RET_V), f32)
    _, o = lax.scan(step, state0, (qc, kc, vc))
    o = o.transpose(1, 0, 3, 2, 4).reshape(bsz, seq, RET_HEADS, RET_V)
    o = o * lax.rsqrt(jnp.mean(o * o, axis=-1, keepdims=True) + RET_EPS)
    y = (jax.nn.silu(g.astype(f32)) * o.reshape(bsz, seq, RET_V_WIDTH)).astype(x.dtype)
    return y @ w_out


def _dilated_group(q, k, v, slopes, window, dilation):
    bsz, seq, h, dh = q.shape
    f32 = jnp.float32
    nw = window // dilation
    sub = seq // dilation
    nb = -(-sub // nw)
    pad = nb * nw - sub

    def to_sub(t):
        t = t.reshape(bsz, sub, dilation, h, dh).transpose(0, 2, 1, 3, 4).reshape(bsz * dilation, sub, h, dh)
        t = jnp.pad(t, ((0, 0), (0, pad), (0, 0), (0, 0)))
        return t.reshape(bsz * dilation, nb, nw, h, dh)

    def with_prev(t):
        prev = jnp.concatenate([jnp.zeros_like(t[:, :1]), t[:, :-1]], axis=1)
        return jnp.concatenate([prev, t], axis=2)

    qb = to_sub(q)
    kk = with_prev(to_sub(k))
    vv = with_prev(to_sub(v))
    scores = jnp.einsum('bnqhd,bnkhd->bnhqk', qb, kk, preferred_element_type=f32) * (dh ** -0.5)
    qi = jnp.arange(nw)[:, None]
    ki = jnp.arange(2 * nw)[None, :]
    steps = qi + nw - ki
    key_pos = (jnp.arange(nb)[:, None, None] - 1) * nw + ki
    valid = (steps >= 0) & (steps <= nw) & (key_pos >= 0)
    alibi = -slopes[:, None, None] * (steps * dilation).astype(f32)
    scores = jnp.where(valid[None, :, None], scores + alibi[None, None], -jnp.inf)
    m = jnp.max(scores, axis=-1)
    p = jnp.exp(scores - m[..., None])
    s = jnp.sum(p, axis=-1)
    o = jnp.einsum('bnhqk,bnkhd->bnqhd', p, vv.astype(f32))

    def from_sub(t):
        rest = t.shape[3:]
        t = t.reshape(bsz, dilation, nb * nw, *rest)[:, :, :sub]
        return jnp.swapaxes(t, 1, 2).reshape(bsz, seq, *rest)

    return from_sub(jnp.swapaxes(m, 2, 3)), from_sub(jnp.swapaxes(s, 2, 3)), from_sub(o)


def dilated_attention_mixer(x, w_in, w_out):
    bsz, seq, _ = x.shape
    u = (x @ w_in).reshape(bsz, seq, 3, ATTN_GROUPS, ATTN_HEADS, ATTN_HEAD_DIM)
    slopes = jnp.exp2(-8.0 * jnp.arange(1, ATTN_HEADS + 1, dtype=jnp.float32) / ATTN_HEADS)
    maxes, dens, nums = [], [], []
    for g, (window, dilation) in enumerate(ATTN_PATTERNS):
        m, s, o = _dilated_group(u[:, :, 0, g], u[:, :, 1, g], u[:, :, 2, g], slopes, window, dilation)
        maxes.append(m)
        dens.append(s)
        nums.append(o)
    m_all = jnp.stack(maxes)
    wgt = jnp.exp(m_all - jnp.max(m_all, axis=0))
    den = jnp.sum(wgt * jnp.stack(dens), axis=0)
    num = jnp.sum(wgt[..., None] * jnp.stack(nums), axis=0)
    y = (num / den[..., None]).reshape(bsz, seq, ATTN_WIDTH).astype(x.dtype)
    return y @ w_out


def moe_ffn(x, w_router, b_router, w_gu, b_gu, w_down, b_down):
    bsz, seq, d = x.shape
    t = bsz * seq
    xt = x.reshape(t, d)
    logits = (xt @ w_router + b_router).astype(jnp.float32)
    top_logit, top_exp = lax.top_k(logits, TOP_K)
    gates = jax.nn.softmax(top_logit, axis=-1)
    n_assign = t * TOP_K
    flat_exp = top_exp.reshape(n_assign)
    order = jnp.argsort(flat_exp)
    sorted_exp = flat_exp[order]
    counts = jnp.bincount(flat_exp, length=N_EXPERTS)
    padded = ((counts + MOE_BLOCK - 1) // MOE_BLOCK) * MOE_BLOCK
    pad_end = jnp.cumsum(padded)
    pad_start = pad_end - padded
    start = jnp.cumsum(counts) - counts
    dest_sorted = pad_start[sorted_exp] + jnp.arange(n_assign) - start[sorted_exp]
    n_blocks = -(-n_assign // MOE_BLOCK) + N_EXPERTS
    n_rows = n_blocks * MOE_BLOCK
    row_token = jnp.full((n_rows,), t, jnp.int32).at[dest_sorted].set((order // TOP_K).astype(jnp.int32))
    x_rows = jnp.concatenate([xt, jnp.zeros((1, d), xt.dtype)], axis=0)[row_token]
    x_rows = x_rows.reshape(n_blocks, MOE_BLOCK, d)
    block_exp = jnp.minimum(jnp.searchsorted(pad_end, jnp.arange(n_blocks) * MOE_BLOCK, side='right'),
                            N_EXPERTS - 1)

    def expert_block(args):
        xb, e = args
        gu = xb @ w_gu[e] + b_gu[e]
        gate, up = gu[:, :D_FF], gu[:, D_FF:]
        gate = jnp.minimum(gate, SWIGLU_LIMIT)
        up = jnp.clip(up, -SWIGLU_LIMIT, SWIGLU_LIMIT)
        hidden = (up + 1.0) * gate * jax.nn.sigmoid(SWIGLU_ALPHA * gate)
        return hidden @ w_down[e] + b_down[e]

    y_rows = lax.map(expert_block, (x_rows, block_exp)).reshape(n_rows, d)
    dest = jnp.zeros((n_assign,), dest_sorted.dtype).at[order].set(dest_sorted)
    y_sel = y_rows[dest].reshape(t, TOP_K, d)
    out = jnp.einsum('tk,tkd->td', gates.astype(x.dtype), y_sel)
    return out.reshape(bsz, seq, d)


def setup_inputs(seed: int = 0) -> dict:
    key = jax.random.key(seed)
    ks = jax.random.split(key, 24)
    f32 = jnp.float32

    def nrm(k, shape, scale):
        return jax.random.normal(k, shape, f32) * scale

    a0 = jax.random.uniform(ks[9], (N_A, LRU_WIDTH), f32, minval=0.9, maxval=0.999)
    root = a0 ** (1.0 / LRU_C)
    return {
        'x': nrm(ks[0], (BATCH, SEQ, D_MODEL), 1.0),
        'a_w_in': nrm(ks[1], (N_A, D_MODEL, 2 * LRU_WIDTH), D_MODEL ** -0.5),
        'a_conv_w': nrm(ks[2], (N_A, CONV_WIDTH, LRU_WIDTH), CONV_WIDTH ** -0.5),
        'a_conv_b': nrm(ks[3], (N_A, LRU_WIDTH), 0.02),
        'a_w_rgate': nrm(ks[4], (N_A, LRU_HEADS, LRU_BLOCK, LRU_BLOCK), LRU_BLOCK ** -0.5),
        'a_b_rgate': nrm(ks[5], (N_A, LRU_WIDTH), 0.02),
        'a_w_igate': nrm(ks[6], (N_A, LRU_HEADS, LRU_BLOCK, LRU_BLOCK), LRU_BLOCK ** -0.5),
        'a_b_igate': nrm(ks[7], (N_A, LRU_WIDTH), 0.02),
        'a_lambda': jnp.log(root) - jnp.log1p(-root),
        'a_w_out': nrm(ks[8], (N_A, LRU_WIDTH, D_MODEL), DEEPNORM_BETA * LRU_WIDTH ** -0.5),
        'b_w_in': nrm(ks[10], (N_B, D_MODEL, 2 * RET_QK_WIDTH + 2 * RET_V_WIDTH), D_MODEL ** -0.5),
        'b_w_out': nrm(ks[11], (N_B, RET_V_WIDTH, D_MODEL), DEEPNORM_BETA * RET_V_WIDTH ** -0.5),
        'c_w_in': nrm(ks[12], (N_C, D_MODEL, 3 * ATTN_GROUPS * ATTN_WIDTH), D_MODEL ** -0.5),
        'c_w_out': nrm(ks[13], (N_C, ATTN_WIDTH, D_MODEL), DEEPNORM_BETA * ATTN_WIDTH ** -0.5),
        'ln_gain': 1.0 + nrm(ks[14], (DEPTH, 2, D_MODEL), 0.02),
        'ln_bias': nrm(ks[15], (DEPTH, 2, D_MODEL), 0.02),
        'moe_w_router': nrm(ks[16], (DEPTH, D_MODEL, N_EXPERTS), D_MODEL ** -0.5),
        'moe_b_router': nrm(ks[17], (DEPTH, N_EXPERTS), 0.01),
        'moe_w_gu': nrm(ks[18], (DEPTH, N_EXPERTS, D_MODEL, 2 * D_FF), D_MODEL ** -0.5),
        'moe_b_gu': nrm(ks[19], (DEPTH, N_EXPERTS, 2 * D_FF), 0.02),
        'moe_w_down': nrm(ks[20], (DEPTH, N_EXPERTS, D_FF, D_MODEL), DEEPNORM_BETA * D_FF ** -0.5),
        'moe_b_down': nrm(ks[21], (DEPTH, N_EXPERTS, D_MODEL), 0.02),
    }


def reference(x, a_w_in, a_conv_w, a_conv_b, a_w_rgate, a_b_rgate, a_w_igate, a_b_igate, a_lambda,
              a_w_out, b_w_in, b_w_out, c_w_in, c_w_out, ln_gain, ln_bias, moe_w_router,
              moe_b_router, moe_w_gu, moe_b_gu, moe_w_down, moe_b_down):
    for layer in range(DEPTH):
        kind, idx = layer % N_MIXERS, layer // N_MIXERS
        if kind == 0:
            y = rglru_mixer(x, a_w_in[idx], a_conv_w[idx], a_conv_b[idx], a_w_rgate[idx], a_b_rgate[idx],
                            a_w_igate[idx], a_b_igate[idx], a_lambda[idx], a_w_out[idx])
        elif kind == 1:
            y = retention_mixer(x, b_w_in[idx], b_w_out[idx])
        else:
            y = dilated_attention_mixer(x, c_w_in[idx], c_w_out[idx])
        x = layer_norm(DEEPNORM_ALPHA * x + y, ln_gain[layer, 0], ln_bias[layer, 0])
        f = moe_ffn(x, moe_w_router[layer], moe_b_router[layer], moe_w_gu[layer], moe_b_gu[layer],
                    moe_w_down[layer], moe_b_down[layer])
        x = layer_norm(DEEPNORM_ALPHA * x + f, ln_gain[layer, 1], ln_bias[layer, 1])
    return x
```

```python
import functools
import math

import jax
import jax.numpy as jnp
from jax import lax
from jax.experimental import pallas as pl
from jax.experimental.pallas import tpu as pltpu

F32 = jnp.float32
BF16 = jnp.bfloat16
I32 = jnp.int32

D_MODEL = 1024
DEPTH = 4
N_MIXERS = 3
DEEPNORM_ALPHA = (2.0 * DEPTH) ** 0.25
LN_EPS = 1e-5
LRU_HEADS = 8
LRU_BLOCK = D_MODEL // LRU_HEADS
CONV_WIDTH = 4
LRU_C = 8.0
RET_HEADS = 4
RET_QK = D_MODEL // RET_HEADS
RET_V = 2 * RET_QK
RET_CHUNK = 128
RET_EPS = 1e-6
ATTN_PATTERNS = ((128, 1), (512, 4), (2048, 16))
ATTN_HEADS = 16
ATTN_HEAD_DIM = D_MODEL // ATTN_HEADS
N_EXPERTS = 32
TOP_K = 4
D_FF = D_MODEL
SWIGLU_LIMIT = 7.0
SWIGLU_ALPHA = 1.702

SUBLANES = 8
LANES = 128
LANE_TILES = D_MODEL // LANES
VMEM_LIMIT = 56 * 1024 * 1024

MOE_TM = 256
ROW_TM = 256


def _params(sem, vmem=VMEM_LIMIT):
    return pltpu.CompilerParams(dimension_semantics=sem, vmem_limit_bytes=vmem)


def _mm_body(x_ref, w_ref, o_ref, wbf_ref):
    @pl.when(pl.program_id(1) == 0)
    def _():
        wbf_ref[...] = w_ref[...].astype(BF16)

    o_ref[...] = jnp.dot(x_ref[...].astype(BF16), wbf_ref[...],
                         preferred_element_type=F32).astype(o_ref.dtype)


def _matmul(x, w_stack, idx, col0, ncols, out_dtype, name, tm=512, tn=1024, col_block_stride=1):
    t, k = x.shape
    assert col0 % tn == 0 and ncols % tn == 0 and t % tm == 0
    return pl.pallas_call(
        _mm_body,
        grid=(ncols // tn, t // tm),
        in_specs=[pl.BlockSpec((tm, k), lambda n, m: (m, 0)),
                  pl.BlockSpec((None, k, tn), lambda n, m: (idx, 0, col0 // tn + n * col_block_stride))],
        out_specs=pl.BlockSpec((tm, tn), lambda n, m: (m, n)),
        out_shape=jax.ShapeDtypeStruct((t, ncols), out_dtype),
        scratch_shapes=[pltpu.VMEM((k, tn), BF16)],
        compiler_params=_params(("arbitrary", "arbitrary")),
        name=name,
    )(x, w_stack)


def _layer_norm(z, gain, bias):
    mu = jnp.mean(z, axis=-1, keepdims=True)
    zc = z - mu
    var = jnp.mean(zc * zc, axis=-1, keepdims=True)
    return zc * lax.rsqrt(var + LN_EPS) * gain + bias


def _store_token_tiles(ref, val):
    tm = val.shape[0]
    for c in range(LANE_TILES):
        ref[pl.ds(c, tm, stride=LANE_TILES), :] = val[:, c * LANES:(c + 1) * LANES]


def _load_token_tiles(ref, tm):
    return jnp.concatenate([ref[pl.ds(c, tm, stride=LANE_TILES), :] for c in range(LANE_TILES)], axis=-1)


def _post_body(y_ref, w_ref, x_ref, g_ref, b_ref, wr_ref, br_ref,
               xt_ref, rf_ref, ri_ref, cnt_ref, wbf_ref, run_ref):
    i = pl.program_id(0)
    tm = x_ref.shape[0]

    @pl.when(i == 0)
    def _():
        wbf_ref[...] = w_ref[...].astype(BF16)
        run_ref[...] = jnp.zeros_like(run_ref)

    z = DEEPNORM_ALPHA * x_ref[...] + jnp.dot(y_ref[...].astype(BF16), wbf_ref[...],
                                              preferred_element_type=F32)
    xn = _layer_norm(z, g_ref[...], b_ref[...])
    _store_token_tiles(xt_ref, xn)

    logits = jnp.dot(xn, wr_ref[...], preferred_element_type=F32,
                     precision=lax.Precision.HIGHEST) + br_ref[...]
    lane_e = lax.broadcasted_iota(I32, (tm, N_EXPERTS), 1).astype(F32)
    work = logits
    top_l, top_e, onehots = [], [], []
    for _ in range(TOP_K):
        m = jnp.max(work, axis=-1, keepdims=True)
        idx = jnp.min(jnp.where(work == m, lane_e, float(N_EXPERTS)), axis=-1, keepdims=True)
        oh = lane_e == idx
        top_l.append(m)
        top_e.append(idx)
        onehots.append(oh)
        work = jnp.where(oh, -jnp.inf, work)
    exps = [jnp.exp(l - top_l[0]) for l in top_l]
    denom = exps[0] + exps[1] + exps[2] + exps[3]
    gates = [e / denom for e in exps]

    sel = jnp.zeros((tm, N_EXPERTS), F32)
    for oh in onehots:
        sel = sel + oh.astype(F32)
    ri = lax.broadcasted_iota(I32, (tm, tm), 0)
    ci = lax.broadcasted_iota(I32, (tm, tm), 1)
    tri = (ci < ri).astype(BF16)
    before = jnp.dot(tri, sel.astype(BF16), preferred_element_type=F32) + run_ref[...]
    ranks = [jnp.sum(jnp.where(oh, before, 0.0), axis=-1, keepdims=True) for oh in onehots]
    run_ref[...] = run_ref[...] + jnp.sum(sel, axis=0, keepdims=True)
    cnt_ref[...] = run_ref[...]

    lane = lax.broadcasted_iota(I32, (tm, LANES), 1)
    rf = jnp.zeros((tm, LANES), F32)
    rint = jnp.zeros((tm, LANES), I32)
    for k in range(TOP_K):
        rf = jnp.where(lane == k, gates[k], rf)
        rint = jnp.where(lane == k, top_e[k].astype(I32), rint)
        rint = jnp.where(lane == TOP_K + k, ranks[k].astype(I32), rint)
    rf_ref[...] = rf
    ri_ref[...] = rint


def _post_mixer(y, w_out_stack, idx, x, gain, bias, w_router, b_router, name):
    t, kd = y.shape
    tm = ROW_TM
    return pl.pallas_call(
        _post_body,
        grid=(t // tm,),
        in_specs=[pl.BlockSpec((tm, kd), lambda i: (i, 0)),
                  pl.BlockSpec((None, kd, D_MODEL), lambda i: (idx, 0, 0)),
                  pl.BlockSpec((tm, D_MODEL), lambda i: (i, 0)),
                  pl.BlockSpec((1, D_MODEL), lambda i: (0, 0)),
                  pl.BlockSpec((1, D_MODEL), lambda i: (0, 0)),
                  pl.BlockSpec((D_MODEL, N_EXPERTS), lambda i: (0, 0)),
                  pl.BlockSpec((1, N_EXPERTS), lambda i: (0, 0))],
        out_specs=[pl.BlockSpec((tm * LANE_TILES, LANES), lambda i: (i, 0)),
                   pl.BlockSpec((tm, LANES), lambda i: (i, 0)),
                   pl.BlockSpec((tm, LANES), lambda i: (i, 0)),
                   pl.BlockSpec((1, N_EXPERTS), lambda i: (0, 0))],
        out_shape=[jax.ShapeDtypeStruct((t * LANE_TILES, LANES), F32),
                   jax.ShapeDtypeStruct((t, LANES), F32),
                   jax.ShapeDtypeStruct((t, LANES), I32),
                   jax.ShapeDtypeStruct((1, N_EXPERTS), F32)],
        scratch_shapes=[pltpu.VMEM((kd, D_MODEL), BF16), pltpu.VMEM((1, N_EXPERTS), F32)],
        compiler_params=_params(("arbitrary",)),
        name=name,
    )(y, w_out_stack, x, gain, bias, w_router, b_router)


def _gather_copy(x_hbm, xbuf, sem, slot, src_row, r):
    return pltpu.make_async_copy(
        x_hbm.at[pl.ds(pl.multiple_of(src_row, SUBLANES), SUBLANES), :],
        xbuf.at[slot, pl.ds(pl.multiple_of(r * SUBLANES, SUBLANES), SUBLANES), :],
        sem.at[slot])


def _scatter_copy(ybuf, y_hbm, sem, slot, dst_row, r):
    return pltpu.make_async_copy(
        ybuf.at[slot, pl.ds(pl.multiple_of(r * SUBLANES, SUBLANES), SUBLANES), :],
        y_hbm.at[pl.ds(pl.multiple_of(dst_row, SUBLANES), SUBLANES), :],
        sem.at[slot])


def _moe_body(be_ref, nu_ref, src_ref, srcn_ref, dst_ref, x_hbm, wgu_ref, bgu_ref, wd_ref, bd_ref,
              y_hbm, xbuf, ybuf, wgu_bf, wd_bf, gsem, ssem):
    i = pl.program_id(0)
    nb = pl.num_programs(0)
    nused = nu_ref[0]
    slot = i % 2
    rows = MOE_TM * LANE_TILES

    def issue_gather(idx_ref, s):
        def body(r, c):
            _gather_copy(x_hbm, xbuf, gsem, s, idx_ref[0, r], r).start()
            return c
        lax.fori_loop(0, MOE_TM, body, 0, unroll=8)

    def wait_scatter(s):
        pltpu.make_async_copy(ybuf.at[s], y_hbm.at[pl.ds(0, rows), :], ssem.at[s]).wait()

    @pl.when(i == 0)
    def _():
        ybuf[...] = jnp.zeros_like(ybuf)
        tail = y_hbm.at[pl.ds(y_hbm.shape[0] - 2 * rows, 2 * rows), :]
        for s in range(2):
            cp = pltpu.make_async_copy(ybuf.at[s], tail.at[pl.ds(s * rows, rows), :], ssem.at[s])
            cp.start()
            cp.wait()

    @pl.when(i < nused)
    def _():
        @pl.when(i == 0)
        def _():
            issue_gather(src_ref, 0)

        @pl.when(i + 1 < nused)
        def _():
            issue_gather(srcn_ref, 1 - slot)

        changed = jnp.logical_or(i == 0, be_ref[i] != be_ref[jnp.maximum(i - 1, 0)])

        @pl.when(changed)
        def _():
            wgu_bf[...] = wgu_ref[...].astype(BF16)
            wd_bf[...] = wd_ref[...].astype(BF16)

        pltpu.make_async_copy(x_hbm.at[pl.ds(0, rows), :], xbuf.at[slot], gsem.at[slot]).wait()
        x = _load_token_tiles(xbuf.at[slot], MOE_TM).astype(BF16)
        gu = jnp.dot(x, wgu_bf[...], preferred_element_type=F32) + bgu_ref[...]
        gate = jnp.minimum(gu[:, :D_FF], SWIGLU_LIMIT)
        up = jnp.clip(gu[:, D_FF:], -SWIGLU_LIMIT, SWIGLU_LIMIT)
        hidden = (up + 1.0) * gate * jax.nn.sigmoid(SWIGLU_ALPHA * gate)
        y = jnp.dot(hidden.astype(BF16), wd_bf[...], preferred_element_type=F32) + bd_ref[...]

        @pl.when(i >= 2)
        def _():
            wait_scatter(slot)

        _store_token_tiles(ybuf.at[slot], y)

        def sbody(r, c):
            _scatter_copy(ybuf, y_hbm, ssem, slot, dst_ref[0, r], r).start()
            return c
        lax.fori_loop(0, MOE_TM, sbody, 0, unroll=8)

    @pl.when(i == nb - 1)
    def _():
        @pl.when(nused >= 1)
        def _():
            wait_scatter((nused - 1) % 2)

        @pl.when(nused >= 2)
        def _():
            wait_scatter(nused % 2)


def _moe_experts(xt, block_exp, n_used, src_rows, dst_rows, w_gu, b_gu, w_down, b_down, layer, n_tok):
    nb = block_exp.shape[0]
    rows = MOE_TM * LANE_TILES
    n_out_rows = (TOP_K * n_tok + 2 * MOE_TM) * LANE_TILES
    smem_blk = lambda f: pl.BlockSpec((None, 1, MOE_TM), f, memory_space=pltpu.SMEM)
    grid_spec = pltpu.PrefetchScalarGridSpec(
        num_scalar_prefetch=2,
        grid=(nb,),
        in_specs=[smem_blk(lambda i, be, nu: (i, 0, 0)),
                  smem_blk(lambda i, be, nu: (jnp.minimum(i + 1, nb - 1), 0, 0)),
                  smem_blk(lambda i, be, nu: (i, 0, 0)),
                  pl.BlockSpec(memory_space=pl.ANY),
                  pl.BlockSpec((None, None, D_MODEL, 2 * D_FF), lambda i, be, nu: (layer, be[i], 0, 0)),
                  pl.BlockSpec((None, None, 1, 2 * D_FF), lambda i, be, nu: (layer, be[i], 0, 0)),
                  pl.BlockSpec((None, None, D_FF, D_MODEL), lambda i, be, nu: (layer, be[i], 0, 0)),
                  pl.BlockSpec((None, None, 1, D_MODEL), lambda i, be, nu: (layer, be[i], 0, 0))],
        out_specs=pl.BlockSpec(memory_space=pl.ANY),
        scratch_shapes=[pltpu.VMEM((2, rows, LANES), F32),
                        pltpu.VMEM((2, rows, LANES), F32),
                        pltpu.VMEM((D_MODEL, 2 * D_FF), BF16),
                        pltpu.VMEM((D_FF, D_MODEL), BF16),
                        pltpu.SemaphoreType.DMA((2,)),
                        pltpu.SemaphoreType.DMA((2,))],
    )
    return pl.pallas_call(
        _moe_body,
        grid_spec=grid_spec,
        out_shape=jax.ShapeDtypeStruct((n_out_rows, LANES), F32),
        compiler_params=_params(("arbitrary",)),
        name=f"moe_experts_l{layer}",
    )(block_exp, n_used, src_rows, src_rows, dst_rows, xt, w_gu, b_gu, w_down, b_down)


def _route_tables(route_i, counts, n_tok):
    n_assign = n_tok * TOP_K
    nb = n_assign // MOE_TM + N_EXPERTS
    counts = counts.reshape(N_EXPERTS).astype(I32)
    padded = ((counts + MOE_TM - 1) // MOE_TM) * MOE_TM
    pad_end = jnp.cumsum(padded)
    pad_start = pad_end - padded
    eid = route_i[:, :TOP_K]
    rank = route_i[:, TOP_K:2 * TOP_K]
    dest = (pad_start[eid] + rank).reshape(n_assign)
    tok = lax.broadcasted_iota(I32, (n_tok, TOP_K), 0)
    kk = lax.broadcasted_iota(I32, (n_tok, TOP_K), 1)
    src_val = (tok * LANE_TILES).reshape(n_assign)
    dst_val = ((kk * n_tok + tok) * LANE_TILES).reshape(n_assign)
    trash = (TOP_K * n_tok + jnp.arange(nb * MOE_TM, dtype=I32) % (2 * MOE_TM)) * LANE_TILES
    src_rows = jnp.zeros((nb * MOE_TM,), I32).at[dest].set(src_val, unique_indices=True)
    dst_rows = trash.at[dest].set(dst_val, unique_indices=True)
    block_exp = jnp.minimum(jnp.searchsorted(pad_end, jnp.arange(nb, dtype=I32) * MOE_TM, side='right'),
                            N_EXPERTS - 1).astype(I32)
    n_used = (pad_end[-1:] // MOE_TM).astype(I32)
    return block_exp, n_used, src_rows.reshape(nb, 1, MOE_TM), dst_rows.reshape(nb, 1, MOE_TM)


def _combine_body(y0_ref, y1_ref, y2_ref, y3_ref, rf_ref, xt_ref, g_ref, b_ref, o_ref):
    tm = o_ref.shape[0]
    rf = rf_ref[...]
    parts = []
    for c in range(LANE_TILES):
        acc = DEEPNORM_ALPHA * xt_ref[pl.ds(c, tm, stride=LANE_TILES), :]
        for k, yk in enumerate((y0_ref, y1_ref, y2_ref, y3_ref)):
            acc = acc + rf[:, k:k + 1] * yk[pl.ds(c, tm, stride=LANE_TILES), :]
        parts.append(acc)
    z = jnp.concatenate(parts, axis=-1)
    o_ref[...] = _layer_norm(z, g_ref[...], b_ref[...])


def _combine(y_tk, route_f, xt, gain, bias, n_tok, name):
    tm = ROW_TM
    nblk = n_tok // tm
    yspec = lambda k: pl.BlockSpec((tm * LANE_TILES, LANES), lambda i: (k * nblk + i, 0))
    return pl.pallas_call(
        _combine_body,
        grid=(nblk,),
        in_specs=[yspec(0), yspec(1), yspec(2), yspec(3),
                  pl.BlockSpec((tm, LANES), lambda i: (i, 0)),
                  pl.BlockSpec((tm * LANE_TILES, LANES), lambda i: (i, 0)),
                  pl.BlockSpec((1, D_MODEL), lambda i: (0, 0)),
                  pl.BlockSpec((1, D_MODEL), lambda i: (0, 0))],
        out_specs=pl.BlockSpec((tm, D_MODEL), lambda i: (i, 0)),
        out_shape=jax.ShapeDtypeStruct((n_tok, D_MODEL), F32),
        compiler_params=_params(("parallel",)),
        name=name,
    )(y_tk, y_tk, y_tk, y_tk, route_f, xt, gain, bias)


def _gelu_tanh(x):
    return 0.5 * x * (1.0 + jnp.tanh(math.sqrt(2.0 / math.pi) * (x + 0.044715 * (x * x * x))))


def _lru_gate_body(rec_ref, prev_ref, cw_ref, cb_ref, wr_ref, br_ref, wi_ref, bi_ref, lam_ref,
                   a_ref, b_ref, ext_ref):
    s = pl.program_id(1)
    ts = rec_ref.shape[0]
    cur = rec_ref[...]
    ext_ref[0:SUBLANES, :] = jnp.where(s == 0, 0.0, prev_ref[...])
    ext_ref[SUBLANES:, :] = cur
    conv = cb_ref[...] + cw_ref[CONV_WIDTH - 1:CONV_WIDTH, :] * cur
    for j in range(CONV_WIDTH - 1):
        conv = conv + cw_ref[j:j + 1, :] * ext_ref[pl.ds(SUBLANES - (CONV_WIDTH - 1) + j, ts), :]
    xb = conv.astype(BF16)
    r_parts, i_parts = [], []
    for h in range(LRU_HEADS):
        xh = xb[:, h * LRU_BLOCK:(h + 1) * LRU_BLOCK]
        r_parts.append(jnp.dot(xh, wr_ref[h].astype(BF16), preferred_element_type=F32))
        i_parts.append(jnp.dot(xh, wi_ref[h].astype(BF16), preferred_element_type=F32))
    r = jax.nn.sigmoid(jnp.concatenate(r_parts, axis=-1) + br_ref[...])
    ig = jax.nn.sigmoid(jnp.concatenate(i_parts, axis=-1) + bi_ref[...])
    neg_lam = -lam_ref[...]
    softplus = jnp.maximum(neg_lam, 0.0) + jnp.log1p(jnp.exp(-jnp.abs(neg_lam)))
    log_a = -LRU_C * r * softplus
    a = jnp.exp(log_a)
    a_ref[...] = a
    one_minus_a2 = -jnp.tanh(log_a) * (a * a + 1.0)
    b_ref[...] = jnp.sqrt(one_minus_a2) * (ig * conv)


def _lru_scan_body(a_ref, b_ref, g_ref, y_ref):
    seq, ct = a_ref.shape
    row = lax.broadcasted_iota(I32, (SUBLANES, ct), 0)

    def step(gi, carry):
        off = pl.multiple_of(gi * SUBLANES, SUBLANES)
        a = a_ref[pl.ds(off, SUBLANES), :]
        b = b_ref[pl.ds(off, SUBLANES), :]
        for d in (1, 2, 4):
            keep = row >= d
            b = jnp.where(keep, a * pltpu.roll(b, d, 0) + b, b)
            a = jnp.where(keep, a * pltpu.roll(a, d, 0), a)
        h = a * carry + b
        y_ref[pl.ds(off, SUBLANES), :] = _gelu_tanh(g_ref[pl.ds(off, SUBLANES), :]) * h
        return jnp.broadcast_to(h[SUBLANES - 1:SUBLANES, :], (SUBLANES, ct))

    lax.fori_loop(0, seq // SUBLANES, step, jnp.zeros((SUBLANES, ct), F32))


def _rglru_mixer(x2, bsz, seq, idx, a_w_in, conv_w, conv_b, w_rgate, b_rgate, w_igate, b_igate, lam, layer):
    t = bsz * seq
    w = D_MODEL
    u = _matmul(x2, a_w_in, idx, 0, 2 * w, F32, f"lru_in_l{layer}")
    ts = 256
    nst = seq // ts
    vec = lambda: pl.BlockSpec((1, w), lambda b, s: (0, 0))
    a, bb = pl.pallas_call(
        _lru_gate_body,
        grid=(bsz, nst),
        in_specs=[pl.BlockSpec((ts, w), lambda b, s: (b * nst + s, 1)),
                  pl.BlockSpec((SUBLANES, w),
                               lambda b, s: (jnp.maximum((b * nst + s) * (ts // SUBLANES) - 1, 0), 1)),
                  pl.BlockSpec((CONV_WIDTH, w), lambda b, s: (0, 0)),
                  vec(),
                  pl.BlockSpec((LRU_HEADS, LRU_BLOCK, LRU_BLOCK), lambda b, s: (0, 0, 0)),
                  vec(),
                  pl.BlockSpec((LRU_HEADS, LRU_BLOCK, LRU_BLOCK), lambda b, s: (0, 0, 0)),
                  vec(), vec()],
        out_specs=[pl.BlockSpec((ts, w), lambda b, s: (b * nst + s, 0)),
                   pl.BlockSpec((ts, w), lambda b, s: (b * nst + s, 0))],
        out_shape=[jax.ShapeDtypeStruct((t, w), F32), jax.ShapeDtypeStruct((t, w), F32)],
        scratch_shapes=[pltpu.VMEM((ts + SUBLANES, w), F32)],
        compiler_params=_params(("parallel", "parallel")),
        name=f"lru_gates_l{layer}",
    )(u, u, conv_w[idx], conv_b[idx][None], w_rgate[idx], b_rgate[idx][None],
      w_igate[idx], b_igate[idx][None], lam[idx][None])
    ct = 256
    return pl.pallas_call(
        _lru_scan_body,
        grid=(bsz, w // ct),
        in_specs=[pl.BlockSpec((seq, ct), lambda b, j: (b, j)),
                  pl.BlockSpec((seq, ct), lambda b, j: (b, j)),
                  pl.BlockSpec((seq, ct), lambda b, j: (b, j))],
        out_specs=pl.BlockSpec((seq, ct), lambda b, j: (b, j)),
        out_shape=jax.ShapeDtypeStruct((t, w), F32),
        compiler_params=_params(("parallel", "parallel")),
        name=f"lru_scan_l{layer}",
    )(a, bb, u)


def _ret_body(lg_ref, q_ref, k_ref, v_ref, g_ref, y_ref, state_ref):
    h = pl.program_id(1)
    c = pl.program_id(2)
    cs = RET_CHUNK

    @pl.when(c == 0)
    def _():
        state_ref[...] = jnp.zeros_like(state_ref)

    lg = lg_ref[h]
    qi = lax.broadcasted_iota(I32, (cs, cs), 0)
    ki = lax.broadcasted_iota(I32, (cs, cs), 1)
    rel = (qi - ki).astype(F32)
    intra_decay = jnp.where(rel >= 0, jnp.exp(lg * jnp.maximum(rel, 0.0)), 0.0)
    pos_col = lax.broadcasted_iota(I32, (cs, 1), 0).astype(F32)
    query_decay = jnp.exp(lg * (pos_col + 1.0))
    key_decay = jnp.exp(lg * (cs - 1.0 - pos_col))
    chunk_decay = jnp.exp(jnp.full((1, 1), lg * cs, F32))

    q = q_ref[...]
    k = k_ref[...]
    v = v_ref[...]
    scale = RET_QK ** -0.5
    intra = lax.dot_general(q, k, (((1,), (1,)), ((), ())), preferred_element_type=F32) * scale * intra_decay
    state = state_ref[...]
    out = (jnp.dot(intra.astype(BF16), v, preferred_element_type=F32)
           + jnp.dot(q, state.astype(BF16), preferred_element_type=F32) * query_decay)
    kd = (k.astype(F32) * (scale * key_decay)).astype(BF16)
    state_ref[...] = chunk_decay * state + lax.dot_general(
        kd, v, (((0,), (0,)), ((), ())), preferred_element_type=F32)
    o = out * lax.rsqrt(jnp.mean(out * out, axis=-1, keepdims=True) + RET_EPS)
    g = g_ref[...]
    y_ref[...] = (g * jax.nn.sigmoid(g) * o).astype(y_ref.dtype)


def _retention_mixer(x2, bsz, seq, idx, b_w_in, layer):
    t = bsz * seq
    qk_w = RET_HEADS * RET_QK
    v_w = RET_HEADS * RET_V
    qkv = _matmul(x2, b_w_in, idx, 0, 2 * qk_w + v_w, BF16, f"ret_qkv_l{layer}")
    gate = _matmul(x2, b_w_in, idx, 2 * qk_w + v_w, v_w, F32, f"ret_gate_l{layer}")
    nc = seq // RET_CHUNK
    log_gamma = jnp.log1p(-jnp.exp2(-5.0 - jnp.arange(RET_HEADS, dtype=F32)))
    grid_spec = pltpu.PrefetchScalarGridSpec(
        num_scalar_prefetch=1,
        grid=(bsz, RET_HEADS, nc),
        in_specs=[pl.BlockSpec((RET_CHUNK, RET_QK), lambda b, h, c, lg: (b * nc + c, h)),
                  pl.BlockSpec((RET_CHUNK, RET_QK), lambda b, h, c, lg: (b * nc + c, RET_HEADS + h)),
                  pl.BlockSpec((RET_CHUNK, RET_V), lambda b, h, c, lg: (b * nc + c, RET_HEADS + h)),
                  pl.BlockSpec((RET_CHUNK, RET_V), lambda b, h, c, lg: (b * nc + c, h))],
        out_specs=pl.BlockSpec((RET_CHUNK, RET_V), lambda b, h, c, lg: (b * nc + c, h)),
        scratch_shapes=[pltpu.VMEM((RET_QK, RET_V), F32)],
    )
    return pl.pallas_call(
        _ret_body,
        grid_spec=grid_spec,
        out_shape=jax.ShapeDtypeStruct((t, v_w), BF16),
        compiler_params=_params(("parallel", "parallel", "arbitrary")),
        name=f"retention_l{layer}",
    )(log_gamma, qkv, qkv, qkv, gate)


def _attn_body(q_ref, kc_ref, kp_ref, vc_ref, vp_ref, o_ref, st_ref, *, dilation):
    n = pl.program_id(2)
    nw = q_ref.shape[0]
    dh = ATTN_HEAD_DIM
    qi = lax.broadcasted_iota(I32, (nw, nw), 0)
    ki = lax.broadcasted_iota(I32, (nw, nw), 1)
    steps_c = qi - ki
    steps_p = qi - ki + nw
    valid_c = steps_c >= 0
    valid_p = steps_p <= jnp.where(n > 0, nw, -1)
    dist_c = (steps_c * dilation).astype(F32)
    dist_p = (steps_p * dilation).astype(F32)
    q = q_ref[...]
    kc = kc_ref[...]
    kp = kp_ref[...]
    vc = vc_ref[...]
    vp = vp_ref[...]
    lane = lax.broadcasted_iota(I32, (nw, LANES), 1)
    stats = jnp.zeros((nw, LANES), F32)
    dn = (((1,), (1,)), ((), ()))
    for h in range(ATTN_HEADS):
        sl = slice(h * dh, (h + 1) * dh)
        slope = 2.0 ** (-8.0 * (h + 1) / ATTN_HEADS)
        qh = q[:, sl]
        sc_c = lax.dot_general(qh, kc[:, sl], dn, preferred_element_type=F32) * (dh ** -0.5)
        sc_p = lax.dot_general(qh, kp[:, sl], dn, preferred_element_type=F32) * (dh ** -0.5)
        sc_c = jnp.where(valid_c, sc_c - slope * dist_c, -jnp.inf)
        sc_p = jnp.where(valid_p, sc_p - slope * dist_p, -jnp.inf)
        m = jnp.maximum(jnp.max(sc_c, axis=-1, keepdims=True), jnp.max(sc_p, axis=-1, keepdims=True))
        pc = jnp.exp(sc_c - m)
        pp = jnp.exp(sc_p - m)
        ssum = jnp.sum(pc, axis=-1, keepdims=True) + jnp.sum(pp, axis=-1, keepdims=True)
        o_ref[:, sl] = (jnp.dot(pc.astype(BF16), vc[:, sl], preferred_element_type=F32)
                        + jnp.dot(pp.astype(BF16), vp[:, sl], preferred_element_type=F32))
        stats = jnp.where(lane == h, m, stats)
        stats = jnp.where(lane == ATTN_HEADS + h, ssum, stats)
    st_ref[...] = stats


def _merge_body(o0_ref, o1_ref, o2_ref, s0_ref, s1_ref, s2_ref, y_ref):
    tm = y_ref.shape[0]
    stats = [s0_ref[...], s1_ref[...], s2_ref[...]]
    lane = lax.broadcasted_iota(I32, (tm, LANES), 1)
    maxes = [s for s in stats]
    sums = [pltpu.roll(s, LANES - ATTN_HEADS, 1) for s in stats]
    m_all = jnp.maximum(jnp.maximum(maxes[0], maxes[1]), maxes[2])
    wgt = [jnp.exp(m - m_all) for m in maxes]
    den = wgt[0] * sums[0] + wgt[1] * sums[1] + wgt[2] * sums[2]
    er = lax.broadcasted_iota(I32, (LANES, D_MODEL), 0)
    ec = lax.broadcasted_iota(I32, (LANES, D_MODEL), 1)
    expand = (ec // ATTN_HEAD_DIM == er).astype(F32)
    acc = jnp.zeros((tm, D_MODEL), F32)
    for w, o_ref in zip(wgt, (o0_ref, o1_ref, o2_ref)):
        coef = jnp.where(lane < ATTN_HEADS, w / den, 0.0)
        acc = acc + jnp.dot(coef, expand, preferred_element_type=F32,
                            precision=lax.Precision.HIGHEST) * o_ref[...]
    y_ref[...] = acc


def _attention_mixer(x2, bsz, seq, idx, c_w_in, layer):
    t = bsz * seq
    w = D_MODEL
    n_groups = len(ATTN_PATTERNS)
    outs, stats = [], []
    for g, (window, dilation) in enumerate(ATTN_PATTERNS):
        nw = window // dilation
        sub = seq // dilation
        assert nw == 128 and sub % nw == 0
        nb = sub // nw

        def to_sub(a):
            return a.reshape(bsz, sub, dilation, -1).transpose(0, 2, 1, 3).reshape(t, -1)

        def from_sub(a):
            return a.reshape(bsz, dilation, sub, -1).transpose(0, 2, 1, 3).reshape(t, -1)

        xs = x2 if dilation == 1 else to_sub(x2)
        qkv = _matmul(xs, c_w_in, idx, g * w, 3 * w, BF16, f"attn_qkv_g{g}_l{layer}",
                      col_block_stride=n_groups)
        cur = lambda slab: pl.BlockSpec((nw, w), lambda b, r, n, slab=slab: ((b * dilation + r) * nb + n, slab))
        prev = lambda slab: pl.BlockSpec(
            (nw, w), lambda b, r, n, slab=slab: ((b * dilation + r) * nb + jnp.maximum(n - 1, 0), slab))
        o, st = pl.pallas_call(
            functools.partial(_attn_body, dilation=dilation),
            grid=(bsz, dilation, nb),
            in_specs=[cur(0), cur(1), prev(1), cur(2), prev(2)],
            out_specs=[pl.BlockSpec((nw, w), lambda b, r, n: ((b * dilation + r) * nb + n, 0)),
                       pl.BlockSpec((nw, LANES), lambda b, r, n: ((b * dilation + r) * nb + n, 0))],
            out_shape=[jax.ShapeDtypeStruct((t, w), F32), jax.ShapeDtypeStruct((t, LANES), F32)],
            compiler_params=_params(("parallel", "parallel", "parallel")),
            name=f"attn_g{g}_l{layer}",
        )(qkv, qkv, qkv, qkv, qkv)
        outs.append(o if dilation == 1 else from_sub(o))
        stats.append(st if dilation == 1 else from_sub(st))
    tm = ROW_TM
    return pl.pallas_call(
        _merge_body,
        grid=(t // tm,),
        in_specs=[pl.BlockSpec((tm, w), lambda i: (i, 0))] * 3 + [pl.BlockSpec((tm, LANES), lambda i: (i, 0))] * 3,
        out_specs=pl.BlockSpec((tm, w), lambda i: (i, 0)),
        out_shape=jax.ShapeDtypeStruct((t, w), F32),
        compiler_params=_params(("parallel",)),
        name=f"attn_merge_l{layer}",
    )(*outs, *stats)


def kernel(x, a_w_in, a_conv_w, a_conv_b, a_w_rgate, a_b_rgate, a_w_igate, a_b_igate, a_lambda, a_w_out,
           b_w_in, b_w_out, c_w_in, c_w_out, ln_gain, ln_bias, moe_w_router, moe_b_router, moe_w_gu,
           moe_b_gu, moe_w_down, moe_b_down):
    bsz, seq, d = x.shape
    t = bsz * seq
    x2 = x.reshape(t, d)
    b_gu4 = moe_b_gu[:, :, None, :]
    b_down4 = moe_b_down[:, :, None, :]
    for layer in range(DEPTH):
        kind, idx = layer % N_MIXERS, layer // N_MIXERS
        if kind == 0:
            y = _rglru_mixer(x2, bsz, seq, idx, a_w_in, a_conv_w, a_conv_b, a_w_rgate, a_b_rgate,
                             a_w_igate, a_b_igate, a_lambda, layer)
            w_out = a_w_out
        elif kind == 1:
            y = _retention_mixer(x2, bsz, seq, idx, b_w_in, layer)
            w_out = b_w_out
        else:
            y = _attention_mixer(x2, bsz, seq, idx, c_w_in, layer)
            w_out = c_w_out
        xt, route_f, route_i, counts = _post_mixer(
            y, w_out, idx, x2, ln_gain[layer, 0][None], ln_bias[layer, 0][None],
            moe_w_router[layer], moe_b_router[layer][None], f"post_mixer_l{layer}")
        block_exp, n_used, src_rows, dst_rows = _route_tables(route_i, counts, t)
        y_tk = _moe_experts(xt, block_exp, n_used, src_rows, dst_rows,
                            moe_w_gu, b_gu4, moe_w_down, b_down4, layer, t)
        x2 = _combine(y_tk, route_f, xt, ln_gain[layer, 1][None], ln_bias[layer, 1][None], t,
                      f"combine_l{layer}")
    return x2.reshape(bsz, seq, d)
```

```python
import functools
import math

import jax
import jax.numpy as jnp
from jax import lax
from jax.experimental import pallas as pl
from jax.experimental.pallas import tpu as pltpu

F32 = jnp.float32
BF16 = jnp.bfloat16
I32 = jnp.int32

D_MODEL = 1024
DEPTH = 4
N_MIXERS = 3
DEEPNORM_ALPHA = (2.0 * DEPTH) ** 0.25
LN_EPS = 1e-5
LRU_HEADS = 8
LRU_BLOCK = D_MODEL // LRU_HEADS
CONV_WIDTH = 4
LRU_C = 8.0
RET_HEADS = 4
RET_QK = D_MODEL // RET_HEADS
RET_V = 2 * RET_QK
RET_CHUNK = 128
RET_EPS = 1e-6
ATTN_PATTERNS = ((128, 1), (512, 4), (2048, 16))
ATTN_HEADS = 16
ATTN_HEAD_DIM = D_MODEL // ATTN_HEADS
N_EXPERTS = 32
TOP_K = 4
D_FF = D_MODEL
SWIGLU_LIMIT = 7.0
SWIGLU_ALPHA = 1.702

SUBLANES = 8
LANES = 128
LANE_TILES = D_MODEL // LANES
VMEM_LIMIT = 56 * 1024 * 1024

MOE_TM = 256
MOE_FF_CHUNK = 256
ROW_TM = 256


def _params(sem, vmem=VMEM_LIMIT):
    return pltpu.CompilerParams(dimension_semantics=sem, vmem_limit_bytes=vmem)


def _mm_body(x_ref, w_ref, o_ref, wbf_ref):
    @pl.when(pl.program_id(1) == 0)
    def _():
        wbf_ref[...] = w_ref[...].astype(BF16)

    o_ref[...] = jnp.dot(x_ref[...].astype(BF16), wbf_ref[...],
                         preferred_element_type=F32).astype(o_ref.dtype)


def _matmul(x, w_stack, idx, col0, ncols, out_dtype, name, tm=512, tn=1024, col_block_stride=1):
    t, k = x.shape
    assert col0 % tn == 0 and ncols % tn == 0 and t % tm == 0
    return pl.pallas_call(
        _mm_body,
        grid=(ncols // tn, t // tm),
        in_specs=[pl.BlockSpec((tm, k), lambda n, m: (m, 0)),
                  pl.BlockSpec((None, k, tn), lambda n, m: (idx, 0, col0 // tn + n * col_block_stride))],
        out_specs=pl.BlockSpec((tm, tn), lambda n, m: (m, n)),
        out_shape=jax.ShapeDtypeStruct((t, ncols), out_dtype),
        scratch_shapes=[pltpu.VMEM((k, tn), BF16)],
        compiler_params=_params(("arbitrary", "arbitrary")),
        name=name,
    )(x, w_stack)


def _layer_norm(z, gain, bias):
    mu = jnp.mean(z, axis=-1, keepdims=True)
    zc = z - mu
    var = jnp.mean(zc * zc, axis=-1, keepdims=True)
    return zc * lax.rsqrt(var + LN_EPS) * gain + bias


def _store_token_tiles(ref, val):
    tm = val.shape[0]
    for c in range(LANE_TILES):
        ref[pl.ds(c, tm, stride=LANE_TILES), :] = val[:, c * LANES:(c + 1) * LANES]


def _load_token_tiles(ref, tm):
    return jnp.concatenate([ref[pl.ds(c, tm, stride=LANE_TILES), :] for c in range(LANE_TILES)], axis=-1)


def _post_body(y_ref, w_ref, x_ref, g_ref, b_ref, wr_ref, br_ref,
               xt_ref, rf_ref, ri_ref, cnt_ref, wbf_ref, run_ref):
    i = pl.program_id(0)
    tm = x_ref.shape[0]

    @pl.when(i == 0)
    def _():
        wbf_ref[...] = w_ref[...].astype(BF16)
        run_ref[...] = jnp.zeros_like(run_ref)

    z = DEEPNORM_ALPHA * x_ref[...] + jnp.dot(y_ref[...].astype(BF16), wbf_ref[...],
                                              preferred_element_type=F32)
    xn = _layer_norm(z, g_ref[...], b_ref[...])
    _store_token_tiles(xt_ref, xn)

    logits = jnp.dot(xn, wr_ref[...], preferred_element_type=F32,
                     precision=lax.Precision.HIGHEST) + br_ref[...]
    lane_e = lax.broadcasted_iota(I32, (tm, N_EXPERTS), 1).astype(F32)
    work = logits
    top_l, top_e, onehots = [], [], []
    for _ in range(TOP_K):
        m = jnp.max(work, axis=-1, keepdims=True)
        idx = jnp.min(jnp.where(work == m, lane_e, float(N_EXPERTS)), axis=-1, keepdims=True)
        oh = lane_e == idx
        top_l.append(m)
        top_e.append(idx)
        onehots.append(oh)
        work = jnp.where(oh, -jnp.inf, work)
    exps = [jnp.exp(l - top_l[0]) for l in top_l]
    denom = exps[0] + exps[1] + exps[2] + exps[3]
    gates = [e / denom for e in exps]

    sel = jnp.zeros((tm, N_EXPERTS), F32)
    for oh in onehots:
        sel = sel + oh.astype(F32)
    ri = lax.broadcasted_iota(I32, (tm, tm), 0)
    ci = lax.broadcasted_iota(I32, (tm, tm), 1)
    tri = (ci < ri).astype(BF16)
    before = jnp.dot(tri, sel.astype(BF16), preferred_element_type=F32) + run_ref[...]
    ranks = [jnp.sum(jnp.where(oh, before, 0.0), axis=-1, keepdims=True) for oh in onehots]
    run_ref[...] = run_ref[...] + jnp.sum(sel, axis=0, keepdims=True)
    cnt_ref[...] = run_ref[...]

    lane = lax.broadcasted_iota(I32, (tm, LANES), 1)
    rf = jnp.zeros((tm, LANES), F32)
    rint = jnp.zeros((tm, LANES), I32)
    for k in range(TOP_K):
        rf = jnp.where(lane == k, gates[k], rf)
        rint = jnp.where(lane == k, top_e[k].astype(I32), rint)
        rint = jnp.where(lane == TOP_K + k, ranks[k].astype(I32), rint)
    rf_ref[...] = rf
    ri_ref[...] = rint


def _post_mixer(y, w_out_stack, idx, x, gain, bias, w_router, b_router, name):
    t, kd = y.shape
    tm = ROW_TM
    return pl.pallas_call(
        _post_body,
        grid=(t // tm,),
        in_specs=[pl.BlockSpec((tm, kd), lambda i: (i, 0)),
                  pl.BlockSpec((None, kd, D_MODEL), lambda i: (idx, 0, 0)),
                  pl.BlockSpec((tm, D_MODEL), lambda i: (i, 0)),
                  pl.BlockSpec((1, D_MODEL), lambda i: (0, 0)),
                  pl.BlockSpec((1, D_MODEL), lambda i: (0, 0)),
                  pl.BlockSpec((D_MODEL, N_EXPERTS), lambda i: (0, 0)),
                  pl.BlockSpec((1, N_EXPERTS), lambda i: (0, 0))],
        out_specs=[pl.BlockSpec((tm * LANE_TILES, LANES), lambda i: (i, 0)),
                   pl.BlockSpec((tm, LANES), lambda i: (i, 0)),
                   pl.BlockSpec((tm, LANES), lambda i: (i, 0)),
                   pl.BlockSpec((1, N_EXPERTS), lambda i: (0, 0))],
        out_shape=[jax.ShapeDtypeStruct((t * LANE_TILES, LANES), F32),
                   jax.ShapeDtypeStruct((t, LANES), F32),
                   jax.ShapeDtypeStruct((t, LANES), I32),
                   jax.ShapeDtypeStruct((1, N_EXPERTS), F32)],
        scratch_shapes=[pltpu.VMEM((kd, D_MODEL), BF16), pltpu.VMEM((1, N_EXPERTS), F32)],
        compiler_params=_params(("arbitrary",)),
        name=name,
    )(y, w_out_stack, x, gain, bias, w_router, b_router)


def _gather_copy(x_hbm, xbuf, sem, src_row, r):
    return pltpu.make_async_copy(
        x_hbm.at[pl.ds(pl.multiple_of(src_row, SUBLANES), SUBLANES), :],
        xbuf.at[pl.ds(pl.multiple_of(r * SUBLANES, SUBLANES), SUBLANES), :],
        sem)


def _scatter_copy(ybuf, y_hbm, sem, dst_row, r):
    return pltpu.make_async_copy(
        ybuf.at[pl.ds(pl.multiple_of(r * SUBLANES, SUBLANES), SUBLANES), :],
        y_hbm.at[pl.ds(pl.multiple_of(dst_row, SUBLANES), SUBLANES), :],
        sem)


def _moe_body(be_ref, nu_ref, src_ref, srcn_ref, dstp_ref, dst_ref, x_hbm, wgu_ref, bgu_ref, wd_ref, bd_ref,
              y_hbm, xbuf, ybuf, xbf, acc_ref, wg_bf, wu_bf, wd_bf, bg_s, bu_s, gsem, ssem):
    i = pl.program_id(0)
    nb = pl.num_programs(0)
    rows = MOE_TM * LANE_TILES
    gsem0 = gsem.at[0]
    ssem0 = ssem.at[0]

    def gather_loop(idx_ref):
        def body(r, c):
            _gather_copy(x_hbm, xbuf, gsem0, idx_ref[0, r], r).start()
            return c
        lax.fori_loop(0, MOE_TM, body, 0, unroll=8)

    def scatter_loop(idx_ref):
        def body(r, c):
            _scatter_copy(ybuf, y_hbm, ssem0, idx_ref[0, r], r).start()
            return c
        lax.fori_loop(0, MOE_TM, body, 0, unroll=8)

    def wait_gather():
        pltpu.make_async_copy(x_hbm.at[pl.ds(0, rows), :], xbuf, gsem0).wait()

    def wait_scatter():
        pltpu.make_async_copy(ybuf, y_hbm.at[pl.ds(0, rows), :], ssem0).wait()

    @pl.when(i == 0)
    def _():
        ybuf[...] = jnp.zeros_like(ybuf)
        tail = y_hbm.at[pl.ds(y_hbm.shape[0] - 2 * rows, 2 * rows), :]
        for s in range(2):
            cp = pltpu.make_async_copy(ybuf, tail.at[pl.ds(s * rows, rows), :], ssem0)
            cp.start()
            cp.wait()
        gather_loop(src_ref)

    wait_gather()
    used = i < nu_ref[0]

    @pl.when(used)
    def _():
        changed = jnp.logical_or(i == 0, be_ref[i] != be_ref[jnp.maximum(i - 1, 0)])

        n_chunks = D_FF // MOE_FF_CHUNK
        per_chunk = MOE_TM // n_chunks

        @pl.when(changed)
        def _():
            for j in range(n_chunks):
                lo, hi = j * MOE_FF_CHUNK, (j + 1) * MOE_FF_CHUNK
                wg_bf[j] = wgu_ref[:, lo:hi].astype(BF16)
                wu_bf[j] = wgu_ref[:, D_FF + lo:D_FF + hi].astype(BF16)
                wd_bf[j] = wd_ref[lo:hi, :].astype(BF16)
                bg_s[j] = bgu_ref[:, lo:hi]
                bu_s[j] = bgu_ref[:, D_FF + lo:D_FF + hi]

        xbf[...] = _load_token_tiles(xbuf, MOE_TM).astype(BF16)
        acc_ref[...] = jnp.broadcast_to(bd_ref[...], acc_ref.shape)

        def chunk(j, carry):
            base = j * per_chunk
            for r in range(per_chunk):
                _gather_copy(x_hbm, xbuf, gsem0, srcn_ref[0, base + r], base + r).start(priority=r % 2)
                _scatter_copy(ybuf, y_hbm, ssem0, dstp_ref[0, base + r], base + r).start(priority=(r + 1) % 2)
            x = xbf[...]
            gate = jnp.dot(x, wg_bf[j], preferred_element_type=F32) + bg_s[j]
            up = jnp.dot(x, wu_bf[j], preferred_element_type=F32) + bu_s[j]
            gate = jnp.minimum(gate, SWIGLU_LIMIT)
            up = jnp.clip(up, -SWIGLU_LIMIT, SWIGLU_LIMIT)
            hidden = (up + 1.0) * gate * jax.nn.sigmoid(SWIGLU_ALPHA * gate)
            acc_ref[...] += jnp.dot(hidden.astype(BF16), wd_bf[j], preferred_element_type=F32)
            return carry

        lax.fori_loop(0, n_chunks, chunk, 0)
        wait_scatter()
        _store_token_tiles(ybuf, acc_ref[...])

    @pl.when(jnp.logical_not(used))
    def _():
        gather_loop(srcn_ref)
        scatter_loop(dstp_ref)
        wait_scatter()

    @pl.when(i == nb - 1)
    def _():
        scatter_loop(dst_ref)
        wait_scatter()
        wait_gather()


def _moe_experts(xt, block_exp, n_used, src_rows, dst_rows, w_gu, b_gu, w_down, b_down, layer, n_tok):
    nb = block_exp.shape[0]
    rows = MOE_TM * LANE_TILES
    n_chunks = D_FF // MOE_FF_CHUNK
    n_out_rows = (TOP_K * n_tok + 2 * MOE_TM) * LANE_TILES
    smem_blk = lambda f: pl.BlockSpec((None, 1, MOE_TM), f, memory_space=pltpu.SMEM)
    grid_spec = pltpu.PrefetchScalarGridSpec(
        num_scalar_prefetch=2,
        grid=(nb,),
        in_specs=[smem_blk(lambda i, be, nu: (i, 0, 0)),
                  smem_blk(lambda i, be, nu: (jnp.minimum(i + 1, nb - 1), 0, 0)),
                  smem_blk(lambda i, be, nu: (jnp.maximum(i - 1, 0), 0, 0)),
                  smem_blk(lambda i, be, nu: (i, 0, 0)),
                  pl.BlockSpec(memory_space=pl.ANY),
                  pl.BlockSpec((None, None, D_MODEL, 2 * D_FF), lambda i, be, nu: (layer, be[i], 0, 0)),
                  pl.BlockSpec((None, None, 1, 2 * D_FF), lambda i, be, nu: (layer, be[i], 0, 0)),
                  pl.BlockSpec((None, None, D_FF, D_MODEL), lambda i, be, nu: (layer, be[i], 0, 0)),
                  pl.BlockSpec((None, None, 1, D_MODEL), lambda i, be, nu: (layer, be[i], 0, 0))],
        out_specs=pl.BlockSpec(memory_space=pl.ANY),
        scratch_shapes=[pltpu.VMEM((rows, LANES), F32),
                        pltpu.VMEM((rows, LANES), F32),
                        pltpu.VMEM((MOE_TM, D_MODEL), BF16),
                        pltpu.VMEM((MOE_TM, D_MODEL), F32),
                        pltpu.VMEM((n_chunks, D_MODEL, MOE_FF_CHUNK), BF16),
                        pltpu.VMEM((n_chunks, D_MODEL, MOE_FF_CHUNK), BF16),
                        pltpu.VMEM((n_chunks, MOE_FF_CHUNK, D_MODEL), BF16),
                        pltpu.VMEM((n_chunks, 1, MOE_FF_CHUNK), F32),
                        pltpu.VMEM((n_chunks, 1, MOE_FF_CHUNK), F32),
                        pltpu.SemaphoreType.DMA((1,)),
                        pltpu.SemaphoreType.DMA((1,))],
    )
    return pl.pallas_call(
        _moe_body,
        grid_spec=grid_spec,
        out_shape=jax.ShapeDtypeStruct((n_out_rows, LANES), F32),
        compiler_params=_params(("arbitrary",)),
        name=f"moe_experts_l{layer}",
    )(block_exp, n_used, src_rows, src_rows, dst_rows, dst_rows, xt, w_gu, b_gu, w_down, b_down)


def _route_tables(route_i, counts, n_tok):
    n_assign = n_tok * TOP_K
    nb = n_assign // MOE_TM + N_EXPERTS
    counts = counts.reshape(N_EXPERTS).astype(I32)
    padded = ((counts + MOE_TM - 1) // MOE_TM) * MOE_TM
    pad_end = jnp.cumsum(padded)
    pad_start = pad_end - padded
    eid = route_i[:, :TOP_K]
    rank = route_i[:, TOP_K:2 * TOP_K]
    dest = (pad_start[eid] + rank).reshape(n_assign)
    pad_slot = n_assign + jnp.arange(nb * MOE_TM, dtype=I32) % (2 * MOE_TM)
    row_assign = pad_slot.at[dest].set(jnp.arange(n_assign, dtype=I32), unique_indices=True)
    tok = row_assign // TOP_K
    src_rows = jnp.minimum(tok, n_tok - 1) * LANE_TILES
    dst_rows = jnp.where(row_assign < n_assign, (row_assign % TOP_K) * n_tok + tok, row_assign) * LANE_TILES
    block_start = jnp.arange(nb, dtype=I32) * MOE_TM
    block_exp = jnp.minimum(jnp.sum((pad_end[None, :] <= block_start[:, None]).astype(I32), axis=1),
                            N_EXPERTS - 1)
    n_used = (pad_end[-1:] // MOE_TM).astype(I32)
    return block_exp, n_used, src_rows.reshape(nb, 1, MOE_TM), dst_rows.reshape(nb, 1, MOE_TM)


def _combine_body(y0_ref, y1_ref, y2_ref, y3_ref, rf_ref, xt_ref, g_ref, b_ref, o_ref):
    tm = o_ref.shape[0]
    rf = rf_ref[...]
    parts = []
    for c in range(LANE_TILES):
        acc = DEEPNORM_ALPHA * xt_ref[pl.ds(c, tm, stride=LANE_TILES), :]
        for k, yk in enumerate((y0_ref, y1_ref, y2_ref, y3_ref)):
            acc = acc + rf[:, k:k + 1] * yk[pl.ds(c, tm, stride=LANE_TILES), :]
        parts.append(acc)
    z = jnp.concatenate(parts, axis=-1)
    o_ref[...] = _layer_norm(z, g_ref[...], b_ref[...])


def _combine(y_tk, route_f, xt, gain, bias, n_tok, name):
    tm = ROW_TM
    nblk = n_tok // tm
    yspec = lambda k: pl.BlockSpec((tm * LANE_TILES, LANES), lambda i: (k * nblk + i, 0))
    return pl.pallas_call(
        _combine_body,
        grid=(nblk,),
        in_specs=[yspec(0), yspec(1), yspec(2), yspec(3),
                  pl.BlockSpec((tm, LANES), lambda i: (i, 0)),
                  pl.BlockSpec((tm * LANE_TILES, LANES), lambda i: (i, 0)),
                  pl.BlockSpec((1, D_MODEL), lambda i: (0, 0)),
                  pl.BlockSpec((1, D_MODEL), lambda i: (0, 0))],
        out_specs=pl.BlockSpec((tm, D_MODEL), lambda i: (i, 0)),
        out_shape=jax.ShapeDtypeStruct((n_tok, D_MODEL), F32),
        compiler_params=_params(("parallel",)),
        name=name,
    )(y_tk, y_tk, y_tk, y_tk, route_f, xt, gain, bias)


def _gelu_tanh(x):
    return 0.5 * x * (1.0 + jnp.tanh(math.sqrt(2.0 / math.pi) * (x + 0.044715 * (x * x * x))))


def _lru_gate_body(rec_ref, prev_ref, cw_ref, cb_ref, wr_ref, br_ref, wi_ref, bi_ref, lam_ref,
                   a_ref, b_ref, ext_ref):
    s = pl.program_id(1)
    ts = rec_ref.shape[0]
    cur = rec_ref[...]
    ext_ref[0:SUBLANES, :] = jnp.where(s == 0, 0.0, prev_ref[...])
    ext_ref[SUBLANES:, :] = cur
    conv = cb_ref[...] + cw_ref[CONV_WIDTH - 1:CONV_WIDTH, :] * cur
    for j in range(CONV_WIDTH - 1):
        conv = conv + cw_ref[j:j + 1, :] * ext_ref[pl.ds(SUBLANES - (CONV_WIDTH - 1) + j, ts), :]
    xb = conv.astype(BF16)
    r_parts, i_parts = [], []
    for h in range(LRU_HEADS):
        xh = xb[:, h * LRU_BLOCK:(h + 1) * LRU_BLOCK]
        r_parts.append(jnp.dot(xh, wr_ref[h].astype(BF16), preferred_element_type=F32))
        i_parts.append(jnp.dot(xh, wi_ref[h].astype(BF16), preferred_element_type=F32))
    r = jax.nn.sigmoid(jnp.concatenate(r_parts, axis=-1) + br_ref[...])
    ig = jax.nn.sigmoid(jnp.concatenate(i_parts, axis=-1) + bi_ref[...])
    neg_lam = -lam_ref[...]
    softplus = jnp.maximum(neg_lam, 0.0) + jnp.log1p(jnp.exp(-jnp.abs(neg_lam)))
    log_a = -LRU_C * r * softplus
    a = jnp.exp(log_a)
    a_ref[...] = a
    one_minus_a2 = -jnp.tanh(log_a) * (a * a + 1.0)
    b_ref[...] = jnp.sqrt(one_minus_a2) * (ig * conv)


def _lru_scan_body(a_ref, b_ref, g_ref, y_ref):
    seq, ct = a_ref.shape
    row = lax.broadcasted_iota(I32, (SUBLANES, ct), 0)

    def step(gi, carry):
        off = pl.multiple_of(gi * SUBLANES, SUBLANES)
        a = a_ref[pl.ds(off, SUBLANES), :]
        b = b_ref[pl.ds(off, SUBLANES), :]
        for d in (1, 2, 4):
            keep = row >= d
            b = jnp.where(keep, a * pltpu.roll(b, d, 0) + b, b)
            a = jnp.where(keep, a * pltpu.roll(a, d, 0), a)
        h = a * carry + b
        y_ref[pl.ds(off, SUBLANES), :] = _gelu_tanh(g_ref[pl.ds(off, SUBLANES), :]) * h
        return jnp.broadcast_to(h[SUBLANES - 1:SUBLANES, :], (SUBLANES, ct))

    lax.fori_loop(0, seq // SUBLANES, step, jnp.zeros((SUBLANES, ct), F32))


def _rglru_mixer(x2, bsz, seq, idx, a_w_in, conv_w, conv_b, w_rgate, b_rgate, w_igate, b_igate, lam, layer):
    t = bsz * seq
    w = D_MODEL
    u = _matmul(x2, a_w_in, idx, 0, 2 * w, F32, f"lru_in_l{layer}")
    ts = 256
    nst = seq // ts
    vec = lambda: pl.BlockSpec((1, w), lambda b, s: (0, 0))
    a, bb = pl.pallas_call(
        _lru_gate_body,
        grid=(bsz, nst),
        in_specs=[pl.BlockSpec((ts, w), lambda b, s: (b * nst + s, 1)),
                  pl.BlockSpec((SUBLANES, w),
                               lambda b, s: (jnp.maximum((b * nst + s) * (ts // SUBLANES) - 1, 0), 1)),
                  pl.BlockSpec((CONV_WIDTH, w), lambda b, s: (0, 0)),
                  vec(),
                  pl.BlockSpec((LRU_HEADS, LRU_BLOCK, LRU_BLOCK), lambda b, s: (0, 0, 0)),
                  vec(),
                  pl.BlockSpec((LRU_HEADS, LRU_BLOCK, LRU_BLOCK), lambda b, s: (0, 0, 0)),
                  vec(), vec()],
        out_specs=[pl.BlockSpec((ts, w), lambda b, s: (b * nst + s, 0)),
                   pl.BlockSpec((ts, w), lambda b, s: (b * nst + s, 0))],
        out_shape=[jax.ShapeDtypeStruct((t, w), F32), jax.ShapeDtypeStruct((t, w), F32)],
        scratch_shapes=[pltpu.VMEM((ts + SUBLANES, w), F32)],
        compiler_params=_params(("parallel", "parallel")),
        name=f"lru_gates_l{layer}",
    )(u, u, conv_w[idx], conv_b[idx][None], w_rgate[idx], b_rgate[idx][None],
      w_igate[idx], b_igate[idx][None], lam[idx][None])
    ct = 256
    return pl.pallas_call(
        _lru_scan_body,
        grid=(bsz, w // ct),
        in_specs=[pl.BlockSpec((seq, ct), lambda b, j: (b, j)),
                  pl.BlockSpec((seq, ct), lambda b, j: (b, j)),
                  pl.BlockSpec((seq, ct), lambda b, j: (b, j))],
        out_specs=pl.BlockSpec((seq, ct), lambda b, j: (b, j)),
        out_shape=jax.ShapeDtypeStruct((t, w), F32),
        compiler_params=_params(("parallel", "parallel")),
        name=f"lru_scan_l{layer}",
    )(a, bb, u)


def _ret_body(lg_ref, q_ref, k_ref, v_ref, g_ref, y_ref, state_ref):
    h = pl.program_id(1)
    c = pl.program_id(2)
    cs = RET_CHUNK

    @pl.when(c == 0)
    def _():
        state_ref[...] = jnp.zeros_like(state_ref)

    lg = lg_ref[h]
    qi = lax.broadcasted_iota(I32, (cs, cs), 0)
    ki = lax.broadcasted_iota(I32, (cs, cs), 1)
    rel = (qi - ki).astype(F32)
    intra_decay = jnp.where(rel >= 0, jnp.exp(lg * jnp.maximum(rel, 0.0)), 0.0)
    pos_col = lax.broadcasted_iota(I32, (cs, 1), 0).astype(F32)
    query_decay = jnp.exp(lg * (pos_col + 1.0))
    key_decay = jnp.exp(lg * (cs - 1.0 - pos_col))
    chunk_decay = jnp.exp(jnp.full((1, 1), lg * cs, F32))

    q = q_ref[...]
    k = k_ref[...]
    v = v_ref[...]
    scale = RET_QK ** -0.5
    intra = lax.dot_general(q, k, (((1,), (1,)), ((), ())), preferred_element_type=F32) * scale * intra_decay
    state = state_ref[...]
    out = (jnp.dot(intra.astype(BF16), v, preferred_element_type=F32)
           + jnp.dot(q, state.astype(BF16), preferred_element_type=F32) * query_decay)
    kd = (k.astype(F32) * (scale * key_decay)).astype(BF16)
    state_ref[...] = chunk_decay * state + lax.dot_general(
        kd, v, (((0,), (0,)), ((), ())), preferred_element_type=F32)
    o = out * lax.rsqrt(jnp.mean(out * out, axis=-1, keepdims=True) + RET_EPS)
    g = g_ref[...]
    y_ref[...] = (g * jax.nn.sigmoid(g) * o).astype(y_ref.dtype)


def _retention_mixer(x2, bsz, seq, idx, b_w_in, layer):
    t = bsz * seq
    qk_w = RET_HEADS * RET_QK
    v_w = RET_HEADS * RET_V
    qkv = _matmul(x2, b_w_in, idx, 0, 2 * qk_w + v_w, BF16, f"ret_qkv_l{layer}")
    gate = _matmul(x2, b_w_in, idx, 2 * qk_w + v_w, v_w, F32, f"ret_gate_l{layer}")
    nc = seq // RET_CHUNK
    log_gamma = jnp.log1p(-jnp.exp2(-5.0 - jnp.arange(RET_HEADS, dtype=F32)))
    grid_spec = pltpu.PrefetchScalarGridSpec(
        num_scalar_prefetch=1,
        grid=(bsz, RET_HEADS, nc),
        in_specs=[pl.BlockSpec((RET_CHUNK, RET_QK), lambda b, h, c, lg: (b * nc + c, h)),
                  pl.BlockSpec((RET_CHUNK, RET_QK), lambda b, h, c, lg: (b * nc + c, RET_HEADS + h)),
                  pl.BlockSpec((RET_CHUNK, RET_V), lambda b, h, c, lg: (b * nc + c, RET_HEADS + h)),
                  pl.BlockSpec((RET_CHUNK, RET_V), lambda b, h, c, lg: (b * nc + c, h))],
        out_specs=pl.BlockSpec((RET_CHUNK, RET_V), lambda b, h, c, lg: (b * nc + c, h)),
        scratch_shapes=[pltpu.VMEM((RET_QK, RET_V), F32)],
    )
    return pl.pallas_call(
        _ret_body,
        grid_spec=grid_spec,
        out_shape=jax.ShapeDtypeStruct((t, v_w), BF16),
        compiler_params=_params(("parallel", "parallel", "arbitrary")),
        name=f"retention_l{layer}",
    )(log_gamma, qkv, qkv, qkv, gate)


def _attn_body(q_ref, kc_ref, kp_ref, vc_ref, vp_ref, o_ref, st_ref, *, dilation):
    n = pl.program_id(2)
    nw = q_ref.shape[0]
    dh = ATTN_HEAD_DIM
    qi = lax.broadcasted_iota(I32, (nw, nw), 0)
    ki = lax.broadcasted_iota(I32, (nw, nw), 1)
    steps_c = qi - ki
    steps_p = qi - ki + nw
    valid_c = steps_c >= 0
    valid_p = steps_p <= jnp.where(n > 0, nw, -1)
    dist_c = (steps_c * dilation).astype(F32)
    dist_p = (steps_p * dilation).astype(F32)
    q = q_ref[...]
    kc = kc_ref[...]
    kp = kp_ref[...]
    vc = vc_ref[...]
    vp = vp_ref[...]
    lane = lax.broadcasted_iota(I32, (nw, LANES), 1)
    stats = jnp.zeros((nw, LANES), F32)
    dn = (((1,), (1,)), ((), ()))
    for h in range(ATTN_HEADS):
        sl = slice(h * dh, (h + 1) * dh)
        slope = 2.0 ** (-8.0 * (h + 1) / ATTN_HEADS)
        qh = q[:, sl]
        sc_c = lax.dot_general(qh, kc[:, sl], dn, preferred_element_type=F32) * (dh ** -0.5)
        sc_p = lax.dot_general(qh, kp[:, sl], dn, preferred_element_type=F32) * (dh ** -0.5)
        sc_c = jnp.where(valid_c, sc_c - slope * dist_c, -jnp.inf)
        sc_p = jnp.where(valid_p, sc_p - slope * dist_p, -jnp.inf)
        m = jnp.maximum(jnp.max(sc_c, axis=-1, keepdims=True), jnp.max(sc_p, axis=-1, keepdims=True))
        pc = jnp.exp(sc_c - m)
        pp = jnp.exp(sc_p - m)
        ssum = jnp.sum(pc, axis=-1, keepdims=True) + jnp.sum(pp, axis=-1, keepdims=True)
        o_ref[:, sl] = (jnp.dot(pc.astype(BF16), vc[:, sl], preferred_element_type=F32)
                        + jnp.dot(pp.astype(BF16), vp[:, sl], preferred_element_type=F32))
        stats = jnp.where(lane == h, m, stats)
        stats = jnp.where(lane == ATTN_HEADS + h, ssum, stats)
    st_ref[...] = stats


def _merge_body(o0_ref, o1_ref, o2_ref, s0_ref, s1_ref, s2_ref, y_ref):
    tm = y_ref.shape[0]
    stats = [s0_ref[...], s1_ref[...], s2_ref[...]]
    lane = lax.broadcasted_iota(I32, (tm, LANES), 1)
    maxes = [s for s in stats]
    sums = [pltpu.roll(s, LANES - ATTN_HEADS, 1) for s in stats]
    m_all = jnp.maximum(jnp.maximum(maxes[0], maxes[1]), maxes[2])
    wgt = [jnp.exp(m - m_all) for m in maxes]
    den = wgt[0] * sums[0] + wgt[1] * sums[1] + wgt[2] * sums[2]
    er = lax.broadcasted_iota(I32, (LANES, D_MODEL), 0)
    ec = lax.broadcasted_iota(I32, (LANES, D_MODEL), 1)
    expand = (ec // ATTN_HEAD_DIM == er).astype(F32)
    acc = jnp.zeros((tm, D_MODEL), F32)
    for w, o_ref in zip(wgt, (o0_ref, o1_ref, o2_ref)):
        coef = jnp.where(lane < ATTN_HEADS, w / den, 0.0)
        acc = acc + jnp.dot(coef, expand, preferred_element_type=F32,
                            precision=lax.Precision.HIGHEST) * o_ref[...]
    y_ref[...] = acc


def _attention_mixer(x2, bsz, seq, idx, c_w_in, layer):
    t = bsz * seq
    w = D_MODEL
    n_groups = len(ATTN_PATTERNS)
    outs, stats = [], []
    for g, (window, dilation) in enumerate(ATTN_PATTERNS):
        nw = window // dilation
        sub = seq // dilation
        assert nw == 128 and sub % nw == 0
        nb = sub // nw

        def to_sub(a):
            return a.reshape(bsz, sub, dilation, -1).transpose(0, 2, 1, 3).reshape(t, -1)

        def from_sub(a):
            return a.reshape(bsz, dilation, sub, -1).transpose(0, 2, 1, 3).reshape(t, -1)

        xs = x2 if dilation == 1 else to_sub(x2)
        qkv = _matmul(xs, c_w_in, idx, g * w, 3 * w, BF16, f"attn_qkv_g{g}_l{layer}",
                      col_block_stride=n_groups)
        cur = lambda slab: pl.BlockSpec((nw, w), lambda b, r, n, slab=slab: ((b * dilation + r) * nb + n, slab))
        prev = lambda slab: pl.BlockSpec(
            (nw, w), lambda b, r, n, slab=slab: ((b * dilation + r) * nb + jnp.maximum(n - 1, 0), slab))
        o, st = pl.pallas_call(
            functools.partial(_attn_body, dilation=dilation),
            grid=(bsz, dilation, nb),
            in_specs=[cur(0), cur(1), prev(1), cur(2), prev(2)],
            out_specs=[pl.BlockSpec((nw, w), lambda b, r, n: ((b * dilation + r) * nb + n, 0)),
                       pl.BlockSpec((nw, LANES), lambda b, r, n: ((b * dilation + r) * nb + n, 0))],
            out_shape=[jax.ShapeDtypeStruct((t, w), F32), jax.ShapeDtypeStruct((t, LANES), F32)],
            compiler_params=_params(("parallel", "parallel", "parallel")),
            name=f"attn_g{g}_l{layer}",
        )(qkv, qkv, qkv, qkv, qkv)
        outs.append(o if dilation == 1 else from_sub(o))
        stats.append(st if dilation == 1 else from_sub(st))
    tm = ROW_TM
    return pl.pallas_call(
        _merge_body,
        grid=(t // tm,),
        in_specs=[pl.BlockSpec((tm, w), lambda i: (i, 0))] * 3 + [pl.BlockSpec((tm, LANES), lambda i: (i, 0))] * 3,
        out_specs=pl.BlockSpec((tm, w), lambda i: (i, 0)),
        out_shape=jax.ShapeDtypeStruct((t, w), F32),
        compiler_params=_params(("parallel",)),
        name=f"attn_merge_l{layer}",
    )(*outs, *stats)


def kernel(x, a_w_in, a_conv_w, a_conv_b, a_w_rgate, a_b_rgate, a_w_igate, a_b_igate, a_lambda, a_w_out,
           b_w_in, b_w_out, c_w_in, c_w_out, ln_gain, ln_bias, moe_w_router, moe_b_router, moe_w_gu,
           moe_b_gu, moe_w_down, moe_b_down):
    bsz, seq, d = x.shape
    t = bsz * seq
    x2 = x.reshape(t, d)
    b_gu4 = moe_b_gu[:, :, None, :]
    b_down4 = moe_b_down[:, :, None, :]
    for layer in range(DEPTH):
        kind, idx = layer % N_MIXERS, layer // N_MIXERS
        if kind == 0:
            y = _rglru_mixer(x2, bsz, seq, idx, a_w_in, a_conv_w, a_conv_b, a_w_rgate, a_b_rgate,
                             a_w_igate, a_b_igate, a_lambda, layer)
            w_out = a_w_out
        elif kind == 1:
            y = _retention_mixer(x2, bsz, seq, idx, b_w_in, layer)
            w_out = b_w_out
        else:
            y = _attention_mixer(x2, bsz, seq, idx, c_w_in, layer)
            w_out = c_w_out
        xt, route_f, route_i, counts = _post_mixer(
            y, w_out, idx, x2, ln_gain[layer, 0][None], ln_bias[layer, 0][None],
            moe_w_router[layer], moe_b_router[layer][None], f"post_mixer_l{layer}")
        block_exp, n_used, src_rows, dst_rows = _route_tables(route_i, counts, t)
        y_tk = _moe_experts(xt, block_exp, n_used, src_rows, dst_rows,
                            moe_w_gu, b_gu4, moe_w_down, b_down4, layer, t)
        x2 = _combine(y_tk, route_f, xt, ln_gain[layer, 1][None], ln_bias[layer, 1][None], t,
                      f"combine_l{layer}")
    return x2.reshape(bsz, seq, d)
```

```python
import functools
import math

import jax
import jax.numpy as jnp
from jax import lax
from jax.experimental import pallas as pl
from jax.experimental.pallas import tpu as pltpu

F32 = jnp.float32
BF16 = jnp.bfloat16
I32 = jnp.int32

D_MODEL = 1024
DEPTH = 4
N_MIXERS = 3
DEEPNORM_ALPHA = (2.0 * DEPTH) ** 0.25
LN_EPS = 1e-5
LRU_HEADS = 8
LRU_BLOCK = D_MODEL // LRU_HEADS
CONV_WIDTH = 4
LRU_C = 8.0
RET_HEADS = 4
RET_QK = D_MODEL // RET_HEADS
RET_V = 2 * RET_QK
RET_CHUNK = 128
RET_EPS = 1e-6
ATTN_PATTERNS = ((128, 1), (512, 4), (2048, 16))
ATTN_HEADS = 16
ATTN_HEAD_DIM = D_MODEL // ATTN_HEADS
N_EXPERTS = 32
TOP_K = 4
D_FF = D_MODEL
SWIGLU_LIMIT = 7.0
SWIGLU_ALPHA = 1.702

SUBLANES = 8
LANES = 128
LANE_TILES = D_MODEL // LANES
VMEM_LIMIT = 56 * 1024 * 1024

MOE_TM = 256
MOE_FF_CHUNK = 256
ROW_TM = 256


def _params(sem, vmem=VMEM_LIMIT):
    return pltpu.CompilerParams(dimension_semantics=sem, vmem_limit_bytes=vmem)


def _mm_body(x_ref, w_ref, o_ref, wbf_ref):
    @pl.when(pl.program_id(1) == 0)
    def _():
        wbf_ref[...] = w_ref[...].astype(BF16)

    o_ref[...] = jnp.dot(x_ref[...].astype(BF16), wbf_ref[...],
                         preferred_element_type=F32).astype(o_ref.dtype)


def _matmul(x, w_stack, idx, col0, ncols, out_dtype, name, tm=512, tn=1024, col_block_stride=1):
    t, k = x.shape
    assert col0 % tn == 0 and ncols % tn == 0 and t % tm == 0
    return pl.pallas_call(
        _mm_body,
        grid=(ncols // tn, t // tm),
        in_specs=[pl.BlockSpec((tm, k), lambda n, m: (m, 0)),
                  pl.BlockSpec((None, k, tn), lambda n, m: (idx, 0, col0 // tn + n * col_block_stride))],
        out_specs=pl.BlockSpec((tm, tn), lambda n, m: (m, n)),
        out_shape=jax.ShapeDtypeStruct((t, ncols), out_dtype),
        scratch_shapes=[pltpu.VMEM((k, tn), BF16)],
        compiler_params=_params(("arbitrary", "arbitrary")),
        name=name,
    )(x, w_stack)


def _layer_norm(z, gain, bias):
    mu = jnp.mean(z, axis=-1, keepdims=True)
    zc = z - mu
    var = jnp.mean(zc * zc, axis=-1, keepdims=True)
    return zc * lax.rsqrt(var + LN_EPS) * gain + bias


def _store_token_tiles(ref, val):
    tm = val.shape[0]
    for c in range(LANE_TILES):
        ref[pl.ds(c, tm, stride=LANE_TILES), :] = val[:, c * LANES:(c + 1) * LANES]


def _load_token_tiles(ref, tm):
    return jnp.concatenate([ref[pl.ds(c, tm, stride=LANE_TILES), :] for c in range(LANE_TILES)], axis=-1)


def _post_body(y_ref, w_ref, x_ref, g_ref, b_ref, wr_ref, br_ref,
               xt_ref, rf_ref, ri_ref, cnt_ref, wbf_ref, run_ref):
    i = pl.program_id(0)
    tm = x_ref.shape[0]

    @pl.when(i == 0)
    def _():
        wbf_ref[...] = w_ref[...].astype(BF16)
        run_ref[...] = jnp.zeros_like(run_ref)

    z = DEEPNORM_ALPHA * x_ref[...] + jnp.dot(y_ref[...].astype(BF16), wbf_ref[...],
                                              preferred_element_type=F32)
    xn = _layer_norm(z, g_ref[...], b_ref[...])
    _store_token_tiles(xt_ref, xn)

    logits = jnp.dot(xn, wr_ref[...], preferred_element_type=F32,
                     precision=lax.Precision.HIGHEST) + br_ref[...]
    lane_e = lax.broadcasted_iota(I32, (tm, N_EXPERTS), 1).astype(F32)
    work = logits
    top_l, top_e, onehots = [], [], []
    for _ in range(TOP_K):
        m = jnp.max(work, axis=-1, keepdims=True)
        idx = jnp.min(jnp.where(work == m, lane_e, float(N_EXPERTS)), axis=-1, keepdims=True)
        oh = lane_e == idx
        top_l.append(m)
        top_e.append(idx)
        onehots.append(oh)
        work = jnp.where(oh, -jnp.inf, work)
    exps = [jnp.exp(l - top_l[0]) for l in top_l]
    denom = exps[0] + exps[1] + exps[2] + exps[3]
    gates = [e / denom for e in exps]

    sel = jnp.zeros((tm, N_EXPERTS), F32)
    for oh in onehots:
        sel = sel + oh.astype(F32)
    ri = lax.broadcasted_iota(I32, (tm, tm), 0)
    ci = lax.broadcasted_iota(I32, (tm, tm), 1)
    tri = (ci < ri).astype(BF16)
    before = jnp.dot(tri, sel.astype(BF16), preferred_element_type=F32) + run_ref[...]
    ranks = [jnp.sum(jnp.where(oh, before, 0.0), axis=-1, keepdims=True) for oh in onehots]
    run_ref[...] = run_ref[...] + jnp.sum(sel, axis=0, keepdims=True)
    cnt_ref[...] = run_ref[...]

    lane = lax.broadcasted_iota(I32, (tm, LANES), 1)
    rf = jnp.zeros((tm, LANES), F32)
    rint = jnp.zeros((tm, LANES), I32)
    for k in range(TOP_K):
        rf = jnp.where(lane == k, gates[k], rf)
        rint = jnp.where(lane == k, top_e[k].astype(I32), rint)
        rint = jnp.where(lane == TOP_K + k, ranks[k].astype(I32), rint)
    rf_ref[...] = rf
    ri_ref[...] = rint


def _post_mixer(y, w_out_stack, idx, x, gain, bias, w_router, b_router, name):
    t, kd = y.shape
    tm = ROW_TM
    return pl.pallas_call(
        _post_body,
        grid=(t // tm,),
        in_specs=[pl.BlockSpec((tm, kd), lambda i: (i, 0)),
                  pl.BlockSpec((None, kd, D_MODEL), lambda i: (idx, 0, 0)),
                  pl.BlockSpec((tm, D_MODEL), lambda i: (i, 0)),
                  pl.BlockSpec((1, D_MODEL), lambda i: (0, 0)),
                  pl.BlockSpec((1, D_MODEL), lambda i: (0, 0)),
                  pl.BlockSpec((D_MODEL, N_EXPERTS), lambda i: (0, 0)),
                  pl.BlockSpec((1, N_EXPERTS), lambda i: (0, 0))],
        out_specs=[pl.BlockSpec((tm * LANE_TILES, LANES), lambda i: (i, 0)),
                   pl.BlockSpec((tm, LANES), lambda i: (i, 0)),
                   pl.BlockSpec((tm, LANES), lambda i: (i, 0)),
                   pl.BlockSpec((1, N_EXPERTS), lambda i: (0, 0))],
        out_shape=[jax.ShapeDtypeStruct((t * LANE_TILES, LANES), F32),
                   jax.ShapeDtypeStruct((t, LANES), F32),
                   jax.ShapeDtypeStruct((t, LANES), I32),
                   jax.ShapeDtypeStruct((1, N_EXPERTS), F32)],
        scratch_shapes=[pltpu.VMEM((kd, D_MODEL), BF16), pltpu.VMEM((1, N_EXPERTS), F32)],
        compiler_params=_params(("arbitrary",)),
        name=name,
    )(y, w_out_stack, x, gain, bias, w_router, b_router)


def _gather_copy(x_hbm, xbuf, sem, src_row, r):
    return pltpu.make_async_copy(
        x_hbm.at[pl.ds(pl.multiple_of(src_row, SUBLANES), SUBLANES), :],
        xbuf.at[pl.ds(pl.multiple_of(r * SUBLANES, SUBLANES), SUBLANES), :],
        sem)


def _scatter_copy(ybuf, y_hbm, sem, dst_row, r):
    return pltpu.make_async_copy(
        ybuf.at[pl.ds(pl.multiple_of(r * SUBLANES, SUBLANES), SUBLANES), :],
        y_hbm.at[pl.ds(pl.multiple_of(dst_row, SUBLANES), SUBLANES), :],
        sem)


def _moe_body(be_ref, nu_ref, src_ref, srcn_ref, dstp_ref, dst_ref, x_hbm, wgu_ref, bgu_ref, wd_ref, bd_ref,
              y_hbm, xbuf, ybuf, xbf, acc_ref, wg_bf, wu_bf, wd_bf, bg_s, bu_s, gsem, ssem):
    i = pl.program_id(0)
    nb = pl.num_programs(0)
    rows = MOE_TM * LANE_TILES
    slot = i % 2
    other = 1 - slot
    n_chunks = D_FF // MOE_FF_CHUNK
    n_groups = 2 * n_chunks
    per_group = MOE_TM // n_groups

    def gather_group(idx_ref, s, g):
        for q in range(per_group):
            r = g * per_group + q
            _gather_copy(x_hbm, xbuf.at[s], gsem.at[s], idx_ref[0, r], r).start(priority=q % 2)

    def scatter_group(idx_ref, s, g):
        for q in range(per_group):
            r = g * per_group + q
            _scatter_copy(ybuf.at[s], y_hbm, ssem.at[s], idx_ref[0, r], r).start(priority=q % 2)

    def gather_all(idx_ref, s):
        lax.fori_loop(0, n_groups, lambda g, c: (gather_group(idx_ref, s, g), c)[1], 0)

    def scatter_all(idx_ref, s):
        lax.fori_loop(0, n_groups, lambda g, c: (scatter_group(idx_ref, s, g), c)[1], 0)

    def wait_gather(s):
        pltpu.make_async_copy(x_hbm.at[pl.ds(0, rows), :], xbuf.at[s], gsem.at[s]).wait()

    def wait_scatter(s):
        pltpu.make_async_copy(ybuf.at[s], y_hbm.at[pl.ds(0, rows), :], ssem.at[s]).wait()

    @pl.when(i == 0)
    def _():
        ybuf[...] = jnp.zeros_like(ybuf)
        tail = y_hbm.at[pl.ds(y_hbm.shape[0] - 2 * rows, 2 * rows), :]
        for s in range(2):
            cp = pltpu.make_async_copy(ybuf.at[s], tail.at[pl.ds(s * rows, rows), :], ssem.at[s])
            cp.start()
            cp.wait()
        gather_all(src_ref, 0)

    wait_gather(slot)
    used = i < nu_ref[0]

    @pl.when(used)
    def _():
        changed = jnp.logical_or(i == 0, be_ref[i] != be_ref[jnp.maximum(i - 1, 0)])

        @pl.when(changed)
        def _():
            for j in range(n_chunks):
                lo, hi = j * MOE_FF_CHUNK, (j + 1) * MOE_FF_CHUNK
                wg_bf[j] = wgu_ref[:, lo:hi].astype(BF16)
                wu_bf[j] = wgu_ref[:, D_FF + lo:D_FF + hi].astype(BF16)
                wd_bf[j] = wd_ref[lo:hi, :].astype(BF16)
                bg_s[j] = bgu_ref[:, lo:hi]
                bu_s[j] = bgu_ref[:, D_FF + lo:D_FF + hi]

        xbf[...] = _load_token_tiles(xbuf.at[slot], MOE_TM).astype(BF16)
        acc_ref[...] = jnp.broadcast_to(bd_ref[...], acc_ref.shape)

        def compute_chunk(j):
            x = xbf[...]
            gate = jnp.dot(x, wg_bf[j], preferred_element_type=F32) + bg_s[j]
            up = jnp.dot(x, wu_bf[j], preferred_element_type=F32) + bu_s[j]
            gate = jnp.minimum(gate, SWIGLU_LIMIT)
            up = jnp.clip(up, -SWIGLU_LIMIT, SWIGLU_LIMIT)
            hidden = (up + 1.0) * gate * jax.nn.sigmoid(SWIGLU_ALPHA * gate)
            acc_ref[...] += jnp.dot(hidden.astype(BF16), wd_bf[j], preferred_element_type=F32)

        half = n_chunks // 2
        groups_per_chunk = n_groups // half

        def gather_chunk(j, carry):
            for q in range(groups_per_chunk):
                gather_group(srcn_ref, other, j * groups_per_chunk + q)
            compute_chunk(j)
            return carry

        def scatter_chunk(j, carry):
            for q in range(groups_per_chunk):
                scatter_group(dstp_ref, other, (j - half) * groups_per_chunk + q)
            compute_chunk(j)
            return carry

        lax.fori_loop(0, half, gather_chunk, 0)
        lax.fori_loop(half, n_chunks, scatter_chunk, 0)

        @pl.when(i >= 1)
        def _():
            wait_scatter(slot)

        _store_token_tiles(ybuf.at[slot], acc_ref[...])

    @pl.when(jnp.logical_not(used))
    def _():
        gather_all(srcn_ref, other)
        scatter_all(dstp_ref, other)

        @pl.when(i >= 1)
        def _():
            wait_scatter(slot)

    @pl.when(i == nb - 1)
    def _():
        scatter_all(dst_ref, slot)
        wait_scatter(other)
        wait_scatter(slot)
        wait_gather(other)


def _moe_experts(xt, block_exp, n_used, src_rows, dst_rows, w_gu, b_gu, w_down, b_down, layer, n_tok):
    nb = block_exp.shape[0]
    rows = MOE_TM * LANE_TILES
    n_chunks = D_FF // MOE_FF_CHUNK
    n_out_rows = (TOP_K * n_tok + 2 * MOE_TM) * LANE_TILES
    smem_blk = lambda f: pl.BlockSpec((None, 1, MOE_TM), f, memory_space=pltpu.SMEM)
    grid_spec = pltpu.PrefetchScalarGridSpec(
        num_scalar_prefetch=2,
        grid=(nb,),
        in_specs=[smem_blk(lambda i, be, nu: (i, 0, 0)),
                  smem_blk(lambda i, be, nu: (i + 1, 0, 0)),
                  smem_blk(lambda i, be, nu: (i, 0, 0)),
                  smem_blk(lambda i, be, nu: (i + 1, 0, 0)),
                  pl.BlockSpec(memory_space=pl.ANY),
                  pl.BlockSpec((None, None, D_MODEL, 2 * D_FF), lambda i, be, nu: (layer, be[i], 0, 0)),
                  pl.BlockSpec((None, None, 1, 2 * D_FF), lambda i, be, nu: (layer, be[i], 0, 0)),
                  pl.BlockSpec((None, None, D_FF, D_MODEL), lambda i, be, nu: (layer, be[i], 0, 0)),
                  pl.BlockSpec((None, None, 1, D_MODEL), lambda i, be, nu: (layer, be[i], 0, 0))],
        out_specs=pl.BlockSpec(memory_space=pl.ANY),
        scratch_shapes=[pltpu.VMEM((2, rows, LANES), F32),
                        pltpu.VMEM((2, rows, LANES), F32),
                        pltpu.VMEM((MOE_TM, D_MODEL), BF16),
                        pltpu.VMEM((MOE_TM, D_MODEL), F32),
                        pltpu.VMEM((n_chunks, D_MODEL, MOE_FF_CHUNK), BF16),
                        pltpu.VMEM((n_chunks, D_MODEL, MOE_FF_CHUNK), BF16),
                        pltpu.VMEM((n_chunks, MOE_FF_CHUNK, D_MODEL), BF16),
                        pltpu.VMEM((n_chunks, 1, MOE_FF_CHUNK), F32),
                        pltpu.VMEM((n_chunks, 1, MOE_FF_CHUNK), F32),
                        pltpu.SemaphoreType.DMA((2,)),
                        pltpu.SemaphoreType.DMA((2,))],
    )
    return pl.pallas_call(
        _moe_body,
        grid_spec=grid_spec,
        out_shape=jax.ShapeDtypeStruct((n_out_rows, LANES), F32),
        compiler_params=_params(("arbitrary",)),
        name=f"moe_experts_l{layer}",
    )(block_exp, n_used, src_rows, src_rows, dst_rows, dst_rows, xt, w_gu, b_gu, w_down, b_down)


def _route_tables(route_i, counts, n_tok):
    n_assign = n_tok * TOP_K
    nb = n_assign // MOE_TM + N_EXPERTS
    counts = counts.reshape(N_EXPERTS).astype(I32)
    padded = ((counts + MOE_TM - 1) // MOE_TM) * MOE_TM
    pad_end = jnp.cumsum(padded)
    pad_start = pad_end - padded
    eid = route_i[:, :TOP_K]
    rank = route_i[:, TOP_K:2 * TOP_K]
    dest = (pad_start[eid] + rank).reshape(n_assign)
    pad_slot = n_assign + jnp.arange(nb * MOE_TM, dtype=I32) % (2 * MOE_TM)
    row_assign = pad_slot.at[dest].set(jnp.arange(n_assign, dtype=I32), unique_indices=True)
    tok = row_assign // TOP_K
    src_rows = jnp.minimum(tok, n_tok - 1) * LANE_TILES
    dst_rows = jnp.where(row_assign < n_assign, (row_assign % TOP_K) * n_tok + tok, row_assign) * LANE_TILES
    block_start = jnp.arange(nb, dtype=I32) * MOE_TM
    block_exp = jnp.minimum(jnp.sum((pad_end[None, :] <= block_start[:, None]).astype(I32), axis=1),
                            N_EXPERTS - 1)
    n_used = (pad_end[-1:] // MOE_TM).astype(I32)
    dummy_src = jnp.zeros((MOE_TM,), I32)
    dummy_dst = (n_assign + MOE_TM + jnp.arange(MOE_TM, dtype=I32)) * LANE_TILES
    src_ext = jnp.concatenate([src_rows, dummy_src]).reshape(nb + 1, 1, MOE_TM)
    dst_ext = jnp.concatenate([dummy_dst, dst_rows]).reshape(nb + 1, 1, MOE_TM)
    return block_exp, n_used, src_ext, dst_ext


def _combine_body(y0_ref, y1_ref, y2_ref, y3_ref, rf_ref, xt_ref, g_ref, b_ref, o_ref):
    tm = o_ref.shape[0]
    rf = rf_ref[...]
    parts = []
    for c in range(LANE_TILES):
        acc = DEEPNORM_ALPHA * xt_ref[pl.ds(c, tm, stride=LANE_TILES), :]
        for k, yk in enumerate((y0_ref, y1_ref, y2_ref, y3_ref)):
            acc = acc + rf[:, k:k + 1] * yk[pl.ds(c, tm, stride=LANE_TILES), :]
        parts.append(acc)
    z = jnp.concatenate(parts, axis=-1)
    o_ref[...] = _layer_norm(z, g_ref[...], b_ref[...])


def _combine(y_tk, route_f, xt, gain, bias, n_tok, name):
    tm = ROW_TM
    nblk = n_tok // tm
    yspec = lambda k: pl.BlockSpec((tm * LANE_TILES, LANES), lambda i: (k * nblk + i, 0))
    return pl.pallas_call(
        _combine_body,
        grid=(nblk,),
        in_specs=[yspec(0), yspec(1), yspec(2), yspec(3),
                  pl.BlockSpec((tm, LANES), lambda i: (i, 0)),
                  pl.BlockSpec((tm * LANE_TILES, LANES), lambda i: (i, 0)),
                  pl.BlockSpec((1, D_MODEL), lambda i: (0, 0)),
                  pl.BlockSpec((1, D_MODEL), lambda i: (0, 0))],
        out_specs=pl.BlockSpec((tm, D_MODEL), lambda i: (i, 0)),
        out_shape=jax.ShapeDtypeStruct((n_tok, D_MODEL), F32),
        compiler_params=_params(("parallel",)),
        name=name,
    )(y_tk, y_tk, y_tk, y_tk, route_f, xt, gain, bias)


def _gelu_tanh(x):
    return 0.5 * x * (1.0 + jnp.tanh(math.sqrt(2.0 / math.pi) * (x + 0.044715 * (x * x * x))))


def _lru_gate_body(rec_ref, prev_ref, cw_ref, cb_ref, wr_ref, br_ref, wi_ref, bi_ref, lam_ref,
                   a_ref, b_ref, ext_ref):
    s = pl.program_id(1)
    ts = rec_ref.shape[0]
    cur = rec_ref[...]
    ext_ref[0:SUBLANES, :] = jnp.where(s == 0, 0.0, prev_ref[...])
    ext_ref[SUBLANES:, :] = cur
    conv = cb_ref[...] + cw_ref[CONV_WIDTH - 1:CONV_WIDTH, :] * cur
    for j in range(CONV_WIDTH - 1):
        conv = conv + cw_ref[j:j + 1, :] * ext_ref[pl.ds(SUBLANES - (CONV_WIDTH - 1) + j, ts), :]
    xb = conv.astype(BF16)
    r_parts, i_parts = [], []
    for h in range(LRU_HEADS):
        xh = xb[:, h * LRU_BLOCK:(h + 1) * LRU_BLOCK]
        r_parts.append(jnp.dot(xh, wr_ref[h].astype(BF16), preferred_element_type=F32))
        i_parts.append(jnp.dot(xh, wi_ref[h].astype(BF16), preferred_element_type=F32))
    r = jax.nn.sigmoid(jnp.concatenate(r_parts, axis=-1) + br_ref[...])
    ig = jax.nn.sigmoid(jnp.concatenate(i_parts, axis=-1) + bi_ref[...])
    neg_lam = -lam_ref[...]
    softplus = jnp.maximum(neg_lam, 0.0) + jnp.log1p(jnp.exp(-jnp.abs(neg_lam)))
    log_a = -LRU_C * r * softplus
    a = jnp.exp(log_a)
    a_ref[...] = a
    one_minus_a2 = -jnp.tanh(log_a) * (a * a + 1.0)
    b_ref[...] = jnp.sqrt(one_minus_a2) * (ig * conv)


def _lru_scan_body(a_ref, b_ref, g_ref, y_ref):
    seq, ct = a_ref.shape
    row = lax.broadcasted_iota(I32, (SUBLANES, ct), 0)

    def step(gi, carry):
        off = pl.multiple_of(gi * SUBLANES, SUBLANES)
        a = a_ref[pl.ds(off, SUBLANES), :]
        b = b_ref[pl.ds(off, SUBLANES), :]
        for d in (1, 2, 4):
            keep = row >= d
            b = jnp.where(keep, a * pltpu.roll(b, d, 0) + b, b)
            a = jnp.where(keep, a * pltpu.roll(a, d, 0), a)
        h = a * carry + b
        y_ref[pl.ds(off, SUBLANES), :] = _gelu_tanh(g_ref[pl.ds(off, SUBLANES), :]) * h
        return jnp.broadcast_to(h[SUBLANES - 1:SUBLANES, :], (SUBLANES, ct))

    lax.fori_loop(0, seq // SUBLANES, step, jnp.zeros((SUBLANES, ct), F32))


def _rglru_mixer(x2, bsz, seq, idx, a_w_in, conv_w, conv_b, w_rgate, b_rgate, w_igate, b_igate, lam, layer):
    t = bsz * seq
    w = D_MODEL
    u = _matmul(x2, a_w_in, idx, 0, 2 * w, F32, f"lru_in_l{layer}")
    ts = 256
    nst = seq // ts
    vec = lambda: pl.BlockSpec((1, w), lambda b, s: (0, 0))
    a, bb = pl.pallas_call(
        _lru_gate_body,
        grid=(bsz, nst),
        in_specs=[pl.BlockSpec((ts, w), lambda b, s: (b * nst + s, 1)),
                  pl.BlockSpec((SUBLANES, w),
                               lambda b, s: (jnp.maximum((b * nst + s) * (ts // SUBLANES) - 1, 0), 1)),
                  pl.BlockSpec((CONV_WIDTH, w), lambda b, s: (0, 0)),
                  vec(),
                  pl.BlockSpec((LRU_HEADS, LRU_BLOCK, LRU_BLOCK), lambda b, s: (0, 0, 0)),
                  vec(),
                  pl.BlockSpec((LRU_HEADS, LRU_BLOCK, LRU_BLOCK), lambda b, s: (0, 0, 0)),
                  vec(), vec()],
        out_specs=[pl.BlockSpec((ts, w), lambda b, s: (b * nst + s, 0)),
                   pl.BlockSpec((ts, w), lambda b, s: (b * nst + s, 0))],
        out_shape=[jax.ShapeDtypeStruct((t, w), F32), jax.ShapeDtypeStruct((t, w), F32)],
        scratch_shapes=[pltpu.VMEM((ts + SUBLANES, w), F32)],
        compiler_params=_params(("parallel", "parallel")),
        name=f"lru_gates_l{layer}",
    )(u, u, conv_w[idx], conv_b[idx][None], w_rgate[idx], b_rgate[idx][None],
      w_igate[idx], b_igate[idx][None], lam[idx][None])
    ct = 256
    return pl.pallas_call(
        _lru_scan_body,
        grid=(bsz, w // ct),
        in_specs=[pl.BlockSpec((seq, ct), lambda b, j: (b, j)),
                  pl.BlockSpec((seq, ct), lambda b, j: (b, j)),
                  pl.BlockSpec((seq, ct), lambda b, j: (b, j))],
        out_specs=pl.BlockSpec((seq, ct), lambda b, j: (b, j)),
        out_shape=jax.ShapeDtypeStruct((t, w), F32),
        compiler_params=_params(("parallel", "parallel")),
        name=f"lru_scan_l{layer}",
    )(a, bb, u)


def _ret_body(lg_ref, q_ref, k_ref, v_ref, g_ref, y_ref, state_ref):
    c = pl.program_id(1)
    cs = RET_CHUNK
    heads = range(RET_HEADS)

    @pl.when(c == 0)
    def _():
        state_ref[...] = jnp.zeros_like(state_ref)

    qi = lax.broadcasted_iota(I32, (cs, cs), 0)
    ki = lax.broadcasted_iota(I32, (cs, cs), 1)
    rel = (qi - ki).astype(F32)
    pos_col = lax.broadcasted_iota(I32, (cs, 1), 0).astype(F32)
    lgs = [lg_ref[h] for h in heads]
    intra_decay = [jnp.where(rel >= 0, jnp.exp(lg * jnp.maximum(rel, 0.0)), 0.0) for lg in lgs]
    query_decay = [jnp.exp(lg * (pos_col + 1.0)) for lg in lgs]
    key_decay = [jnp.exp(lg * (cs - 1.0 - pos_col)) for lg in lgs]
    chunk_decay = [jnp.exp(jnp.full((1, 1), lg * cs, F32)) for lg in lgs]

    scale = RET_QK ** -0.5
    dn_nt = (((1,), (1,)), ((), ()))
    dn_tn = (((0,), (0,)), ((), ()))
    qs = [q_ref[:, h * RET_QK:(h + 1) * RET_QK] for h in heads]
    ks = [k_ref[:, h * RET_QK:(h + 1) * RET_QK] for h in heads]
    vs = [v_ref[:, h * RET_V:(h + 1) * RET_V] for h in heads]
    intra = [lax.dot_general(q, k, dn_nt, preferred_element_type=F32) * scale * d
             for q, k, d in zip(qs, ks, intra_decay)]
    states = [state_ref[h] for h in heads]
    cross = [jnp.dot(q, s.astype(BF16), preferred_element_type=F32) * d
             for q, s, d in zip(qs, states, query_decay)]
    outs = [jnp.dot(a.astype(BF16), v, preferred_element_type=F32) + x for a, v, x in zip(intra, vs, cross)]
    kds = [(k.astype(F32) * (scale * d)).astype(BF16) for k, d in zip(ks, key_decay)]
    upd = [lax.dot_general(kd, v, dn_tn, preferred_element_type=F32) for kd, v in zip(kds, vs)]
    for h in heads:
        state_ref[h] = chunk_decay[h] * states[h] + upd[h]
    for h in heads:
        out = outs[h]
        o = out * lax.rsqrt(jnp.mean(out * out, axis=-1, keepdims=True) + RET_EPS)
        g = g_ref[:, h * RET_V:(h + 1) * RET_V]
        y_ref[:, h * RET_V:(h + 1) * RET_V] = (g * jax.nn.sigmoid(g) * o).astype(y_ref.dtype)


def _retention_mixer(x2, bsz, seq, idx, b_w_in, layer):
    t = bsz * seq
    qk_w = RET_HEADS * RET_QK
    v_w = RET_HEADS * RET_V
    qkv = _matmul(x2, b_w_in, idx, 0, 2 * qk_w + v_w, BF16, f"ret_qkv_l{layer}")
    gate = _matmul(x2, b_w_in, idx, 2 * qk_w + v_w, v_w, F32, f"ret_gate_l{layer}")
    nc = seq // RET_CHUNK
    log_gamma = jnp.log1p(-jnp.exp2(-5.0 - jnp.arange(RET_HEADS, dtype=F32)))
    grid_spec = pltpu.PrefetchScalarGridSpec(
        num_scalar_prefetch=1,
        grid=(bsz, nc),
        in_specs=[pl.BlockSpec((RET_CHUNK, qk_w), lambda b, c, lg: (b * nc + c, 0)),
                  pl.BlockSpec((RET_CHUNK, qk_w), lambda b, c, lg: (b * nc + c, 1)),
                  pl.BlockSpec((RET_CHUNK, v_w), lambda b, c, lg: (b * nc + c, 1)),
                  pl.BlockSpec((RET_CHUNK, v_w), lambda b, c, lg: (b * nc + c, 0))],
        out_specs=pl.BlockSpec((RET_CHUNK, v_w), lambda b, c, lg: (b * nc + c, 0)),
        scratch_shapes=[pltpu.VMEM((RET_HEADS, RET_QK, RET_V), F32)],
    )
    return pl.pallas_call(
        _ret_body,
        grid_spec=grid_spec,
        out_shape=jax.ShapeDtypeStruct((t, v_w), BF16),
        compiler_params=_params(("parallel", "arbitrary")),
        name=f"retention_l{layer}",
    )(log_gamma, qkv, qkv, qkv, gate)


def _attn_body(*refs, dilation, has_prev):
    if has_prev:
        q_ref, kc_ref, kp_ref, vc_ref, vp_ref, o_ref, st_ref = refs
    else:
        q_ref, kc_ref, vc_ref, o_ref, st_ref = refs
    n = pl.program_id(2)
    nw = q_ref.shape[0]
    dh = ATTN_HEAD_DIM
    scale = dh ** -0.5
    qi = lax.broadcasted_iota(I32, (nw, nw), 0)
    ki = lax.broadcasted_iota(I32, (nw, nw), 1)
    steps_c = qi - ki
    valid_c = steps_c >= 0
    dist_c = (steps_c * dilation).astype(F32)
    q = q_ref[...]
    kc = kc_ref[...]
    vc = vc_ref[...]
    lane = lax.broadcasted_iota(I32, (nw, LANES), 1)
    dn = (((1,), (1,)), ((), ()))
    heads = range(ATTN_HEADS)
    sls = [slice(h * dh, (h + 1) * dh) for h in heads]
    slopes = [2.0 ** (-8.0 * (h + 1) / ATTN_HEADS) for h in heads]
    sc_c = [lax.dot_general(q[:, sl], kc[:, sl], dn, preferred_element_type=F32) for sl in sls]
    sc_c = [jnp.where(valid_c, s * scale - sp * dist_c, -jnp.inf) for s, sp in zip(sc_c, slopes)]
    ms = [jnp.max(a, axis=-1, keepdims=True) for a in sc_c]
    if has_prev:
        steps_p = steps_c + nw
        valid_p = steps_p <= jnp.where(n > 0, nw, -1)
        dist_p = (steps_p * dilation).astype(F32)
        kp = kp_ref[...]
        vp = vp_ref[...]
        sc_p = [lax.dot_general(q[:, sl], kp[:, sl], dn, preferred_element_type=F32) for sl in sls]
        sc_p = [jnp.where(valid_p, s * scale - sp * dist_p, -jnp.inf) for s, sp in zip(sc_p, slopes)]
        ms = [jnp.maximum(m, jnp.max(b, axis=-1, keepdims=True)) for m, b in zip(ms, sc_p)]
    pc = [jnp.exp(a - m) for a, m in zip(sc_c, ms)]
    sums = [jnp.sum(a, axis=-1, keepdims=True) for a in pc]
    outs = [jnp.dot(a.astype(BF16), vc[:, sl], preferred_element_type=F32) for a, sl in zip(pc, sls)]
    if has_prev:
        pp = [jnp.exp(b - m) for b, m in zip(sc_p, ms)]
        sums = [s + jnp.sum(b, axis=-1, keepdims=True) for s, b in zip(sums, pp)]
        outs = [o + jnp.dot(b.astype(BF16), vp[:, sl], preferred_element_type=F32)
                for o, b, sl in zip(outs, pp, sls)]
    stats = jnp.zeros((nw, LANES), F32)
    for h, sl, o, m, sm in zip(heads, sls, outs, ms, sums):
        o_ref[:, sl] = o
        stats = jnp.where(lane == h, m, stats)
        stats = jnp.where(lane == ATTN_HEADS + h, sm, stats)
    st_ref[...] = stats


def _merge_body(o0_ref, o1_ref, o2_ref, s0_ref, s1_ref, s2_ref, y_ref):
    tm = y_ref.shape[0]
    stats = [s0_ref[...], s1_ref[...], s2_ref[...]]
    lane = lax.broadcasted_iota(I32, (tm, LANES), 1)
    maxes = [s for s in stats]
    sums = [pltpu.roll(s, LANES - ATTN_HEADS, 1) for s in stats]
    m_all = jnp.maximum(jnp.maximum(maxes[0], maxes[1]), maxes[2])
    wgt = [jnp.exp(m - m_all) for m in maxes]
    den = wgt[0] * sums[0] + wgt[1] * sums[1] + wgt[2] * sums[2]
    er = lax.broadcasted_iota(I32, (LANES, D_MODEL), 0)
    ec = lax.broadcasted_iota(I32, (LANES, D_MODEL), 1)
    expand = (ec // ATTN_HEAD_DIM == er).astype(F32)
    acc = jnp.zeros((tm, D_MODEL), F32)
    for w, o_ref in zip(wgt, (o0_ref, o1_ref, o2_ref)):
        coef = jnp.where(lane < ATTN_HEADS, w / den, 0.0)
        acc = acc + jnp.dot(coef, expand, preferred_element_type=F32,
                            precision=lax.Precision.HIGHEST) * o_ref[...]
    y_ref[...] = acc


def _attention_mixer(x2, bsz, seq, idx, c_w_in, layer):
    t = bsz * seq
    w = D_MODEL
    n_groups = len(ATTN_PATTERNS)
    outs, stats = [], []
    for g, (window, dilation) in enumerate(ATTN_PATTERNS):
        nw = window // dilation
        sub = seq // dilation
        assert nw == 128 and sub % nw == 0
        nb = sub // nw

        def to_sub(a):
            return a.reshape(bsz, sub, dilation, -1).transpose(0, 2, 1, 3).reshape(t, -1)

        def from_sub(a):
            return a.reshape(bsz, dilation, sub, -1).transpose(0, 2, 1, 3).reshape(t, -1)

        xs = x2 if dilation == 1 else to_sub(x2)
        qkv = _matmul(xs, c_w_in, idx, g * w, 3 * w, BF16, f"attn_qkv_g{g}_l{layer}",
                      col_block_stride=n_groups)
        cur = lambda slab: pl.BlockSpec((nw, w), lambda b, r, n, slab=slab: ((b * dilation + r) * nb + n, slab))
        prev = lambda slab: pl.BlockSpec(
            (nw, w), lambda b, r, n, slab=slab: ((b * dilation + r) * nb + jnp.maximum(n - 1, 0), slab))
        has_prev = nb > 1
        in_specs = [cur(0), cur(1), prev(1), cur(2), prev(2)] if has_prev else [cur(0), cur(1), cur(2)]
        o, st = pl.pallas_call(
            functools.partial(_attn_body, dilation=dilation, has_prev=has_prev),
            grid=(bsz, dilation, nb),
            in_specs=in_specs,
            out_specs=[pl.BlockSpec((nw, w), lambda b, r, n: ((b * dilation + r) * nb + n, 0)),
                       pl.BlockSpec((nw, LANES), lambda b, r, n: ((b * dilation + r) * nb + n, 0))],
            out_shape=[jax.ShapeDtypeStruct((t, w), F32), jax.ShapeDtypeStruct((t, LANES), F32)],
            compiler_params=_params(("parallel", "parallel", "parallel")),
            name=f"attn_g{g}_l{layer}",
        )(*([qkv] * len(in_specs)))
        outs.append(o if dilation == 1 else from_sub(o))
        stats.append(st if dilation == 1 else from_sub(st))
    tm = ROW_TM
    return pl.pallas_call(
        _merge_body,
        grid=(t // tm,),
        in_specs=[pl.BlockSpec((tm, w), lambda i: (i, 0))] * 3 + [pl.BlockSpec((tm, LANES), lambda i: (i, 0))] * 3,
        out_specs=pl.BlockSpec((tm, w), lambda i: (i, 0)),
        out_shape=jax.ShapeDtypeStruct((t, w), F32),
        compiler_params=_params(("parallel",)),
        name=f"attn_merge_l{layer}",
    )(*outs, *stats)


def kernel(x, a_w_in, a_conv_w, a_conv_b, a_w_rgate, a_b_rgate, a_w_igate, a_b_igate, a_lambda, a_w_out,
           b_w_in, b_w_out, c_w_in, c_w_out, ln_gain, ln_bias, moe_w_router, moe_b_router, moe_w_gu,
           moe_b_gu, moe_w_down, moe_b_down):
    bsz, seq, d = x.shape
    t = bsz * seq
    x2 = x.reshape(t, d)
    b_gu4 = moe_b_gu[:, :, None, :]
    b_down4 = moe_b_down[:, :, None, :]
    for layer in range(DEPTH):
        kind, idx = layer % N_MIXERS, layer // N_MIXERS
        if kind == 0:
            y = _rglru_mixer(x2, bsz, seq, idx, a_w_in, a_conv_w, a_conv_b, a_w_rgate, a_b_rgate,
                             a_w_igate, a_b_igate, a_lambda, layer)
            w_out = a_w_out
        elif kind == 1:
            y = _retention_mixer(x2, bsz, seq, idx, b_w_in, layer)
            w_out = b_w_out
        else:
            y = _attention_mixer(x2, bsz, seq, idx, c_w_in, layer)
            w_out = c_w_out
        xt, route_f, route_i, counts = _post_mixer(
            y, w_out, idx, x2, ln_gain[layer, 0][None], ln_bias[layer, 0][None],
            moe_w_router[layer], moe_b_router[layer][None], f"post_mixer_l{layer}")
        block_exp, n_used, src_rows, dst_rows = _route_tables(route_i, counts, t)
        y_tk = _moe_experts(xt, block_exp, n_used, src_rows, dst_rows,
                            moe_w_gu, b_gu4, moe_w_down, b_down4, layer, t)
        x2 = _combine(y_tk, route_f, xt, ln_gain[layer, 1][None], ln_bias[layer, 1][None], t,
                      f"combine_l{layer}")
    return x2.reshape(bsz, seq, d)
```

```python
import functools
import math

import jax
import jax.numpy as jnp
from jax import lax
from jax.experimental import pallas as pl
from jax.experimental.pallas import tpu as pltpu

F32 = jnp.float32
BF16 = jnp.bfloat16
I32 = jnp.int32

D_MODEL = 1024
DEPTH = 4
N_MIXERS = 3
DEEPNORM_ALPHA = (2.0 * DEPTH) ** 0.25
LN_EPS = 1e-5
LRU_HEADS = 8
LRU_BLOCK = D_MODEL // LRU_HEADS
CONV_WIDTH = 4
LRU_C = 8.0
RET_HEADS = 4
RET_QK = D_MODEL // RET_HEADS
RET_V = 2 * RET_QK
RET_CHUNK = 128
RET_EPS = 1e-6
ATTN_PATTERNS = ((128, 1), (512, 4), (2048, 16))
ATTN_HEADS = 16
ATTN_HEAD_DIM = D_MODEL // ATTN_HEADS
N_EXPERTS = 32
TOP_K = 4
D_FF = D_MODEL
SWIGLU_LIMIT = 7.0
SWIGLU_ALPHA = 1.702

SUBLANES = 8
LANES = 128
LANE_TILES = D_MODEL // LANES
VMEM_LIMIT = 56 * 1024 * 1024

MOE_TM = 256
MOE_FF_CHUNK = 256
ROW_TM = 256


def _params(sem, vmem=VMEM_LIMIT):
    return pltpu.CompilerParams(dimension_semantics=sem, vmem_limit_bytes=vmem)


def _mm_body(x_ref, w_ref, o_ref, wbf_ref):
    @pl.when(pl.program_id(1) == 0)
    def _():
        wbf_ref[...] = w_ref[...].astype(BF16)

    o_ref[...] = jnp.dot(x_ref[...].astype(BF16), wbf_ref[...],
                         preferred_element_type=F32).astype(o_ref.dtype)


def _matmul(x, w_stack, idx, col0, ncols, out_dtype, name, tm=512, tn=1024, col_block_stride=1):
    t, k = x.shape
    assert col0 % tn == 0 and ncols % tn == 0 and t % tm == 0
    return pl.pallas_call(
        _mm_body,
        grid=(ncols // tn, t // tm),
        in_specs=[pl.BlockSpec((tm, k), lambda n, m: (m, 0)),
                  pl.BlockSpec((None, k, tn), lambda n, m: (idx, 0, col0 // tn + n * col_block_stride))],
        out_specs=pl.BlockSpec((tm, tn), lambda n, m: (m, n)),
        out_shape=jax.ShapeDtypeStruct((t, ncols), out_dtype),
        scratch_shapes=[pltpu.VMEM((k, tn), BF16)],
        compiler_params=_params(("arbitrary", "arbitrary")),
        name=name,
    )(x, w_stack)


def _layer_norm(z, gain, bias):
    mu = jnp.mean(z, axis=-1, keepdims=True)
    zc = z - mu
    var = jnp.mean(zc * zc, axis=-1, keepdims=True)
    return zc * lax.rsqrt(var + LN_EPS) * gain + bias


def _store_token_tiles(ref, val):
    tm = val.shape[0]
    for c in range(LANE_TILES):
        ref[pl.ds(c, tm, stride=LANE_TILES), :] = val[:, c * LANES:(c + 1) * LANES]


def _load_token_tiles(ref, tm):
    return jnp.concatenate([ref[pl.ds(c, tm, stride=LANE_TILES), :] for c in range(LANE_TILES)], axis=-1)


def _post_body(y_ref, w_ref, x_ref, g_ref, b_ref, wr_ref, br_ref,
               xt_ref, rf_ref, ri_ref, cnt_ref, wbf_ref, run_ref):
    i = pl.program_id(0)
    tm = x_ref.shape[0]

    @pl.when(i == 0)
    def _():
        wbf_ref[...] = w_ref[...].astype(BF16)
        run_ref[...] = jnp.zeros_like(run_ref)

    z = DEEPNORM_ALPHA * x_ref[...] + jnp.dot(y_ref[...].astype(BF16), wbf_ref[...],
                                              preferred_element_type=F32)
    xn = _layer_norm(z, g_ref[...], b_ref[...])
    _store_token_tiles(xt_ref, xn)

    logits = jnp.dot(xn, wr_ref[...], preferred_element_type=F32,
                     precision=lax.Precision.HIGHEST) + br_ref[...]
    lane_e = lax.broadcasted_iota(I32, (tm, N_EXPERTS), 1).astype(F32)
    work = logits
    top_l, top_e, onehots = [], [], []
    for _ in range(TOP_K):
        m = jnp.max(work, axis=-1, keepdims=True)
        idx = jnp.min(jnp.where(work == m, lane_e, float(N_EXPERTS)), axis=-1, keepdims=True)
        oh = lane_e == idx
        top_l.append(m)
        top_e.append(idx)
        onehots.append(oh)
        work = jnp.where(oh, -jnp.inf, work)
    exps = [jnp.exp(l - top_l[0]) for l in top_l]
    denom = exps[0] + exps[1] + exps[2] + exps[3]
    gates = [e / denom for e in exps]

    sel = jnp.zeros((tm, N_EXPERTS), F32)
    for oh in onehots:
        sel = sel + oh.astype(F32)
    ri = lax.broadcasted_iota(I32, (tm, tm), 0)
    ci = lax.broadcasted_iota(I32, (tm, tm), 1)
    tri = (ci < ri).astype(BF16)
    before = jnp.dot(tri, sel.astype(BF16), preferred_element_type=F32) + run_ref[...]
    ranks = [jnp.sum(jnp.where(oh, before, 0.0), axis=-1, keepdims=True) for oh in onehots]
    run_ref[...] = run_ref[...] + jnp.sum(sel, axis=0, keepdims=True)
    cnt_ref[...] = run_ref[...]

    lane = lax.broadcasted_iota(I32, (tm, LANES), 1)
    rf = jnp.zeros((tm, LANES), F32)
    rint = jnp.zeros((tm, LANES), I32)
    for k in range(TOP_K):
        rf = jnp.where(lane == k, gates[k], rf)
        rint = jnp.where(lane == k, top_e[k].astype(I32), rint)
        rint = jnp.where(lane == TOP_K + k, ranks[k].astype(I32), rint)
    rf_ref[...] = rf
    ri_ref[...] = rint


def _post_mixer(y, w_out_stack, idx, x, gain, bias, w_router, b_router, name):
    t, kd = y.shape
    tm = ROW_TM
    return pl.pallas_call(
        _post_body,
        grid=(t // tm,),
        in_specs=[pl.BlockSpec((tm, kd), lambda i: (i, 0)),
                  pl.BlockSpec((None, kd, D_MODEL), lambda i: (idx, 0, 0)),
                  pl.BlockSpec((tm, D_MODEL), lambda i: (i, 0)),
                  pl.BlockSpec((1, D_MODEL), lambda i: (0, 0)),
                  pl.BlockSpec((1, D_MODEL), lambda i: (0, 0)),
                  pl.BlockSpec((D_MODEL, N_EXPERTS), lambda i: (0, 0)),
                  pl.BlockSpec((1, N_EXPERTS), lambda i: (0, 0))],
        out_specs=[pl.BlockSpec((tm * LANE_TILES, LANES), lambda i: (i, 0)),
                   pl.BlockSpec((tm, LANES), lambda i: (i, 0)),
                   pl.BlockSpec((tm, LANES), lambda i: (i, 0)),
                   pl.BlockSpec((1, N_EXPERTS), lambda i: (0, 0))],
        out_shape=[jax.ShapeDtypeStruct((t * LANE_TILES, LANES), F32),
                   jax.ShapeDtypeStruct((t, LANES), F32),
                   jax.ShapeDtypeStruct((t, LANES), I32),
                   jax.ShapeDtypeStruct((1, N_EXPERTS), F32)],
        scratch_shapes=[pltpu.VMEM((kd, D_MODEL), BF16), pltpu.VMEM((1, N_EXPERTS), F32)],
        compiler_params=_params(("arbitrary",)),
        name=name,
    )(y, w_out_stack, x, gain, bias, w_router, b_router)


def _gather_copy(x_hbm, xbuf, sem, src_row, r):
    return pltpu.make_async_copy(
        x_hbm.at[pl.ds(pl.multiple_of(src_row, SUBLANES), SUBLANES), :],
        xbuf.at[pl.ds(pl.multiple_of(r * SUBLANES, SUBLANES), SUBLANES), :],
        sem)


def _scatter_copy(ybuf, y_hbm, sem, dst_row, r):
    return pltpu.make_async_copy(
        ybuf.at[pl.ds(pl.multiple_of(r * SUBLANES, SUBLANES), SUBLANES), :],
        y_hbm.at[pl.ds(pl.multiple_of(dst_row, SUBLANES), SUBLANES), :],
        sem)


def _moe_body(be_ref, nu_ref, src_ref, srcn_ref, dstp_ref, dst_ref, x_hbm, wgu_ref, bgu_ref, wd_ref, bd_ref,
              y_hbm, xbuf, ybuf, xbf, hid, rawg, rawu, wg_bf, wu_bf, wd_bf, bg_s, bu_s, bd_s, gsem, ssem):
    i = pl.program_id(0)
    nb = pl.num_programs(0)
    rows = MOE_TM * LANE_TILES
    slot = i % 2
    other = 1 - slot
    n_chunks = D_FF // MOE_FF_CHUNK
    per_chunk = MOE_TM // n_chunks
    loop_group = 32

    def gather_rows(idx_ref, s, r0, n):
        for q in range(n):
            _gather_copy(x_hbm, xbuf.at[s], gsem.at[s], idx_ref[0, r0 + q], r0 + q).start(priority=q % 2)

    def scatter_rows(idx_ref, s, r0, n):
        for q in range(n):
            _scatter_copy(ybuf.at[s], y_hbm, ssem.at[s], idx_ref[0, r0 + q], r0 + q).start(priority=q % 2)

    def gather_all(idx_ref, s):
        lax.fori_loop(0, MOE_TM // loop_group,
                      lambda g, c: (gather_rows(idx_ref, s, g * loop_group, loop_group), c)[1], 0)

    def scatter_all(idx_ref, s):
        lax.fori_loop(0, MOE_TM // loop_group,
                      lambda g, c: (scatter_rows(idx_ref, s, g * loop_group, loop_group), c)[1], 0)

    def wait_gather(s):
        pltpu.make_async_copy(x_hbm.at[pl.ds(0, rows), :], xbuf.at[s], gsem.at[s]).wait()

    def wait_scatter(s):
        pltpu.make_async_copy(ybuf.at[s], y_hbm.at[pl.ds(0, rows), :], ssem.at[s]).wait()

    @pl.when(i == 0)
    def _():
        ybuf[...] = jnp.zeros_like(ybuf)
        tail = y_hbm.at[pl.ds(y_hbm.shape[0] - 2 * rows, 2 * rows), :]
        for s in range(2):
            cp = pltpu.make_async_copy(ybuf.at[s], tail.at[pl.ds(s * rows, rows), :], ssem.at[s])
            cp.start()
            cp.wait()
        gather_all(src_ref, 0)

    wait_gather(slot)
    used = i < nu_ref[0]

    @pl.when(used)
    def _():
        changed = jnp.logical_or(i == 0, be_ref[i] != be_ref[jnp.maximum(i - 1, 0)])
        out_chunk = D_MODEL // n_chunks

        @pl.when(changed)
        def _():
            for j in range(n_chunks):
                lo, hi = j * MOE_FF_CHUNK, (j + 1) * MOE_FF_CHUNK
                wg_bf[j] = wgu_ref[:, lo:hi].astype(BF16)
                wu_bf[j] = wgu_ref[:, D_FF + lo:D_FF + hi].astype(BF16)
                bg_s[j] = bgu_ref[:, lo:hi]
                bu_s[j] = bgu_ref[:, D_FF + lo:D_FF + hi]
                wd_bf[j] = wd_ref[:, j * out_chunk:(j + 1) * out_chunk].astype(BF16)
                bd_s[j] = bd_ref[:, j * out_chunk:(j + 1) * out_chunk]

        xbf[...] = _load_token_tiles(xbuf.at[slot], MOE_TM).astype(BF16)

        quarter = per_chunk // 4

        def activate(gate, up):
            gate = jnp.minimum(gate, SWIGLU_LIMIT)
            up = jnp.clip(up, -SWIGLU_LIMIT, SWIGLU_LIMIT)
            return ((up + 1.0) * gate * jax.nn.sigmoid(SWIGLU_ALPHA * gate)).astype(BF16)

        gather_rows(srcn_ref, other, 0, 2 * quarter)
        rawg[0] = jnp.dot(xbf[...], wg_bf[0], preferred_element_type=F32) + bg_s[0]
        gather_rows(srcn_ref, other, 2 * quarter, 2 * quarter)
        rawu[0] = jnp.dot(xbf[...], wu_bf[0], preferred_element_type=F32) + bu_s[0]

        def up_chunk(j, carry):
            r0 = j * per_chunk
            cur = j % 2
            x = xbf[...]
            gather_rows(srcn_ref, other, r0, quarter)
            rawg[cur] = jnp.dot(x, wg_bf[j], preferred_element_type=F32) + bg_s[j]
            gather_rows(srcn_ref, other, r0 + quarter, quarter)
            rawu[cur] = jnp.dot(x, wu_bf[j], preferred_element_type=F32) + bu_s[j]
            gather_rows(srcn_ref, other, r0 + 2 * quarter, quarter)
            hid[j - 1] = activate(rawg[1 - cur], rawu[1 - cur])
            gather_rows(srcn_ref, other, r0 + 3 * quarter, quarter)
            return carry

        lax.fori_loop(1, n_chunks, up_chunk, 0)
        hid[n_chunks - 1] = activate(rawg[(n_chunks - 1) % 2], rawu[(n_chunks - 1) % 2])

        @pl.when(i >= 1)
        def _():
            wait_scatter(slot)

        ytile = ybuf.at[slot]
        tiles_per_chunk = out_chunk // LANES

        def down_chunk(n, carry):
            r0 = n * per_chunk
            scatter_rows(dstp_ref, other, r0, 2 * quarter)
            h = jnp.concatenate([hid[j] for j in range(n_chunks)], axis=-1)
            y = jnp.dot(h, wd_bf[n], preferred_element_type=F32) + bd_s[n]
            scatter_rows(dstp_ref, other, r0 + 2 * quarter, 2 * quarter)
            for c in range(tiles_per_chunk):
                ytile[pl.ds(n * tiles_per_chunk + c, MOE_TM, stride=LANE_TILES), :] = y[:, c * LANES:(c + 1) * LANES]
            return carry

        lax.fori_loop(0, n_chunks, down_chunk, 0)

    @pl.when(jnp.logical_not(used))
    def _():
        gather_all(srcn_ref, other)
        scatter_all(dstp_ref, other)

        @pl.when(i >= 1)
        def _():
            wait_scatter(slot)

    @pl.when(i == nb - 1)
    def _():
        scatter_all(dst_ref, slot)
        wait_scatter(other)
        wait_scatter(slot)
        wait_gather(other)


def _moe_experts(xt, block_exp, n_used, src_rows, dst_rows, w_gu, b_gu, w_down, b_down, layer, n_tok):
    nb = block_exp.shape[0]
    rows = MOE_TM * LANE_TILES
    n_chunks = D_FF // MOE_FF_CHUNK
    n_out_rows = (TOP_K * n_tok + 2 * MOE_TM) * LANE_TILES
    smem_blk = lambda f: pl.BlockSpec((None, 1, MOE_TM), f, memory_space=pltpu.SMEM)
    grid_spec = pltpu.PrefetchScalarGridSpec(
        num_scalar_prefetch=2,
        grid=(nb,),
        in_specs=[smem_blk(lambda i, be, nu: (i, 0, 0)),
                  smem_blk(lambda i, be, nu: (i + 1, 0, 0)),
                  smem_blk(lambda i, be, nu: (i, 0, 0)),
                  smem_blk(lambda i, be, nu: (i + 1, 0, 0)),
                  pl.BlockSpec(memory_space=pl.ANY),
                  pl.BlockSpec((None, None, D_MODEL, 2 * D_FF), lambda i, be, nu: (layer, be[i], 0, 0)),
                  pl.BlockSpec((None, None, 1, 2 * D_FF), lambda i, be, nu: (layer, be[i], 0, 0)),
                  pl.BlockSpec((None, None, D_FF, D_MODEL), lambda i, be, nu: (layer, be[i], 0, 0)),
                  pl.BlockSpec((None, None, 1, D_MODEL), lambda i, be, nu: (layer, be[i], 0, 0))],
        out_specs=pl.BlockSpec(memory_space=pl.ANY),
        scratch_shapes=[pltpu.VMEM((2, rows, LANES), F32),
                        pltpu.VMEM((2, rows, LANES), F32),
                        pltpu.VMEM((MOE_TM, D_MODEL), BF16),
                        pltpu.VMEM((n_chunks, MOE_TM, MOE_FF_CHUNK), BF16),
                        pltpu.VMEM((2, MOE_TM, MOE_FF_CHUNK), F32),
                        pltpu.VMEM((2, MOE_TM, MOE_FF_CHUNK), F32),
                        pltpu.VMEM((n_chunks, D_MODEL, MOE_FF_CHUNK), BF16),
                        pltpu.VMEM((n_chunks, D_MODEL, MOE_FF_CHUNK), BF16),
                        pltpu.VMEM((n_chunks, D_FF, D_MODEL // n_chunks), BF16),
                        pltpu.VMEM((n_chunks, 1, MOE_FF_CHUNK), F32),
                        pltpu.VMEM((n_chunks, 1, MOE_FF_CHUNK), F32),
                        pltpu.VMEM((n_chunks, 1, D_MODEL // n_chunks), F32),
                        pltpu.SemaphoreType.DMA((2,)),
                        pltpu.SemaphoreType.DMA((2,))],
    )
    return pl.pallas_call(
        _moe_body,
        grid_spec=grid_spec,
        out_shape=jax.ShapeDtypeStruct((n_out_rows, LANES), F32),
        compiler_params=_params(("arbitrary",)),
        name=f"moe_experts_l{layer}",
    )(block_exp, n_used, src_rows, src_rows, dst_rows, dst_rows, xt, w_gu, b_gu, w_down, b_down)


ROW_TABLE_CHUNK = 4096


def _row_table_body(lo_ref, hi_ref, dest_ref, tbl_ref, *, n_assign):
    i = pl.program_id(0)

    @pl.when(i == 0)
    def _():
        def fill_range(e, c):
            def fill(r, c2):
                tbl_ref[r] = n_assign + jnp.bitwise_and(r, 2 * MOE_TM - 1)
                return c2
            lax.fori_loop(lo_ref[e], hi_ref[e], fill, 0)
            return c
        lax.fori_loop(0, lo_ref.shape[0], fill_range, 0)

    def body(a, c):
        tbl_ref[dest_ref[0, a]] = i * ROW_TABLE_CHUNK + a
        return c
    lax.fori_loop(0, ROW_TABLE_CHUNK, body, 0, unroll=8)


def _row_table(dest, pad_lo, pad_hi, n_rows, name):
    n_assign = dest.shape[0]
    steps = n_assign // ROW_TABLE_CHUNK
    grid_spec = pltpu.PrefetchScalarGridSpec(
        num_scalar_prefetch=2,
        grid=(steps,),
        in_specs=[pl.BlockSpec((None, 1, ROW_TABLE_CHUNK), lambda i, lo, hi: (i, 0, 0),
                               memory_space=pltpu.SMEM)],
        out_specs=pl.BlockSpec(memory_space=pltpu.SMEM),
    )
    return pl.pallas_call(
        functools.partial(_row_table_body, n_assign=n_assign),
        grid_spec=grid_spec,
        out_shape=jax.ShapeDtypeStruct((n_rows,), I32),
        compiler_params=_params(("arbitrary",)),
        name=name,
    )(pad_lo, pad_hi, dest.reshape(steps, 1, ROW_TABLE_CHUNK))


def _route_tables(route_i, counts, n_tok, name):
    n_assign = n_tok * TOP_K
    nb = n_assign // MOE_TM + N_EXPERTS
    counts = counts.reshape(N_EXPERTS).astype(I32)
    padded = ((counts + MOE_TM - 1) // MOE_TM) * MOE_TM
    pad_end = jnp.cumsum(padded)
    pad_start = pad_end - padded
    eid = route_i[:, :TOP_K]
    rank = route_i[:, TOP_K:2 * TOP_K]
    dest = (pad_start[eid] + rank).reshape(n_assign)
    pad_lo = jnp.concatenate([pad_start + counts, pad_end[-1:]]).astype(I32)
    pad_hi = jnp.concatenate([pad_end, jnp.full((1,), nb * MOE_TM, I32)]).astype(I32)
    row_assign = _row_table(dest, pad_lo, pad_hi, nb * MOE_TM, name)
    tok = row_assign // TOP_K
    src_rows = jnp.minimum(tok, n_tok - 1) * LANE_TILES
    dst_rows = jnp.where(row_assign < n_assign, (row_assign % TOP_K) * n_tok + tok, row_assign) * LANE_TILES
    block_start = jnp.arange(nb, dtype=I32) * MOE_TM
    block_exp = jnp.minimum(jnp.sum((pad_end[None, :] <= block_start[:, None]).astype(I32), axis=1),
                            N_EXPERTS - 1)
    n_used = (pad_end[-1:] // MOE_TM).astype(I32)
    dummy_src = jnp.zeros((MOE_TM,), I32)
    dummy_dst = (n_assign + MOE_TM + jnp.arange(MOE_TM, dtype=I32)) * LANE_TILES
    src_ext = jnp.concatenate([src_rows, dummy_src]).reshape(nb + 1, 1, MOE_TM)
    dst_ext = jnp.concatenate([dummy_dst, dst_rows]).reshape(nb + 1, 1, MOE_TM)
    return block_exp, n_used, src_ext, dst_ext


def _combine_body(y0_ref, y1_ref, y2_ref, y3_ref, rf_ref, xt_ref, g_ref, b_ref, o_ref):
    tm = o_ref.shape[0]
    rf = rf_ref[...]
    parts = []
    for c in range(LANE_TILES):
        acc = DEEPNORM_ALPHA * xt_ref[pl.ds(c, tm, stride=LANE_TILES), :]
        for k, yk in enumerate((y0_ref, y1_ref, y2_ref, y3_ref)):
            acc = acc + rf[:, k:k + 1] * yk[pl.ds(c, tm, stride=LANE_TILES), :]
        parts.append(acc)
    z = jnp.concatenate(parts, axis=-1)
    o_ref[...] = _layer_norm(z, g_ref[...], b_ref[...])


def _combine(y_tk, route_f, xt, gain, bias, n_tok, name):
    tm = ROW_TM
    nblk = n_tok // tm
    yspec = lambda k: pl.BlockSpec((tm * LANE_TILES, LANES), lambda i: (k * nblk + i, 0))
    return pl.pallas_call(
        _combine_body,
        grid=(nblk,),
        in_specs=[yspec(0), yspec(1), yspec(2), yspec(3),
                  pl.BlockSpec((tm, LANES), lambda i: (i, 0)),
                  pl.BlockSpec((tm * LANE_TILES, LANES), lambda i: (i, 0)),
                  pl.BlockSpec((1, D_MODEL), lambda i: (0, 0)),
                  pl.BlockSpec((1, D_MODEL), lambda i: (0, 0))],
        out_specs=pl.BlockSpec((tm, D_MODEL), lambda i: (i, 0)),
        out_shape=jax.ShapeDtypeStruct((n_tok, D_MODEL), F32),
        compiler_params=_params(("parallel",)),
        name=name,
    )(y_tk, y_tk, y_tk, y_tk, route_f, xt, gain, bias)


def _gelu_tanh(x):
    return 0.5 * x * (1.0 + jnp.tanh(math.sqrt(2.0 / math.pi) * (x + 0.044715 * (x * x * x))))


def _lru_gate_body(rec_ref, prev_ref, cw_ref, cb_ref, wr_ref, br_ref, wi_ref, bi_ref, lam_ref,
                   a_ref, b_ref, ext_ref):
    s = pl.program_id(1)
    ts = rec_ref.shape[0]
    cur = rec_ref[...]
    ext_ref[0:SUBLANES, :] = jnp.where(s == 0, 0.0, prev_ref[...])
    ext_ref[SUBLANES:, :] = cur
    conv = cb_ref[...] + cw_ref[CONV_WIDTH - 1:CONV_WIDTH, :] * cur
    for j in range(CONV_WIDTH - 1):
        conv = conv + cw_ref[j:j + 1, :] * ext_ref[pl.ds(SUBLANES - (CONV_WIDTH - 1) + j, ts), :]
    xb = conv.astype(BF16)
    r_parts, i_parts = [], []
    for h in range(LRU_HEADS):
        xh = xb[:, h * LRU_BLOCK:(h + 1) * LRU_BLOCK]
        r_parts.append(jnp.dot(xh, wr_ref[h].astype(BF16), preferred_element_type=F32))
        i_parts.append(jnp.dot(xh, wi_ref[h].astype(BF16), preferred_element_type=F32))
    r = jax.nn.sigmoid(jnp.concatenate(r_parts, axis=-1) + br_ref[...])
    ig = jax.nn.sigmoid(jnp.concatenate(i_parts, axis=-1) + bi_ref[...])
    neg_lam = -lam_ref[...]
    softplus = jnp.maximum(neg_lam, 0.0) + jnp.log1p(jnp.exp(-jnp.abs(neg_lam)))
    log_a = -LRU_C * r * softplus
    a = jnp.exp(log_a)
    a_ref[...] = a
    one_minus_a2 = -jnp.tanh(log_a) * (a * a + 1.0)
    b_ref[...] = jnp.sqrt(one_minus_a2) * (ig * conv)


def _lru_scan_body(a_ref, b_ref, g_ref, y_ref):
    seq, ct = a_ref.shape
    row = lax.broadcasted_iota(I32, (SUBLANES, ct), 0)

    def step(gi, carry):
        off = pl.multiple_of(gi * SUBLANES, SUBLANES)
        a = a_ref[pl.ds(off, SUBLANES), :]
        b = b_ref[pl.ds(off, SUBLANES), :]
        for d in (1, 2, 4):
            keep = row >= d
            b = jnp.where(keep, a * pltpu.roll(b, d, 0) + b, b)
            a = jnp.where(keep, a * pltpu.roll(a, d, 0), a)
        h = a * carry + b
        y_ref[pl.ds(off, SUBLANES), :] = _gelu_tanh(g_ref[pl.ds(off, SUBLANES), :]) * h
        return jnp.broadcast_to(h[SUBLANES - 1:SUBLANES, :], (SUBLANES, ct))

    lax.fori_loop(0, seq // SUBLANES, step, jnp.zeros((SUBLANES, ct), F32))


def _rglru_mixer(x2, bsz, seq, idx, a_w_in, conv_w, conv_b, w_rgate, b_rgate, w_igate, b_igate, lam, layer):
    t = bsz * seq
    w = D_MODEL
    u = _matmul(x2, a_w_in, idx, 0, 2 * w, F32, f"lru_in_l{layer}")
    ts = 256
    nst = seq // ts
    vec = lambda: pl.BlockSpec((1, w), lambda b, s: (0, 0))
    a, bb = pl.pallas_call(
        _lru_gate_body,
        grid=(bsz, nst),
        in_specs=[pl.BlockSpec((ts, w), lambda b, s: (b * nst + s, 1)),
                  pl.BlockSpec((SUBLANES, w),
                               lambda b, s: (jnp.maximum((b * nst + s) * (ts // SUBLANES) - 1, 0), 1)),
                  pl.BlockSpec((CONV_WIDTH, w), lambda b, s: (0, 0)),
                  vec(),
                  pl.BlockSpec((LRU_HEADS, LRU_BLOCK, LRU_BLOCK), lambda b, s: (0, 0, 0)),
                  vec(),
                  pl.BlockSpec((LRU_HEADS, LRU_BLOCK, LRU_BLOCK), lambda b, s: (0, 0, 0)),
                  vec(), vec()],
        out_specs=[pl.BlockSpec((ts, w), lambda b, s: (b * nst + s, 0)),
                   pl.BlockSpec((ts, w), lambda b, s: (b * nst + s, 0))],
        out_shape=[jax.ShapeDtypeStruct((t, w), F32), jax.ShapeDtypeStruct((t, w), F32)],
        scratch_shapes=[pltpu.VMEM((ts + SUBLANES, w), F32)],
        compiler_params=_params(("parallel", "parallel")),
        name=f"lru_gates_l{layer}",
    )(u, u, conv_w[idx], conv_b[idx][None], w_rgate[idx], b_rgate[idx][None],
      w_igate[idx], b_igate[idx][None], lam[idx][None])
    ct = 256
    return pl.pallas_call(
        _lru_scan_body,
        grid=(bsz, w // ct),
        in_specs=[pl.BlockSpec((seq, ct), lambda b, j: (b, j)),
                  pl.BlockSpec((seq, ct), lambda b, j: (b, j)),
                  pl.BlockSpec((seq, ct), lambda b, j: (b, j))],
        out_specs=pl.BlockSpec((seq, ct), lambda b, j: (b, j)),
        out_shape=jax.ShapeDtypeStruct((t, w), F32),
        compiler_params=_params(("parallel", "parallel")),
        name=f"lru_scan_l{layer}",
    )(a, bb, u)


def _ret_body(lg_ref, q_ref, k_ref, v_ref, g_ref, y_ref, state_ref):
    c = pl.program_id(1)
    cs = RET_CHUNK
    heads = range(RET_HEADS)

    @pl.when(c == 0)
    def _():
        state_ref[...] = jnp.zeros_like(state_ref)

    qi = lax.broadcasted_iota(I32, (cs, cs), 0)
    ki = lax.broadcasted_iota(I32, (cs, cs), 1)
    rel = (qi - ki).astype(F32)
    pos_col = lax.broadcasted_iota(I32, (cs, 1), 0).astype(F32)
    lgs = [lg_ref[h] for h in heads]
    intra_decay = [jnp.where(rel >= 0, jnp.exp(lg * jnp.maximum(rel, 0.0)), 0.0) for lg in lgs]
    query_decay = [jnp.exp(lg * (pos_col + 1.0)) for lg in lgs]
    key_decay = [jnp.exp(lg * (cs - 1.0 - pos_col)) for lg in lgs]
    chunk_decay = [jnp.exp(jnp.full((1, 1), lg * cs, F32)) for lg in lgs]

    scale = RET_QK ** -0.5
    dn_nt = (((1,), (1,)), ((), ()))
    dn_tn = (((0,), (0,)), ((), ()))
    qs = [q_ref[:, h * RET_QK:(h + 1) * RET_QK] for h in heads]
    ks = [k_ref[:, h * RET_QK:(h + 1) * RET_QK] for h in heads]
    vs = [v_ref[:, h * RET_V:(h + 1) * RET_V] for h in heads]
    intra = [lax.dot_general(q, k, dn_nt, preferred_element_type=F32) * scale * d
             for q, k, d in zip(qs, ks, intra_decay)]
    states = [state_ref[h] for h in heads]
    cross = [jnp.dot(q, s.astype(BF16), preferred_element_type=F32) * d
             for q, s, d in zip(qs, states, query_decay)]
    outs = [jnp.dot(a.astype(BF16), v, preferred_element_type=F32) + x for a, v, x in zip(intra, vs, cross)]
    kds = [(k.astype(F32) * (scale * d)).astype(BF16) for k, d in zip(ks, key_decay)]
    upd = [lax.dot_general(kd, v, dn_tn, preferred_element_type=F32) for kd, v in zip(kds, vs)]
    for h in heads:
        state_ref[h] = chunk_decay[h] * states[h] + upd[h]
    for h in heads:
        out = outs[h]
        o = out * lax.rsqrt(jnp.mean(out * out, axis=-1, keepdims=True) + RET_EPS)
        g = g_ref[:, h * RET_V:(h + 1) * RET_V]
        y_ref[:, h * RET_V:(h + 1) * RET_V] = (g * jax.nn.sigmoid(g) * o).astype(y_ref.dtype)


def _retention_mixer(x2, bsz, seq, idx, b_w_in, layer):
    t = bsz * seq
    qk_w = RET_HEADS * RET_QK
    v_w = RET_HEADS * RET_V
    qkv = _matmul(x2, b_w_in, idx, 0, 2 * qk_w + v_w, BF16, f"ret_qkv_l{layer}")
    gate = _matmul(x2, b_w_in, idx, 2 * qk_w + v_w, v_w, F32, f"ret_gate_l{layer}")
    nc = seq // RET_CHUNK
    log_gamma = jnp.log1p(-jnp.exp2(-5.0 - jnp.arange(RET_HEADS, dtype=F32)))
    grid_spec = pltpu.PrefetchScalarGridSpec(
        num_scalar_prefetch=1,
        grid=(bsz, nc),
        in_specs=[pl.BlockSpec((RET_CHUNK, qk_w), lambda b, c, lg: (b * nc + c, 0)),
                  pl.BlockSpec((RET_CHUNK, qk_w), lambda b, c, lg: (b * nc + c, 1)),
                  pl.BlockSpec((RET_CHUNK, v_w), lambda b, c, lg: (b * nc + c, 1)),
                  pl.BlockSpec((RET_CHUNK, v_w), lambda b, c, lg: (b * nc + c, 0))],
        out_specs=pl.BlockSpec((RET_CHUNK, v_w), lambda b, c, lg: (b * nc + c, 0)),
        scratch_shapes=[pltpu.VMEM((RET_HEADS, RET_QK, RET_V), F32)],
    )
    return pl.pallas_call(
        _ret_body,
        grid_spec=grid_spec,
        out_shape=jax.ShapeDtypeStruct((t, v_w), BF16),
        compiler_params=_params(("parallel", "arbitrary")),
        name=f"retention_l{layer}",
    )(log_gamma, qkv, qkv, qkv, gate)


def _attn_body(*refs, dilation, has_prev):
    if has_prev:
        q_ref, kc_ref, kp_ref, vc_ref, vp_ref, o_ref, st_ref = refs
    else:
        q_ref, kc_ref, vc_ref, o_ref, st_ref = refs
    n = pl.program_id(2)
    nw = q_ref.shape[0]
    dh = ATTN_HEAD_DIM
    scale = dh ** -0.5
    qi = lax.broadcasted_iota(I32, (nw, nw), 0)
    ki = lax.broadcasted_iota(I32, (nw, nw), 1)
    steps_c = qi - ki
    valid_c = steps_c >= 0
    dist_c = (steps_c * dilation).astype(F32)
    q = q_ref[...]
    kc = kc_ref[...]
    vc = vc_ref[...]
    lane = lax.broadcasted_iota(I32, (nw, LANES), 1)
    dn = (((1,), (1,)), ((), ()))
    heads = range(ATTN_HEADS)
    sls = [slice(h * dh, (h + 1) * dh) for h in heads]
    slopes = [2.0 ** (-8.0 * (h + 1) / ATTN_HEADS) for h in heads]
    sc_c = [lax.dot_general(q[:, sl], kc[:, sl], dn, preferred_element_type=F32) for sl in sls]
    sc_c = [jnp.where(valid_c, s * scale - sp * dist_c, -jnp.inf) for s, sp in zip(sc_c, slopes)]
    ms = [jnp.max(a, axis=-1, keepdims=True) for a in sc_c]
    if has_prev:
        steps_p = steps_c + nw
        valid_p = steps_p <= jnp.where(n > 0, nw, -1)
        dist_p = (steps_p * dilation).astype(F32)
        kp = kp_ref[...]
        vp = vp_ref[...]
        sc_p = [lax.dot_general(q[:, sl], kp[:, sl], dn, preferred_element_type=F32) for sl in sls]
        sc_p = [jnp.where(valid_p, s * scale - sp * dist_p, -jnp.inf) for s, sp in zip(sc_p, slopes)]
        ms = [jnp.maximum(m, jnp.max(b, axis=-1, keepdims=True)) for m, b in zip(ms, sc_p)]
    pc = [jnp.exp(a - m) for a, m in zip(sc_c, ms)]
    sums = [jnp.sum(a, axis=-1, keepdims=True) for a in pc]
    outs = [jnp.dot(a.astype(BF16), vc[:, sl], preferred_element_type=F32) for a, sl in zip(pc, sls)]
    if has_prev:
        pp = [jnp.exp(b - m) for b, m in zip(sc_p, ms)]
        sums = [s + jnp.sum(b, axis=-1, keepdims=True) for s, b in zip(sums, pp)]
        outs = [o + jnp.dot(b.astype(BF16), vp[:, sl], preferred_element_type=F32)
                for o, b, sl in zip(outs, pp, sls)]
    stats = jnp.zeros((nw, LANES), F32)
    for h, sl, o, m, sm in zip(heads, sls, outs, ms, sums):
        o_ref[:, sl] = o
        stats = jnp.where(lane == h, m, stats)
        stats = jnp.where(lane == ATTN_HEADS + h, sm, stats)
    st_ref[...] = stats


def _merge_body(o0_ref, o1_ref, o2_ref, s0_ref, s1_ref, s2_ref, y_ref):
    tm = y_ref.shape[0]
    stats = [s0_ref[...], s1_ref[...], s2_ref[...]]
    lane = lax.broadcasted_iota(I32, (tm, LANES), 1)
    maxes = [s for s in stats]
    sums = [pltpu.roll(s, LANES - ATTN_HEADS, 1) for s in stats]
    m_all = jnp.maximum(jnp.maximum(maxes[0], maxes[1]), maxes[2])
    wgt = [jnp.exp(m - m_all) for m in maxes]
    den = wgt[0] * sums[0] + wgt[1] * sums[1] + wgt[2] * sums[2]
    er = lax.broadcasted_iota(I32, (LANES, D_MODEL), 0)
    ec = lax.broadcasted_iota(I32, (LANES, D_MODEL), 1)
    expand = (ec // ATTN_HEAD_DIM == er).astype(F32)
    acc = jnp.zeros((tm, D_MODEL), F32)
    for w, o_ref in zip(wgt, (o0_ref, o1_ref, o2_ref)):
        coef = jnp.where(lane < ATTN_HEADS, w / den, 0.0)
        acc = acc + jnp.dot(coef, expand, preferred_element_type=F32,
                            precision=lax.Precision.HIGHEST) * o_ref[...]
    y_ref[...] = acc


def _attention_mixer(x2, bsz, seq, idx, c_w_in, layer):
    t = bsz * seq
    w = D_MODEL
    n_groups = len(ATTN_PATTERNS)
    outs, stats = [], []
    for g, (window, dilation) in enumerate(ATTN_PATTERNS):
        nw = window // dilation
        sub = seq // dilation
        assert nw == 128 and sub % nw == 0
        nb = sub // nw

        def to_sub(a):
            return a.reshape(bsz, sub, dilation, -1).transpose(0, 2, 1, 3).reshape(t, -1)

        def from_sub(a):
            return a.reshape(bsz, dilation, sub, -1).transpose(0, 2, 1, 3).reshape(t, -1)

        xs = x2 if dilation == 1 else to_sub(x2)
        qkv = _matmul(xs, c_w_in, idx, g * w, 3 * w, BF16, f"attn_qkv_g{g}_l{layer}",
                      col_block_stride=n_groups)
        cur = lambda slab: pl.BlockSpec((nw, w), lambda b, r, n, slab=slab: ((b * dilation + r) * nb + n, slab))
        prev = lambda slab: pl.BlockSpec(
            (nw, w), lambda b, r, n, slab=slab: ((b * dilation + r) * nb + jnp.maximum(n - 1, 0), slab))
        has_prev = nb > 1
        in_specs = [cur(0), cur(1), prev(1), cur(2), prev(2)] if has_prev else [cur(0), cur(1), cur(2)]
        o, st = pl.pallas_call(
            functools.partial(_attn_body, dilation=dilation, has_prev=has_prev),
            grid=(bsz, dilation, nb),
            in_specs=in_specs,
            out_specs=[pl.BlockSpec((nw, w), lambda b, r, n: ((b * dilation + r) * nb + n, 0)),
                       pl.BlockSpec((nw, LANES), lambda b, r, n: ((b * dilation + r) * nb + n, 0))],
            out_shape=[jax.ShapeDtypeStruct((t, w), F32), jax.ShapeDtypeStruct((t, LANES), F32)],
            compiler_params=_params(("parallel", "parallel", "parallel")),
            name=f"attn_g{g}_l{layer}",
        )(*([qkv] * len(in_specs)))
        outs.append(o if dilation == 1 else from_sub(o))
        stats.append(st if dilation == 1 else from_sub(st))
    tm = ROW_TM
    return pl.pallas_call(
        _merge_body,
        grid=(t // tm,),
        in_specs=[pl.BlockSpec((tm, w), lambda i: (i, 0))] * 3 + [pl.BlockSpec((tm, LANES), lambda i: (i, 0))] * 3,
        out_specs=pl.BlockSpec((tm, w), lambda i: (i, 0)),
        out_shape=jax.ShapeDtypeStruct((t, w), F32),
        compiler_params=_params(("parallel",)),
        name=f"attn_merge_l{layer}",
    )(*outs, *stats)


def kernel(x, a_w_in, a_conv_w, a_conv_b, a_w_rgate, a_b_rgate, a_w_igate, a_b_igate, a_lambda, a_w_out,
           b_w_in, b_w_out, c_w_in, c_w_out, ln_gain, ln_bias, moe_w_router, moe_b_router, moe_w_gu,
           moe_b_gu, moe_w_down, moe_b_down):
    bsz, seq, d = x.shape
    t = bsz * seq
    x2 = x.reshape(t, d)
    b_gu4 = moe_b_gu[:, :, None, :]
    b_down4 = moe_b_down[:, :, None, :]
    for layer in range(DEPTH):
        kind, idx = layer % N_MIXERS, layer // N_MIXERS
        if kind == 0:
            y = _rglru_mixer(x2, bsz, seq, idx, a_w_in, a_conv_w, a_conv_b, a_w_rgate, a_b_rgate,
                             a_w_igate, a_b_igate, a_lambda, layer)
            w_out = a_w_out
        elif kind == 1:
            y = _retention_mixer(x2, bsz, seq, idx, b_w_in, layer)
            w_out = b_w_out
        else:
            y = _attention_mixer(x2, bsz, seq, idx, c_w_in, layer)
            w_out = c_w_out
        xt, route_f, route_i, counts = _post_mixer(
            y, w_out, idx, x2, ln_gain[layer, 0][None], ln_bias[layer, 0][None],
            moe_w_router[layer], moe_b_router[layer][None], f"post_mixer_l{layer}")
        block_exp, n_used, src_rows, dst_rows = _route_tables(route_i, counts, t, f"row_table_l{layer}")
        y_tk = _moe_experts(xt, block_exp, n_used, src_rows, dst_rows,
                            moe_w_gu, b_gu4, moe_w_down, b_down4, layer, t)
        x2 = _combine(y_tk, route_f, xt, ln_gain[layer, 1][None], ln_bias[layer, 1][None], t,
                      f"combine_l{layer}")
    return x2.reshape(bsz, seq, d)
```

```python
import functools
import math

import jax
import jax.numpy as jnp
from jax import lax
from jax.experimental import pallas as pl
from jax.experimental.pallas import tpu as pltpu

F32 = jnp.float32
BF16 = jnp.bfloat16
I32 = jnp.int32

D_MODEL = 1024
DEPTH = 4
N_MIXERS = 3
DEEPNORM_ALPHA = (2.0 * DEPTH) ** 0.25
LN_EPS = 1e-5
LRU_HEADS = 8
LRU_BLOCK = D_MODEL // LRU_HEADS
CONV_WIDTH = 4
LRU_C = 8.0
RET_HEADS = 4
RET_QK = D_MODEL // RET_HEADS
RET_V = 2 * RET_QK
RET_CHUNK = 128
RET_EPS = 1e-6
ATTN_PATTERNS = ((128, 1), (512, 4), (2048, 16))
ATTN_HEADS = 16
ATTN_HEAD_DIM = D_MODEL // ATTN_HEADS
N_EXPERTS = 32
TOP_K = 4
D_FF = D_MODEL
SWIGLU_LIMIT = 7.0
SWIGLU_ALPHA = 1.702

SUBLANES = 8
LANES = 128
LANE_TILES = D_MODEL // LANES
VMEM_LIMIT = 56 * 1024 * 1024

MOE_TM = 256
ROW_TM = 256


def _params(sem, vmem=VMEM_LIMIT):
    return pltpu.CompilerParams(dimension_semantics=sem, vmem_limit_bytes=vmem)


def _mm_body(x_ref, w_ref, o_ref, wbf_ref):
    @pl.when(pl.program_id(1) == 0)
    def _():
        wbf_ref[...] = w_ref[...].astype(BF16)

    o_ref[...] = jnp.dot(x_ref[...].astype(BF16), wbf_ref[...],
                         preferred_element_type=F32).astype(o_ref.dtype)


def _matmul(x, w_stack, idx, col0, ncols, out_dtype, name, tm=1024, tn=1024, col_block_stride=1):
    t, k = x.shape
    assert col0 % tn == 0 and ncols % tn == 0 and t % tm == 0
    return pl.pallas_call(
        _mm_body,
        grid=(ncols // tn, t // tm),
        in_specs=[pl.BlockSpec((tm, k), lambda n, m: (m, 0)),
                  pl.BlockSpec((None, k, tn), lambda n, m: (idx, 0, col0 // tn + n * col_block_stride))],
        out_specs=pl.BlockSpec((tm, tn), lambda n, m: (m, n)),
        out_shape=jax.ShapeDtypeStruct((t, ncols), out_dtype),
        scratch_shapes=[pltpu.VMEM((k, tn), BF16)],
        compiler_params=_params(("arbitrary", "arbitrary")),
        name=name,
    )(x, w_stack)


def _layer_norm(z, gain, bias):
    mu = jnp.mean(z, axis=-1, keepdims=True)
    zc = z - mu
    var = jnp.mean(zc * zc, axis=-1, keepdims=True)
    return zc * lax.rsqrt(var + LN_EPS) * gain + bias


def _store_token_tiles(ref, val):
    tm = val.shape[0]
    for c in range(LANE_TILES):
        ref[pl.ds(c, tm, stride=LANE_TILES), :] = val[:, c * LANES:(c + 1) * LANES]


def _load_token_tiles(ref, tm):
    return jnp.concatenate([ref[pl.ds(c, tm, stride=LANE_TILES), :] for c in range(LANE_TILES)], axis=-1)


def _post_body(y_ref, w_ref, x_ref, g_ref, b_ref, wr_ref, br_ref,
               xt_ref, rf_ref, ri_ref, cnt_ref, wbf_ref, run_ref):
    i = pl.program_id(0)
    tm = x_ref.shape[0]

    @pl.when(i == 0)
    def _():
        wbf_ref[...] = w_ref[...].astype(BF16)
        run_ref[...] = jnp.zeros_like(run_ref)

    z = DEEPNORM_ALPHA * x_ref[...] + jnp.dot(y_ref[...].astype(BF16), wbf_ref[...],
                                              preferred_element_type=F32)
    xn = _layer_norm(z, g_ref[...], b_ref[...])
    _store_token_tiles(xt_ref, xn)

    logits = jnp.dot(xn, wr_ref[...], preferred_element_type=F32,
                     precision=lax.Precision.HIGHEST) + br_ref[...]
    lane_e = lax.broadcasted_iota(I32, (tm, N_EXPERTS), 1).astype(F32)
    work = logits
    top_l, top_e, onehots = [], [], []
    for _ in range(TOP_K):
        m = jnp.max(work, axis=-1, keepdims=True)
        idx = jnp.min(jnp.where(work == m, lane_e, float(N_EXPERTS)), axis=-1, keepdims=True)
        oh = lane_e == idx
        top_l.append(m)
        top_e.append(idx)
        onehots.append(oh)
        work = jnp.where(oh, -jnp.inf, work)
    exps = [jnp.exp(l - top_l[0]) for l in top_l]
    denom = exps[0] + exps[1] + exps[2] + exps[3]
    gates = [e / denom for e in exps]

    sel = jnp.zeros((tm, N_EXPERTS), F32)
    for oh in onehots:
        sel = sel + oh.astype(F32)
    ri = lax.broadcasted_iota(I32, (tm, tm), 0)
    ci = lax.broadcasted_iota(I32, (tm, tm), 1)
    tri = (ci < ri).astype(BF16)
    before = jnp.dot(tri, sel.astype(BF16), preferred_element_type=F32) + run_ref[...]
    ranks = [jnp.sum(jnp.where(oh, before, 0.0), axis=-1, keepdims=True) for oh in onehots]
    run_ref[...] = run_ref[...] + jnp.sum(sel, axis=0, keepdims=True)
    cnt_ref[...] = run_ref[...]

    lane = lax.broadcasted_iota(I32, (tm, LANES), 1)
    rf = jnp.zeros((tm, LANES), F32)
    rint = jnp.zeros((tm, LANES), I32)
    for k in range(TOP_K):
        rf = jnp.where(lane == k, gates[k], rf)
        rint = jnp.where(lane == k, top_e[k].astype(I32), rint)
        rint = jnp.where(lane == TOP_K + k, ranks[k].astype(I32), rint)
    rf_ref[...] = rf
    ri_ref[...] = rint


def _post_mixer(y, w_out_stack, idx, x, gain, bias, w_router, b_router, name):
    t, kd = y.shape
    tm = ROW_TM
    return pl.pallas_call(
        _post_body,
        grid=(t // tm,),
        in_specs=[pl.BlockSpec((tm, kd), lambda i: (i, 0)),
                  pl.BlockSpec((None, kd, D_MODEL), lambda i: (idx, 0, 0)),
                  pl.BlockSpec((tm, D_MODEL), lambda i: (i, 0)),
                  pl.BlockSpec((1, D_MODEL), lambda i: (0, 0)),
                  pl.BlockSpec((1, D_MODEL), lambda i: (0, 0)),
                  pl.BlockSpec((D_MODEL, N_EXPERTS), lambda i: (0, 0)),
                  pl.BlockSpec((1, N_EXPERTS), lambda i: (0, 0))],
        out_specs=[pl.BlockSpec((tm * LANE_TILES, LANES), lambda i: (i, 0)),
                   pl.BlockSpec((tm, LANES), lambda i: (i, 0)),
                   pl.BlockSpec((tm, LANES), lambda i: (i, 0)),
                   pl.BlockSpec((1, N_EXPERTS), lambda i: (0, 0))],
        out_shape=[jax.ShapeDtypeStruct((t * LANE_TILES, LANES), F32),
                   jax.ShapeDtypeStruct((t, LANES), F32),
                   jax.ShapeDtypeStruct((t, LANES), I32),
                   jax.ShapeDtypeStruct((1, N_EXPERTS), F32)],
        scratch_shapes=[pltpu.VMEM((kd, D_MODEL), BF16), pltpu.VMEM((1, N_EXPERTS), F32)],
        compiler_params=_params(("arbitrary",)),
        name=name,
    )(y, w_out_stack, x, gain, bias, w_router, b_router)


def _gather_copy(x_hbm, xbuf, sem, src_row, r):
    return pltpu.make_async_copy(
        x_hbm.at[pl.ds(pl.multiple_of(src_row, SUBLANES), SUBLANES), :],
        xbuf.at[pl.ds(pl.multiple_of(r * SUBLANES, SUBLANES), SUBLANES), :],
        sem)


def _scatter_copy(ybuf, y_hbm, sem, dst_row, r):
    return pltpu.make_async_copy(
        ybuf.at[pl.ds(pl.multiple_of(r * SUBLANES, SUBLANES), SUBLANES), :],
        y_hbm.at[pl.ds(pl.multiple_of(dst_row, SUBLANES), SUBLANES), :],
        sem)


def _moe_body(be_ref, nu_ref, src_ref, srcn_ref, dst_ref, x_hbm, wgu_ref, bgu_ref, wd_ref, bd_ref,
              y_hbm, xbuf, ybuf, wgu_bf, wd_bf, gsem, ssem):
    i = pl.program_id(0)
    nb = pl.num_programs(0)
    nused = nu_ref[0]
    slot = i % 2
    rows = MOE_TM * LANE_TILES
    group = 32

    def gather_all(idx_ref, s):
        def body(g, c):
            for q in range(group):
                r = g * group + q
                _gather_copy(x_hbm, xbuf.at[s], gsem.at[s], idx_ref[0, r], r).start(priority=q % 2)
            return c
        lax.fori_loop(0, MOE_TM // group, body, 0)

    def scatter_all(idx_ref, s):
        def body(g, c):
            for q in range(group):
                r = g * group + q
                _scatter_copy(ybuf.at[s], y_hbm, ssem.at[s], idx_ref[0, r], r).start(priority=q % 2)
            return c
        lax.fori_loop(0, MOE_TM // group, body, 0)

    def wait_scatter(s):
        pltpu.make_async_copy(ybuf.at[s], y_hbm.at[pl.ds(0, rows), :], ssem.at[s]).wait()

    @pl.when(i == 0)
    def _():
        ybuf[...] = jnp.zeros_like(ybuf)
        tail = y_hbm.at[pl.ds(y_hbm.shape[0] - 2 * rows, 2 * rows), :]
        for s in range(2):
            cp = pltpu.make_async_copy(ybuf.at[s], tail.at[pl.ds(s * rows, rows), :], ssem.at[s])
            cp.start()
            cp.wait()

    @pl.when(i < nused)
    def _():
        @pl.when(i == 0)
        def _():
            gather_all(src_ref, 0)

        @pl.when(i + 1 < nused)
        def _():
            gather_all(srcn_ref, 1 - slot)

        changed = jnp.logical_or(i == 0, be_ref[i] != be_ref[jnp.maximum(i - 1, 0)])

        @pl.when(changed)
        def _():
            wgu_bf[...] = wgu_ref[...].astype(BF16)
            wd_bf[...] = wd_ref[...].astype(BF16)

        pltpu.make_async_copy(x_hbm.at[pl.ds(0, rows), :], xbuf.at[slot], gsem.at[slot]).wait()
        x = _load_token_tiles(xbuf.at[slot], MOE_TM).astype(BF16)
        gu = jnp.dot(x, wgu_bf[...], preferred_element_type=F32) + bgu_ref[...]
        gate = jnp.minimum(gu[:, :D_FF], SWIGLU_LIMIT)
        up = jnp.clip(gu[:, D_FF:], -SWIGLU_LIMIT, SWIGLU_LIMIT)
        hidden = (up + 1.0) * gate * jax.nn.sigmoid(SWIGLU_ALPHA * gate)
        y = jnp.dot(hidden.astype(BF16), wd_bf[...], preferred_element_type=F32) + bd_ref[...]

        @pl.when(i >= 2)
        def _():
            wait_scatter(slot)

        _store_token_tiles(ybuf.at[slot], y)
        scatter_all(dst_ref, slot)

    @pl.when(i == nb - 1)
    def _():
        @pl.when(nused >= 1)
        def _():
            wait_scatter((nused - 1) % 2)

        @pl.when(nused >= 2)
        def _():
            wait_scatter(nused % 2)


def _moe_experts(xt, block_exp, n_used, src_rows, dst_rows, w_gu, b_gu, w_down, b_down, layer, n_tok):
    nb = block_exp.shape[0]
    rows = MOE_TM * LANE_TILES
    n_out_rows = (TOP_K * n_tok + 2 * MOE_TM) * LANE_TILES
    smem_blk = lambda f: pl.BlockSpec((None, 1, MOE_TM), f, memory_space=pltpu.SMEM)
    grid_spec = pltpu.PrefetchScalarGridSpec(
        num_scalar_prefetch=2,
        grid=(nb,),
        in_specs=[smem_blk(lambda i, be, nu: (i, 0, 0)),
                  smem_blk(lambda i, be, nu: (jnp.minimum(i + 1, nb - 1), 0, 0)),
                  smem_blk(lambda i, be, nu: (i, 0, 0)),
                  pl.BlockSpec(memory_space=pl.ANY),
                  pl.BlockSpec((None, None, D_MODEL, 2 * D_FF), lambda i, be, nu: (layer, be[i], 0, 0)),
                  pl.BlockSpec((None, None, 1, 2 * D_FF), lambda i, be, nu: (layer, be[i], 0, 0)),
                  pl.BlockSpec((None, None, D_FF, D_MODEL), lambda i, be, nu: (layer, be[i], 0, 0)),
                  pl.BlockSpec((None, None, 1, D_MODEL), lambda i, be, nu: (layer, be[i], 0, 0))],
        out_specs=pl.BlockSpec(memory_space=pl.ANY),
        scratch_shapes=[pltpu.VMEM((2, rows, LANES), F32),
                        pltpu.VMEM((2, rows, LANES), F32),
                        pltpu.VMEM((D_MODEL, 2 * D_FF), BF16),
                        pltpu.VMEM((D_FF, D_MODEL), BF16),
                        pltpu.SemaphoreType.DMA((2,)),
                        pltpu.SemaphoreType.DMA((2,))],
    )
    return pl.pallas_call(
        _moe_body,
        grid_spec=grid_spec,
        out_shape=jax.ShapeDtypeStruct((n_out_rows, LANES), F32),
        compiler_params=_params(("arbitrary",)),
        name=f"moe_experts_l{layer}",
    )(block_exp, n_used, src_rows, src_rows, dst_rows, xt, w_gu, b_gu, w_down, b_down)


ROW_TABLE_CHUNK = 4096


def _row_table_body(lo_ref, hi_ref, dest_ref, tbl_ref, *, n_assign):
    i = pl.program_id(0)

    @pl.when(i == 0)
    def _():
        def fill_range(e, c):
            def fill(r, c2):
                tbl_ref[r] = n_assign + jnp.bitwise_and(r, 2 * MOE_TM - 1)
                return c2
            lax.fori_loop(lo_ref[e], hi_ref[e], fill, 0)
            return c
        lax.fori_loop(0, lo_ref.shape[0], fill_range, 0)

    def body(a, c):
        tbl_ref[dest_ref[0, a]] = i * ROW_TABLE_CHUNK + a
        return c
    lax.fori_loop(0, ROW_TABLE_CHUNK, body, 0, unroll=8)


def _row_table(dest, pad_lo, pad_hi, n_rows, name):
    n_assign = dest.shape[0]
    steps = n_assign // ROW_TABLE_CHUNK
    grid_spec = pltpu.PrefetchScalarGridSpec(
        num_scalar_prefetch=2,
        grid=(steps,),
        in_specs=[pl.BlockSpec((None, 1, ROW_TABLE_CHUNK), lambda i, lo, hi: (i, 0, 0),
                               memory_space=pltpu.SMEM)],
        out_specs=pl.BlockSpec(memory_space=pltpu.SMEM),
    )
    return pl.pallas_call(
        functools.partial(_row_table_body, n_assign=n_assign),
        grid_spec=grid_spec,
        out_shape=jax.ShapeDtypeStruct((n_rows,), I32),
        compiler_params=_params(("arbitrary",)),
        name=name,
    )(pad_lo, pad_hi, dest.reshape(steps, 1, ROW_TABLE_CHUNK))


def _route_tables(route_i, counts, n_tok, name):
    n_assign = n_tok * TOP_K
    nb = n_assign // MOE_TM + N_EXPERTS
    counts = counts.reshape(N_EXPERTS).astype(I32)
    padded = ((counts + MOE_TM - 1) // MOE_TM) * MOE_TM
    pad_end = jnp.cumsum(padded)
    pad_start = pad_end - padded
    eid = route_i[:, :TOP_K]
    rank = route_i[:, TOP_K:2 * TOP_K]
    dest = (pad_start[eid] + rank).reshape(n_assign)
    pad_lo = jnp.concatenate([pad_start + counts, pad_end[-1:]]).astype(I32)
    pad_hi = jnp.concatenate([pad_end, jnp.full((1,), nb * MOE_TM, I32)]).astype(I32)
    row_assign = _row_table(dest, pad_lo, pad_hi, nb * MOE_TM, name)
    tok = row_assign // TOP_K
    src_rows = jnp.minimum(tok, n_tok - 1) * LANE_TILES
    dst_rows = jnp.where(row_assign < n_assign, (row_assign % TOP_K) * n_tok + tok, row_assign) * LANE_TILES
    block_start = jnp.arange(nb, dtype=I32) * MOE_TM
    block_exp = jnp.minimum(jnp.sum((pad_end[None, :] <= block_start[:, None]).astype(I32), axis=1),
                            N_EXPERTS - 1)
    n_used = (pad_end[-1:] // MOE_TM).astype(I32)
    return block_exp, n_used, src_rows.reshape(nb, 1, MOE_TM), dst_rows.reshape(nb, 1, MOE_TM)


def _combine_body(y0_ref, y1_ref, y2_ref, y3_ref, rf_ref, xt_ref, g_ref, b_ref, o_ref):
    tm = o_ref.shape[0]
    rf = rf_ref[...]
    parts = []
    for c in range(LANE_TILES):
        acc = DEEPNORM_ALPHA * xt_ref[pl.ds(c, tm, stride=LANE_TILES), :]
        for k, yk in enumerate((y0_ref, y1_ref, y2_ref, y3_ref)):
            acc = acc + rf[:, k:k + 1] * yk[pl.ds(c, tm, stride=LANE_TILES), :]
        parts.append(acc)
    z = jnp.concatenate(parts, axis=-1)
    o_ref[...] = _layer_norm(z, g_ref[...], b_ref[...])


def _combine(y_tk, route_f, xt, gain, bias, n_tok, name):
    tm = ROW_TM
    nblk = n_tok // tm
    yspec = lambda k: pl.BlockSpec((tm * LANE_TILES, LANES), lambda i: (k * nblk + i, 0))
    return pl.pallas_call(
        _combine_body,
        grid=(nblk,),
        in_specs=[yspec(0), yspec(1), yspec(2), yspec(3),
                  pl.BlockSpec((tm, LANES), lambda i: (i, 0)),
                  pl.BlockSpec((tm * LANE_TILES, LANES), lambda i: (i, 0)),
                  pl.BlockSpec((1, D_MODEL), lambda i: (0, 0)),
                  pl.BlockSpec((1, D_MODEL), lambda i: (0, 0))],
        out_specs=pl.BlockSpec((tm, D_MODEL), lambda i: (i, 0)),
        out_shape=jax.ShapeDtypeStruct((n_tok, D_MODEL), F32),
        compiler_params=_params(("parallel",)),
        name=name,
    )(y_tk, y_tk, y_tk, y_tk, route_f, xt, gain, bias)


def _gelu_tanh(x):
    return 0.5 * x * (1.0 + jnp.tanh(math.sqrt(2.0 / math.pi) * (x + 0.044715 * (x * x * x))))


def _lru_gate_body(rec_ref, prev_ref, cw_ref, cb_ref, wr_ref, br_ref, wi_ref, bi_ref, lam_ref,
                   a_ref, b_ref, ext_ref):
    s = pl.program_id(1)
    ts = rec_ref.shape[0]
    cur = rec_ref[...]
    ext_ref[0:SUBLANES, :] = jnp.where(s == 0, 0.0, prev_ref[...])
    ext_ref[SUBLANES:, :] = cur
    conv = cb_ref[...] + cw_ref[CONV_WIDTH - 1:CONV_WIDTH, :] * cur
    for j in range(CONV_WIDTH - 1):
        conv = conv + cw_ref[j:j + 1, :] * ext_ref[pl.ds(SUBLANES - (CONV_WIDTH - 1) + j, ts), :]
    xb = conv.astype(BF16)
    r_parts, i_parts = [], []
    for h in range(LRU_HEADS):
        xh = xb[:, h * LRU_BLOCK:(h + 1) * LRU_BLOCK]
        r_parts.append(jnp.dot(xh, wr_ref[h].astype(BF16), preferred_element_type=F32))
        i_parts.append(jnp.dot(xh, wi_ref[h].astype(BF16), preferred_element_type=F32))
    r = jax.nn.sigmoid(jnp.concatenate(r_parts, axis=-1) + br_ref[...])
    ig = jax.nn.sigmoid(jnp.concatenate(i_parts, axis=-1) + bi_ref[...])
    neg_lam = -lam_ref[...]
    softplus = jnp.maximum(neg_lam, 0.0) + jnp.log1p(jnp.exp(-jnp.abs(neg_lam)))
    log_a = -LRU_C * r * softplus
    a = jnp.exp(log_a)
    a_ref[...] = a
    one_minus_a2 = -jnp.tanh(log_a) * (a * a + 1.0)
    b_ref[...] = jnp.sqrt(one_minus_a2) * (ig * conv)


def _lru_scan_body(a_ref, b_ref, g_ref, y_ref):
    seq, ct = a_ref.shape
    row = lax.broadcasted_iota(I32, (SUBLANES, ct), 0)

    def step(gi, carry):
        off = pl.multiple_of(gi * SUBLANES, SUBLANES)
        a = a_ref[pl.ds(off, SUBLANES), :]
        b = b_ref[pl.ds(off, SUBLANES), :]
        for d in (1, 2, 4):
            keep = row >= d
            b = jnp.where(keep, a * pltpu.roll(b, d, 0) + b, b)
            a = jnp.where(keep, a * pltpu.roll(a, d, 0), a)
        h = a * carry + b
        y_ref[pl.ds(off, SUBLANES), :] = _gelu_tanh(g_ref[pl.ds(off, SUBLANES), :]) * h
        return jnp.broadcast_to(h[SUBLANES - 1:SUBLANES, :], (SUBLANES, ct))

    lax.fori_loop(0, seq // SUBLANES, step, jnp.zeros((SUBLANES, ct), F32))


def _rglru_mixer(x2, bsz, seq, idx, a_w_in, conv_w, conv_b, w_rgate, b_rgate, w_igate, b_igate, lam, layer):
    t = bsz * seq
    w = D_MODEL
    u = _matmul(x2, a_w_in, idx, 0, 2 * w, F32, f"lru_in_l{layer}")
    ts = 256
    nst = seq // ts
    vec = lambda: pl.BlockSpec((1, w), lambda b, s: (0, 0))
    a, bb = pl.pallas_call(
        _lru_gate_body,
        grid=(bsz, nst),
        in_specs=[pl.BlockSpec((ts, w), lambda b, s: (b * nst + s, 1)),
                  pl.BlockSpec((SUBLANES, w),
                               lambda b, s: (jnp.maximum((b * nst + s) * (ts // SUBLANES) - 1, 0), 1)),
                  pl.BlockSpec((CONV_WIDTH, w), lambda b, s: (0, 0)),
                  vec(),
                  pl.BlockSpec((LRU_HEADS, LRU_BLOCK, LRU_BLOCK), lambda b, s: (0, 0, 0)),
                  vec(),
                  pl.BlockSpec((LRU_HEADS, LRU_BLOCK, LRU_BLOCK), lambda b, s: (0, 0, 0)),
                  vec(), vec()],
        out_specs=[pl.BlockSpec((ts, w), lambda b, s: (b * nst + s, 0)),
                   pl.BlockSpec((ts, w), lambda b, s: (b * nst + s, 0))],
        out_shape=[jax.ShapeDtypeStruct((t, w), F32), jax.ShapeDtypeStruct((t, w), F32)],
        scratch_shapes=[pltpu.VMEM((ts + SUBLANES, w), F32)],
        compiler_params=_params(("parallel", "parallel")),
        name=f"lru_gates_l{layer}",
    )(u, u, conv_w[idx], conv_b[idx][None], w_rgate[idx], b_rgate[idx][None],
      w_igate[idx], b_igate[idx][None], lam[idx][None])
    ct = 256
    return pl.pallas_call(
        _lru_scan_body,
        grid=(bsz, w // ct),
        in_specs=[pl.BlockSpec((seq, ct), lambda b, j: (b, j)),
                  pl.BlockSpec((seq, ct), lambda b, j: (b, j)),
                  pl.BlockSpec((seq, ct), lambda b, j: (b, j))],
        out_specs=pl.BlockSpec((seq, ct), lambda b, j: (b, j)),
        out_shape=jax.ShapeDtypeStruct((t, w), F32),
        compiler_params=_params(("parallel", "parallel")),
        name=f"lru_scan_l{layer}",
    )(a, bb, u)


def _ret_body(lg_ref, q_ref, k_ref, v_ref, g_ref, y_ref, state_ref):
    c = pl.program_id(1)
    cs = RET_CHUNK
    heads = range(RET_HEADS)

    @pl.when(c == 0)
    def _():
        state_ref[...] = jnp.zeros_like(state_ref)

    qi = lax.broadcasted_iota(I32, (cs, cs), 0)
    ki = lax.broadcasted_iota(I32, (cs, cs), 1)
    rel = (qi - ki).astype(F32)
    pos_col = lax.broadcasted_iota(I32, (cs, 1), 0).astype(F32)
    lgs = [lg_ref[h] for h in heads]
    intra_decay = [jnp.where(rel >= 0, jnp.exp(lg * jnp.maximum(rel, 0.0)), 0.0) for lg in lgs]
    query_decay = [jnp.exp(lg * (pos_col + 1.0)) for lg in lgs]
    key_decay = [jnp.exp(lg * (cs - 1.0 - pos_col)) for lg in lgs]
    chunk_decay = [jnp.exp(jnp.full((1, 1), lg * cs, F32)) for lg in lgs]

    scale = RET_QK ** -0.5
    dn_nt = (((1,), (1,)), ((), ()))
    dn_tn = (((0,), (0,)), ((), ()))
    qs = [q_ref[:, h * RET_QK:(h + 1) * RET_QK] for h in heads]
    ks = [k_ref[:, h * RET_QK:(h + 1) * RET_QK] for h in heads]
    vs = [v_ref[:, h * RET_V:(h + 1) * RET_V] for h in heads]
    intra = [lax.dot_general(q, k, dn_nt, preferred_element_type=F32) * scale * d
             for q, k, d in zip(qs, ks, intra_decay)]
    states = [state_ref[h] for h in heads]
    cross = [jnp.dot(q, s.astype(BF16), preferred_element_type=F32) * d
             for q, s, d in zip(qs, states, query_decay)]
    outs = [jnp.dot(a.astype(BF16), v, preferred_element_type=F32) + x for a, v, x in zip(intra, vs, cross)]
    kds = [(k.astype(F32) * (scale * d)).astype(BF16) for k, d in zip(ks, key_decay)]
    upd = [lax.dot_general(kd, v, dn_tn, preferred_element_type=F32) for kd, v in zip(kds, vs)]
    for h in heads:
        state_ref[h] = chunk_decay[h] * states[h] + upd[h]
    for h in heads:
        out = outs[h]
        o = out * lax.rsqrt(jnp.mean(out * out, axis=-1, keepdims=True) + RET_EPS)
        g = g_ref[:, h * RET_V:(h + 1) * RET_V]
        y_ref[:, h * RET_V:(h + 1) * RET_V] = (g * jax.nn.sigmoid(g) * o).astype(y_ref.dtype)


def _retention_mixer(x2, bsz, seq, idx, b_w_in, layer):
    t = bsz * seq
    qk_w = RET_HEADS * RET_QK
    v_w = RET_HEADS * RET_V
    qkv = _matmul(x2, b_w_in, idx, 0, 2 * qk_w + v_w, BF16, f"ret_qkv_l{layer}")
    gate = _matmul(x2, b_w_in, idx, 2 * qk_w + v_w, v_w, F32, f"ret_gate_l{layer}")
    nc = seq // RET_CHUNK
    log_gamma = jnp.log1p(-jnp.exp2(-5.0 - jnp.arange(RET_HEADS, dtype=F32)))
    grid_spec = pltpu.PrefetchScalarGridSpec(
        num_scalar_prefetch=1,
        grid=(bsz, nc),
        in_specs=[pl.BlockSpec((RET_CHUNK, qk_w), lambda b, c, lg: (b * nc + c, 0)),
                  pl.BlockSpec((RET_CHUNK, qk_w), lambda b, c, lg: (b * nc + c, 1)),
                  pl.BlockSpec((RET_CHUNK, v_w), lambda b, c, lg: (b * nc + c, 1)),
                  pl.BlockSpec((RET_CHUNK, v_w), lambda b, c, lg: (b * nc + c, 0))],
        out_specs=pl.BlockSpec((RET_CHUNK, v_w), lambda b, c, lg: (b * nc + c, 0)),
        scratch_shapes=[pltpu.VMEM((RET_HEADS, RET_QK, RET_V), F32)],
    )
    return pl.pallas_call(
        _ret_body,
        grid_spec=grid_spec,
        out_shape=jax.ShapeDtypeStruct((t, v_w), BF16),
        compiler_params=_params(("parallel", "arbitrary")),
        name=f"retention_l{layer}",
    )(log_gamma, qkv, qkv, qkv, gate)


def _attn_body(*refs, dilation, has_prev):
    if has_prev:
        q_ref, kc_ref, kp_ref, vc_ref, vp_ref, o_ref, st_ref = refs
    else:
        q_ref, kc_ref, vc_ref, o_ref, st_ref = refs
    n = pl.program_id(2)
    nw = q_ref.shape[0]
    dh = ATTN_HEAD_DIM
    scale = dh ** -0.5
    qi = lax.broadcasted_iota(I32, (nw, nw), 0)
    ki = lax.broadcasted_iota(I32, (nw, nw), 1)
    steps_c = qi - ki
    valid_c = steps_c >= 0
    dist_c = (steps_c * dilation).astype(F32)
    q = q_ref[...]
    kc = kc_ref[...]
    vc = vc_ref[...]
    lane = lax.broadcasted_iota(I32, (nw, LANES), 1)
    dn = (((1,), (1,)), ((), ()))
    heads = range(ATTN_HEADS)
    sls = [slice(h * dh, (h + 1) * dh) for h in heads]
    slopes = [2.0 ** (-8.0 * (h + 1) / ATTN_HEADS) for h in heads]
    sc_c = [lax.dot_general(q[:, sl], kc[:, sl], dn, preferred_element_type=F32) for sl in sls]
    sc_c = [jnp.where(valid_c, s * scale - sp * dist_c, -jnp.inf) for s, sp in zip(sc_c, slopes)]
    ms = [jnp.max(a, axis=-1, keepdims=True) for a in sc_c]
    if has_prev:
        steps_p = steps_c + nw
        valid_p = steps_p <= jnp.where(n > 0, nw, -1)
        dist_p = (steps_p * dilation).astype(F32)
        kp = kp_ref[...]
        vp = vp_ref[...]
        sc_p = [lax.dot_general(q[:, sl], kp[:, sl], dn, preferred_element_type=F32) for sl in sls]
        sc_p = [jnp.where(valid_p, s * scale - sp * dist_p, -jnp.inf) for s, sp in zip(sc_p, slopes)]
        ms = [jnp.maximum(m, jnp.max(b, axis=-1, keepdims=True)) for m, b in zip(ms, sc_p)]
    pc = [jnp.exp(a - m) for a, m in zip(sc_c, ms)]
    sums = [jnp.sum(a, axis=-1, keepdims=True) for a in pc]
    outs = [jnp.dot(a.astype(BF16), vc[:, sl], preferred_element_type=F32) for a, sl in zip(pc, sls)]
    if has_prev:
        pp = [jnp.exp(b - m) for b, m in zip(sc_p, ms)]
        sums = [s + jnp.sum(b, axis=-1, keepdims=True) for s, b in zip(sums, pp)]
        outs = [o + jnp.dot(b.astype(BF16), vp[:, sl], preferred_element_type=F32)
                for o, b, sl in zip(outs, pp, sls)]
    stats = jnp.zeros((nw, LANES), F32)
    for h, sl, o, m, sm in zip(heads, sls, outs, ms, sums):
        o_ref[:, sl] = o
        stats = jnp.where(lane == h, m, stats)
        stats = jnp.where(lane == ATTN_HEADS + h, sm, stats)
    st_ref[...] = stats


def _merge_body(o0_ref, o1_ref, o2_ref, s0_ref, s1_ref, s2_ref, y_ref):
    tm = y_ref.shape[0]
    stats = [s0_ref[...], s1_ref[...], s2_ref[...]]
    lane = lax.broadcasted_iota(I32, (tm, LANES), 1)
    maxes = [s for s in stats]
    sums = [pltpu.roll(s, LANES - ATTN_HEADS, 1) for s in stats]
    m_all = jnp.maximum(jnp.maximum(maxes[0], maxes[1]), maxes[2])
    wgt = [jnp.exp(m - m_all) for m in maxes]
    den = wgt[0] * sums[0] + wgt[1] * sums[1] + wgt[2] * sums[2]
    er = lax.broadcasted_iota(I32, (LANES, D_MODEL), 0)
    ec = lax.broadcasted_iota(I32, (LANES, D_MODEL), 1)
    expand = (ec // ATTN_HEAD_DIM == er).astype(F32)
    acc = jnp.zeros((tm, D_MODEL), F32)
    for w, o_ref in zip(wgt, (o0_ref, o1_ref, o2_ref)):
        coef = jnp.where(lane < ATTN_HEADS, w / den, 0.0)
        acc = acc + jnp.dot(coef, expand, preferred_element_type=F32,
                            precision=lax.Precision.HIGHEST) * o_ref[...]
    y_ref[...] = acc


def _attention_mixer(x2, bsz, seq, idx, c_w_in, layer):
    t = bsz * seq
    w = D_MODEL
    n_groups = len(ATTN_PATTERNS)
    outs, stats = [], []
    for g, (window, dilation) in enumerate(ATTN_PATTERNS):
        nw = window // dilation
        sub = seq // dilation
        assert nw == 128 and sub % nw == 0
        nb = sub // nw

        def to_sub(a):
            return a.reshape(bsz, sub, dilation, -1).transpose(0, 2, 1, 3).reshape(t, -1)

        def from_sub(a):
            return a.reshape(bsz, dilation, sub, -1).transpose(0, 2, 1, 3).reshape(t, -1)

        xs = x2 if dilation == 1 else to_sub(x2)
        qkv = _matmul(xs, c_w_in, idx, g * w, 3 * w, BF16, f"attn_qkv_g{g}_l{layer}",
                      col_block_stride=n_groups)
        cur = lambda slab: pl.BlockSpec((nw, w), lambda b, r, n, slab=slab: ((b * dilation + r) * nb + n, slab))
        prev = lambda slab: pl.BlockSpec(
            (nw, w), lambda b, r, n, slab=slab: ((b * dilation + r) * nb + jnp.maximum(n - 1, 0), slab))
        has_prev = nb > 1
        in_specs = [cur(0), cur(1), prev(1), cur(2), prev(2)] if has_prev else [cur(0), cur(1), cur(2)]
        o, st = pl.pallas_call(
            functools.partial(_attn_body, dilation=dilation, has_prev=has_prev),
            grid=(bsz, dilation, nb),
            in_specs=in_specs,
            out_specs=[pl.BlockSpec((nw, w), lambda b, r, n: ((b * dilation + r) * nb + n, 0)),
                       pl.BlockSpec((nw, LANES), lambda b, r, n: ((b * dilation + r) * nb + n, 0))],
            out_shape=[jax.ShapeDtypeStruct((t, w), F32), jax.ShapeDtypeStruct((t, LANES), F32)],
            compiler_params=_params(("parallel", "parallel", "parallel")),
            name=f"attn_g{g}_l{layer}",
        )(*([qkv] * len(in_specs)))
        outs.append(o if dilation == 1 else from_sub(o))
        stats.append(st if dilation == 1 else from_sub(st))
    tm = ROW_TM
    return pl.pallas_call(
        _merge_body,
        grid=(t // tm,),
        in_specs=[pl.BlockSpec((tm, w), lambda i: (i, 0))] * 3 + [pl.BlockSpec((tm, LANES), lambda i: (i, 0))] * 3,
        out_specs=pl.BlockSpec((tm, w), lambda i: (i, 0)),
        out_shape=jax.ShapeDtypeStruct((t, w), F32),
        compiler_params=_params(("parallel",)),
        name=f"attn_merge_l{layer}",
    )(*outs, *stats)


def kernel(x, a_w_in, a_conv_w, a_conv_b, a_w_rgate, a_b_rgate, a_w_igate, a_b_igate, a_lambda, a_w_out,
           b_w_in, b_w_out, c_w_in, c_w_out, ln_gain, ln_bias, moe_w_router, moe_b_router, moe_w_gu,
           moe_b_gu, moe_w_down, moe_b_down):
    bsz, seq, d = x.shape
    t = bsz * seq
    x2 = x.reshape(t, d)
    b_gu4 = moe_b_gu[:, :, None, :]
    b_down4 = moe_b_down[:, :, None, :]
    for layer in range(DEPTH):
        kind, idx = layer % N_MIXERS, layer // N_MIXERS
        if kind == 0:
            y = _rglru_mixer(x2, bsz, seq, idx, a_w_in, a_conv_w, a_conv_b, a_w_rgate, a_b_rgate,
                             a_w_igate, a_b_igate, a_lambda, layer)
            w_out = a_w_out
        elif kind == 1:
            y = _retention_mixer(x2, bsz, seq, idx, b_w_in, layer)
            w_out = b_w_out
        else:
            y = _attention_mixer(x2, bsz, seq, idx, c_w_in, layer)
            w_out = c_w_out
        xt, route_f, route_i, counts = _post_mixer(
            y, w_out, idx, x2, ln_gain[layer, 0][None], ln_bias[layer, 0][None],
            moe_w_router[layer], moe_b_router[layer][None], f"post_mixer_l{layer}")
        block_exp, n_used, src_rows, dst_rows = _route_tables(route_i, counts, t, f"row_table_l{layer}")
        y_tk = _moe_experts(xt, block_exp, n_used, src_rows, dst_rows,
                            moe_w_gu, b_gu4, moe_w_down, b_down4, layer, t)
        x2 = _combine(y_tk, route_f, xt, ln_gain[layer, 1][None], ln_bias[layer, 1][None], t,
                      f"combine_l{layer}")
    return x2.reshape(bsz, seq, d)
```

```python
import functools
import math

import jax
import jax.numpy as jnp
from jax import lax
from jax.experimental import pallas as pl
from jax.experimental.pallas import tpu as pltpu

F32 = jnp.float32
BF16 = jnp.bfloat16
I32 = jnp.int32

D_MODEL = 1024
DEPTH = 4
N_MIXERS = 3
DEEPNORM_ALPHA = (2.0 * DEPTH) ** 0.25
LN_EPS = 1e-5
LRU_HEADS = 8
LRU_BLOCK = D_MODEL // LRU_HEADS
CONV_WIDTH = 4
LRU_C = 8.0
RET_HEADS = 4
RET_QK = D_MODEL // RET_HEADS
RET_V = 2 * RET_QK
RET_CHUNK = 128
RET_EPS = 1e-6
ATTN_PATTERNS = ((128, 1), (512, 4), (2048, 16))
ATTN_HEADS = 16
ATTN_HEAD_DIM = D_MODEL // ATTN_HEADS
N_EXPERTS = 32
TOP_K = 4
D_FF = D_MODEL
SWIGLU_LIMIT = 7.0
SWIGLU_ALPHA = 1.702

SUBLANES = 8
LANES = 128
LANE_TILES = D_MODEL // LANES
VMEM_LIMIT = 56 * 1024 * 1024

MOE_TM = 256
ROW_TM = 256


def _params(sem, vmem=VMEM_LIMIT):
    return pltpu.CompilerParams(dimension_semantics=sem, vmem_limit_bytes=vmem)


def _mm_body(x_ref, w_ref, o_ref, wbf_ref):
    @pl.when(pl.program_id(1) == 0)
    def _():
        wbf_ref[...] = w_ref[...].astype(BF16)

    o_ref[...] = jnp.dot(x_ref[...].astype(BF16), wbf_ref[...],
                         preferred_element_type=F32).astype(o_ref.dtype)


def _matmul(x, w_stack, idx, col0, ncols, out_dtype, name, tm=1024, tn=1024, col_block_stride=1):
    t, k = x.shape
    assert col0 % tn == 0 and ncols % tn == 0 and t % tm == 0
    return pl.pallas_call(
        _mm_body,
        grid=(ncols // tn, t // tm),
        in_specs=[pl.BlockSpec((tm, k), lambda n, m: (m, 0)),
                  pl.BlockSpec((None, k, tn), lambda n, m: (idx, 0, col0 // tn + n * col_block_stride))],
        out_specs=pl.BlockSpec((tm, tn), lambda n, m: (m, n)),
        out_shape=jax.ShapeDtypeStruct((t, ncols), out_dtype),
        scratch_shapes=[pltpu.VMEM((k, tn), BF16)],
        compiler_params=_params(("arbitrary", "arbitrary")),
        name=name,
    )(x, w_stack)


def _layer_norm(z, gain, bias):
    mu = jnp.mean(z, axis=-1, keepdims=True)
    zc = z - mu
    var = jnp.mean(zc * zc, axis=-1, keepdims=True)
    return zc * lax.rsqrt(var + LN_EPS) * gain + bias


def _store_token_tiles(ref, val):
    tm = val.shape[0]
    for c in range(LANE_TILES):
        ref[pl.ds(c, tm, stride=LANE_TILES), :] = val[:, c * LANES:(c + 1) * LANES]


def _load_token_tiles(ref, tm):
    return jnp.concatenate([ref[pl.ds(c, tm, stride=LANE_TILES), :] for c in range(LANE_TILES)], axis=-1)


def _post_body(y_ref, w_ref, x_ref, g_ref, b_ref, wr_ref, br_ref,
               xt_ref, rf_ref, ri_ref, cnt_ref, wbf_ref, wrh_ref, wrl_ref, run_ref):
    i = pl.program_id(0)
    tm = x_ref.shape[0]

    @pl.when(i == 0)
    def _():
        wbf_ref[...] = w_ref[...].astype(BF16)
        wr = wr_ref[...]
        wrh = wr.astype(BF16)
        wrh_ref[...] = wrh
        wrl_ref[...] = (wr - wrh.astype(F32)).astype(BF16)
        run_ref[...] = jnp.zeros_like(run_ref)

    z = DEEPNORM_ALPHA * x_ref[...] + jnp.dot(y_ref[...].astype(BF16), wbf_ref[...],
                                              preferred_element_type=F32)
    xn = _layer_norm(z, g_ref[...], b_ref[...])
    _store_token_tiles(xt_ref, xn)

    xh = xn.astype(BF16)
    xl = (xn - xh.astype(F32)).astype(BF16)
    logits_tm = (jnp.dot(xh, wrh_ref[...], preferred_element_type=F32)
                 + jnp.dot(xl, wrh_ref[...], preferred_element_type=F32)
                 + jnp.dot(xh, wrl_ref[...], preferred_element_type=F32) + br_ref[...])
    logits = jnp.transpose(logits_tm)[:N_EXPERTS, :]
    row_e = lax.broadcasted_iota(I32, (N_EXPERTS, tm), 0).astype(F32)
    work = logits
    top_l, top_e, onehots = [], [], []
    for _ in range(TOP_K):
        m = jnp.max(work, axis=0, keepdims=True)
        idx = jnp.min(jnp.where(work == m, row_e, float(N_EXPERTS)), axis=0, keepdims=True)
        oh = row_e == idx
        top_l.append(m)
        top_e.append(idx)
        onehots.append(oh)
        work = jnp.where(oh, -jnp.inf, work)
    exps = [jnp.exp(l - top_l[0]) for l in top_l]
    denom = exps[0] + exps[1] + exps[2] + exps[3]
    gates = [e / denom for e in exps]

    sel = jnp.zeros((N_EXPERTS, tm), F32)
    for oh in onehots:
        sel = sel + oh.astype(F32)
    ri = lax.broadcasted_iota(I32, (tm, tm), 0)
    ci = lax.broadcasted_iota(I32, (tm, tm), 1)
    earlier = (ri < ci).astype(BF16)
    before = jnp.dot(sel.astype(BF16), earlier, preferred_element_type=F32) + run_ref[...]
    ranks = [jnp.sum(jnp.where(oh, before, 0.0), axis=0, keepdims=True) for oh in onehots]
    run_ref[...] = run_ref[...] + jnp.sum(sel, axis=1, keepdims=True)
    cnt_ref[...] = run_ref[...]

    row8 = lax.broadcasted_iota(I32, (2 * TOP_K, tm), 0)
    gate_rows = jnp.zeros((2 * TOP_K, tm), F32)
    rint = jnp.zeros((2 * TOP_K, tm), I32)
    for k in range(TOP_K):
        gate_rows = jnp.where(row8 == k, gates[k], gate_rows)
        rint = jnp.where(row8 == k, top_e[k].astype(I32), rint)
        rint = jnp.where(row8 == TOP_K + k, ranks[k].astype(I32), rint)
    ri_ref[...] = rint
    pick = (lax.broadcasted_iota(I32, (2 * TOP_K, LANES), 0)
            == lax.broadcasted_iota(I32, (2 * TOP_K, LANES), 1)).astype(F32)
    rf_ref[...] = lax.dot_general(gate_rows, pick, (((0,), (0,)), ((), ())), preferred_element_type=F32,
                                  precision=lax.Precision.HIGHEST)


def _post_mixer(y, w_out_stack, idx, x, gain, bias, w_router, b_router, name):
    t, kd = y.shape
    tm = ROW_TM
    return pl.pallas_call(
        _post_body,
        grid=(t // tm,),
        in_specs=[pl.BlockSpec((tm, kd), lambda i: (i, 0)),
                  pl.BlockSpec((None, kd, D_MODEL), lambda i: (idx, 0, 0)),
                  pl.BlockSpec((tm, D_MODEL), lambda i: (i, 0)),
                  pl.BlockSpec((1, D_MODEL), lambda i: (0, 0)),
                  pl.BlockSpec((1, D_MODEL), lambda i: (0, 0)),
                  pl.BlockSpec((D_MODEL, LANES), lambda i: (0, 0)),
                  pl.BlockSpec((1, LANES), lambda i: (0, 0))],
        out_specs=[pl.BlockSpec((tm * LANE_TILES, LANES), lambda i: (i, 0)),
                   pl.BlockSpec((tm, LANES), lambda i: (i, 0)),
                   pl.BlockSpec((2 * TOP_K, tm), lambda i: (0, i)),
                   pl.BlockSpec((N_EXPERTS, 1), lambda i: (0, 0))],
        out_shape=[jax.ShapeDtypeStruct((t * LANE_TILES, LANES), F32),
                   jax.ShapeDtypeStruct((t, LANES), F32),
                   jax.ShapeDtypeStruct((2 * TOP_K, t), I32),
                   jax.ShapeDtypeStruct((N_EXPERTS, 1), F32)],
        scratch_shapes=[pltpu.VMEM((kd, D_MODEL), BF16), pltpu.VMEM((D_MODEL, LANES), BF16),
                        pltpu.VMEM((D_MODEL, LANES), BF16), pltpu.VMEM((N_EXPERTS, 1), F32)],
        compiler_params=_params(("arbitrary",)),
        name=name,
    )(y, w_out_stack, x, gain, bias,
      jnp.pad(w_router, ((0, 0), (0, LANES - N_EXPERTS))),
      jnp.pad(b_router.reshape(1, N_EXPERTS), ((0, 0), (0, LANES - N_EXPERTS))))


def _gather_copy(x_hbm, xbuf, sem, src_row, r):
    return pltpu.make_async_copy(
        x_hbm.at[pl.ds(pl.multiple_of(src_row, SUBLANES), SUBLANES), :],
        xbuf.at[pl.ds(pl.multiple_of(r * SUBLANES, SUBLANES), SUBLANES), :],
        sem)


def _scatter_copy(ybuf, y_hbm, sem, dst_row, r):
    return pltpu.make_async_copy(
        ybuf.at[pl.ds(pl.multiple_of(r * SUBLANES, SUBLANES), SUBLANES), :],
        y_hbm.at[pl.ds(pl.multiple_of(dst_row, SUBLANES), SUBLANES), :],
        sem)


def _moe_body(be_ref, nu_ref, src_ref, srcn_ref, dst_ref, x_hbm, wgu_ref, bgu_ref, wd_ref, bd_ref,
              y_hbm, xbuf, ybuf, wgu_bf, wd_bf, gsem, ssem):
    i = pl.program_id(0)
    nb = pl.num_programs(0)
    nused = nu_ref[0]
    slot = i % 2
    rows = MOE_TM * LANE_TILES
    group = 32

    def gather_all(idx_ref, s):
        def body(g, c):
            for q in range(group):
                r = g * group + q
                _gather_copy(x_hbm, xbuf.at[s], gsem.at[s], idx_ref[0, r], r).start(priority=q % 2)
            return c
        lax.fori_loop(0, MOE_TM // group, body, 0)

    def scatter_all(idx_ref, s):
        def body(g, c):
            for q in range(group):
                r = g * group + q
                _scatter_copy(ybuf.at[s], y_hbm, ssem.at[s], idx_ref[0, r], r).start(priority=q % 2)
            return c
        lax.fori_loop(0, MOE_TM // group, body, 0)

    def wait_scatter(s):
        pltpu.make_async_copy(ybuf.at[s], y_hbm.at[pl.ds(0, rows), :], ssem.at[s]).wait()

    @pl.when(i == 0)
    def _():
        ybuf[...] = jnp.zeros_like(ybuf)
        tail = y_hbm.at[pl.ds(y_hbm.shape[0] - 2 * rows, 2 * rows), :]
        for s in range(2):
            cp = pltpu.make_async_copy(ybuf.at[s], tail.at[pl.ds(s * rows, rows), :], ssem.at[s])
            cp.start()
            cp.wait()

    @pl.when(i < nused)
    def _():
        @pl.when(i == 0)
        def _():
            gather_all(src_ref, 0)

        @pl.when(i + 1 < nused)
        def _():
            gather_all(srcn_ref, 1 - slot)

        changed = jnp.logical_or(i == 0, be_ref[i] != be_ref[jnp.maximum(i - 1, 0)])

        @pl.when(changed)
        def _():
            wgu_bf[...] = wgu_ref[...].astype(BF16)
            wd_bf[...] = wd_ref[...].astype(BF16)

        pltpu.make_async_copy(x_hbm.at[pl.ds(0, rows), :], xbuf.at[slot], gsem.at[slot]).wait()
        x = _load_token_tiles(xbuf.at[slot], MOE_TM).astype(BF16)
        gu = jnp.dot(x, wgu_bf[...], preferred_element_type=F32) + bgu_ref[...]
        gate = jnp.minimum(gu[:, :D_FF], SWIGLU_LIMIT)
        up = jnp.clip(gu[:, D_FF:], -SWIGLU_LIMIT, SWIGLU_LIMIT)
        hidden = (up + 1.0) * gate * jax.nn.sigmoid(SWIGLU_ALPHA * gate)
        y = jnp.dot(hidden.astype(BF16), wd_bf[...], preferred_element_type=F32) + bd_ref[...]

        @pl.when(i >= 2)
        def _():
            wait_scatter(slot)

        _store_token_tiles(ybuf.at[slot], y)
        scatter_all(dst_ref, slot)

    @pl.when(i == nb - 1)
    def _():
        @pl.when(nused >= 1)
        def _():
            wait_scatter((nused - 1) % 2)

        @pl.when(nused >= 2)
        def _():
            wait_scatter(nused % 2)


def _moe_experts(xt, block_exp, n_used, src_rows, dst_rows, w_gu, b_gu, w_down, b_down, layer, n_tok):
    nb = block_exp.shape[0]
    rows = MOE_TM * LANE_TILES
    n_out_rows = (TOP_K * n_tok + 2 * MOE_TM) * LANE_TILES
    smem_blk = lambda f: pl.BlockSpec((None, 1, MOE_TM), f, memory_space=pltpu.SMEM)
    grid_spec = pltpu.PrefetchScalarGridSpec(
        num_scalar_prefetch=2,
        grid=(nb,),
        in_specs=[smem_blk(lambda i, be, nu: (i, 0, 0)),
                  smem_blk(lambda i, be, nu: (jnp.minimum(i + 1, nb - 1), 0, 0)),
                  smem_blk(lambda i, be, nu: (i, 0, 0)),
                  pl.BlockSpec(memory_space=pl.ANY),
                  pl.BlockSpec((None, None, D_MODEL, 2 * D_FF), lambda i, be, nu: (layer, be[i], 0, 0)),
                  pl.BlockSpec((None, None, 1, 2 * D_FF), lambda i, be, nu: (layer, be[i], 0, 0)),
                  pl.BlockSpec((None, None, D_FF, D_MODEL), lambda i, be, nu: (layer, be[i], 0, 0)),
                  pl.BlockSpec((None, None, 1, D_MODEL), lambda i, be, nu: (layer, be[i], 0, 0))],
        out_specs=pl.BlockSpec(memory_space=pl.ANY),
        scratch_shapes=[pltpu.VMEM((2, rows, LANES), F32),
                        pltpu.VMEM((2, rows, LANES), F32),
                        pltpu.VMEM((D_MODEL, 2 * D_FF), BF16),
                        pltpu.VMEM((D_FF, D_MODEL), BF16),
                        pltpu.SemaphoreType.DMA((2,)),
                        pltpu.SemaphoreType.DMA((2,))],
    )
    return pl.pallas_call(
        _moe_body,
        grid_spec=grid_spec,
        out_shape=jax.ShapeDtypeStruct((n_out_rows, LANES), F32),
        compiler_params=_params(("arbitrary",)),
        name=f"moe_experts_l{layer}",
    )(block_exp, n_used, src_rows, src_rows, dst_rows, xt, w_gu, b_gu, w_down, b_down)


ROW_TABLE_CHUNK = 4096


def _row_table_body(lo_ref, hi_ref, dest_ref, tbl_ref, *, n_assign):
    i = pl.program_id(0)

    @pl.when(i == 0)
    def _():
        def fill_range(e, c):
            def fill(r, c2):
                tbl_ref[r] = n_assign + jnp.bitwise_and(r, 2 * MOE_TM - 1)
                return c2
            lax.fori_loop(lo_ref[e], hi_ref[e], fill, 0)
            return c
        lax.fori_loop(0, lo_ref.shape[0], fill_range, 0)

    batch = 16

    def body(g, c):
        base = g * batch
        rows = [dest_ref[0, base + q] for q in range(batch)]
        first = i * ROW_TABLE_CHUNK + base
        for q, r in enumerate(rows):
            tbl_ref[r] = first + q
        return c
    lax.fori_loop(0, ROW_TABLE_CHUNK // batch, body, 0)


def _row_table(dest, pad_lo, pad_hi, n_rows, name):
    n_assign = dest.shape[0]
    steps = n_assign // ROW_TABLE_CHUNK
    grid_spec = pltpu.PrefetchScalarGridSpec(
        num_scalar_prefetch=2,
        grid=(steps,),
        in_specs=[pl.BlockSpec((None, 1, ROW_TABLE_CHUNK), lambda i, lo, hi: (i, 0, 0),
                               memory_space=pltpu.SMEM)],
        out_specs=pl.BlockSpec(memory_space=pltpu.SMEM),
    )
    return pl.pallas_call(
        functools.partial(_row_table_body, n_assign=n_assign),
        grid_spec=grid_spec,
        out_shape=jax.ShapeDtypeStruct((n_rows,), I32),
        compiler_params=_params(("arbitrary",)),
        name=name,
    )(pad_lo, pad_hi, dest.reshape(steps, 1, ROW_TABLE_CHUNK))


def _route_tables(route_i, counts, n_tok, name):
    n_assign = n_tok * TOP_K
    nb = n_assign // MOE_TM + N_EXPERTS
    counts = counts.reshape(N_EXPERTS).astype(I32)
    padded = ((counts + MOE_TM - 1) // MOE_TM) * MOE_TM
    pad_end = jnp.cumsum(padded)
    pad_start = pad_end - padded
    eid = route_i[:TOP_K]
    rank = route_i[TOP_K:]
    dest = (pad_start[eid] + rank).reshape(n_assign)
    pad_lo = jnp.concatenate([pad_start + counts, pad_end[-1:]]).astype(I32)
    pad_hi = jnp.concatenate([pad_end, jnp.full((1,), nb * MOE_TM, I32)]).astype(I32)
    row_assign = _row_table(dest, pad_lo, pad_hi, nb * MOE_TM, name)
    src_rows = jnp.where(row_assign < n_assign, row_assign % n_tok, 0) * LANE_TILES
    dst_rows = row_assign * LANE_TILES
    block_start = jnp.arange(nb, dtype=I32) * MOE_TM
    block_exp = jnp.minimum(jnp.sum((pad_end[None, :] <= block_start[:, None]).astype(I32), axis=1),
                            N_EXPERTS - 1)
    n_used = (pad_end[-1:] // MOE_TM).astype(I32)
    return block_exp, n_used, src_rows.reshape(nb, 1, MOE_TM), dst_rows.reshape(nb, 1, MOE_TM)


def _combine_body(y0_ref, y1_ref, y2_ref, y3_ref, rf_ref, xt_ref, g_ref, b_ref, o_ref):
    tm = o_ref.shape[0]
    rf = rf_ref[...]
    parts = []
    for c in range(LANE_TILES):
        acc = DEEPNORM_ALPHA * xt_ref[pl.ds(c, tm, stride=LANE_TILES), :]
        for k, yk in enumerate((y0_ref, y1_ref, y2_ref, y3_ref)):
            acc = acc + rf[:, k:k + 1] * yk[pl.ds(c, tm, stride=LANE_TILES), :]
        parts.append(acc)
    z = jnp.concatenate(parts, axis=-1)
    o_ref[...] = _layer_norm(z, g_ref[...], b_ref[...])


def _combine(y_tk, route_f, xt, gain, bias, n_tok, name):
    tm = ROW_TM
    nblk = n_tok // tm
    yspec = lambda k: pl.BlockSpec((tm * LANE_TILES, LANES), lambda i: (k * nblk + i, 0))
    return pl.pallas_call(
        _combine_body,
        grid=(nblk,),
        in_specs=[yspec(0), yspec(1), yspec(2), yspec(3),
                  pl.BlockSpec((tm, LANES), lambda i: (i, 0)),
                  pl.BlockSpec((tm * LANE_TILES, LANES), lambda i: (i, 0)),
                  pl.BlockSpec((1, D_MODEL), lambda i: (0, 0)),
                  pl.BlockSpec((1, D_MODEL), lambda i: (0, 0))],
        out_specs=pl.BlockSpec((tm, D_MODEL), lambda i: (i, 0)),
        out_shape=jax.ShapeDtypeStruct((n_tok, D_MODEL), F32),
        compiler_params=_params(("parallel",)),
        name=name,
    )(y_tk, y_tk, y_tk, y_tk, route_f, xt, gain, bias)


def _gelu_tanh(x):
    return 0.5 * x * (1.0 + jnp.tanh(math.sqrt(2.0 / math.pi) * (x + 0.044715 * (x * x * x))))


def _lru_gate_body(rec_ref, prev_ref, cw_ref, cb_ref, wr_ref, br_ref, wi_ref, bi_ref, lam_ref,
                   a_ref, b_ref, ext_ref):
    s = pl.program_id(1)
    ts = rec_ref.shape[0]
    cur = rec_ref[...]
    ext_ref[0:SUBLANES, :] = jnp.where(s == 0, 0.0, prev_ref[...])
    ext_ref[SUBLANES:, :] = cur
    conv = cb_ref[...] + cw_ref[CONV_WIDTH - 1:CONV_WIDTH, :] * cur
    for j in range(CONV_WIDTH - 1):
        conv = conv + cw_ref[j:j + 1, :] * ext_ref[pl.ds(SUBLANES - (CONV_WIDTH - 1) + j, ts), :]
    xb = conv.astype(BF16)
    r_parts, i_parts = [], []
    for h in range(LRU_HEADS):
        xh = xb[:, h * LRU_BLOCK:(h + 1) * LRU_BLOCK]
        r_parts.append(jnp.dot(xh, wr_ref[h].astype(BF16), preferred_element_type=F32))
        i_parts.append(jnp.dot(xh, wi_ref[h].astype(BF16), preferred_element_type=F32))
    r = jax.nn.sigmoid(jnp.concatenate(r_parts, axis=-1) + br_ref[...])
    ig = jax.nn.sigmoid(jnp.concatenate(i_parts, axis=-1) + bi_ref[...])
    neg_lam = -lam_ref[...]
    softplus = jnp.maximum(neg_lam, 0.0) + jnp.log1p(jnp.exp(-jnp.abs(neg_lam)))
    log_a = -LRU_C * r * softplus
    a = jnp.exp(log_a)
    a_ref[...] = a
    one_minus_a2 = -jnp.tanh(log_a) * (a * a + 1.0)
    b_ref[...] = jnp.sqrt(one_minus_a2) * (ig * conv)


def _lru_scan_body(a_ref, b_ref, g_ref, y_ref):
    seq, ct = a_ref.shape
    row = lax.broadcasted_iota(I32, (SUBLANES, ct), 0)

    def step(gi, carry):
        off = pl.multiple_of(gi * SUBLANES, SUBLANES)
        a = a_ref[pl.ds(off, SUBLANES), :]
        b = b_ref[pl.ds(off, SUBLANES), :]
        for d in (1, 2, 4):
            keep = row >= d
            b = jnp.where(keep, a * pltpu.roll(b, d, 0) + b, b)
            a = jnp.where(keep, a * pltpu.roll(a, d, 0), a)
        h = a * carry + b
        y_ref[pl.ds(off, SUBLANES), :] = _gelu_tanh(g_ref[pl.ds(off, SUBLANES), :]) * h
        return jnp.broadcast_to(h[SUBLANES - 1:SUBLANES, :], (SUBLANES, ct))

    lax.fori_loop(0, seq // SUBLANES, step, jnp.zeros((SUBLANES, ct), F32))


def _rglru_mixer(x2, bsz, seq, idx, a_w_in, conv_w, conv_b, w_rgate, b_rgate, w_igate, b_igate, lam, layer):
    t = bsz * seq
    w = D_MODEL
    u = _matmul(x2, a_w_in, idx, 0, 2 * w, F32, f"lru_in_l{layer}")
    ts = 256
    nst = seq // ts
    vec = lambda: pl.BlockSpec((1, w), lambda b, s: (0, 0))
    a, bb = pl.pallas_call(
        _lru_gate_body,
        grid=(bsz, nst),
        in_specs=[pl.BlockSpec((ts, w), lambda b, s: (b * nst + s, 1)),
                  pl.BlockSpec((SUBLANES, w),
                               lambda b, s: (jnp.maximum((b * nst + s) * (ts // SUBLANES) - 1, 0), 1)),
                  pl.BlockSpec((CONV_WIDTH, w), lambda b, s: (0, 0)),
                  vec(),
                  pl.BlockSpec((LRU_HEADS, LRU_BLOCK, LRU_BLOCK), lambda b, s: (0, 0, 0)),
                  vec(),
                  pl.BlockSpec((LRU_HEADS, LRU_BLOCK, LRU_BLOCK), lambda b, s: (0, 0, 0)),
                  vec(), vec()],
        out_specs=[pl.BlockSpec((ts, w), lambda b, s: (b * nst + s, 0)),
                   pl.BlockSpec((ts, w), lambda b, s: (b * nst + s, 0))],
        out_shape=[jax.ShapeDtypeStruct((t, w), F32), jax.ShapeDtypeStruct((t, w), F32)],
        scratch_shapes=[pltpu.VMEM((ts + SUBLANES, w), F32)],
        compiler_params=_params(("parallel", "parallel")),
        name=f"lru_gates_l{layer}",
    )(u, u, conv_w[idx], conv_b[idx][None], w_rgate[idx], b_rgate[idx][None],
      w_igate[idx], b_igate[idx][None], lam[idx][None])
    ct = 256
    return pl.pallas_call(
        _lru_scan_body,
        grid=(bsz, w // ct),
        in_specs=[pl.BlockSpec((seq, ct), lambda b, j: (b, j)),
                  pl.BlockSpec((seq, ct), lambda b, j: (b, j)),
                  pl.BlockSpec((seq, ct), lambda b, j: (b, j))],
        out_specs=pl.BlockSpec((seq, ct), lambda b, j: (b, j)),
        out_shape=jax.ShapeDtypeStruct((t, w), F32),
        compiler_params=_params(("parallel", "parallel")),
        name=f"lru_scan_l{layer}",
    )(a, bb, u)


def _ret_body(lg_ref, q_ref, k_ref, v_ref, g_ref, y_ref, state_ref):
    c = pl.program_id(1)
    cs = RET_CHUNK
    heads = range(RET_HEADS)

    @pl.when(c == 0)
    def _():
        state_ref[...] = jnp.zeros_like(state_ref)

    qi = lax.broadcasted_iota(I32, (cs, cs), 0)
    ki = lax.broadcasted_iota(I32, (cs, cs), 1)
    rel = (qi - ki).astype(F32)
    pos_col = lax.broadcasted_iota(I32, (cs, 1), 0).astype(F32)
    lgs = [lg_ref[h] for h in heads]
    intra_decay = [jnp.where(rel >= 0, jnp.exp(lg * jnp.maximum(rel, 0.0)), 0.0) for lg in lgs]
    query_decay = [jnp.exp(lg * (pos_col + 1.0)) for lg in lgs]
    key_decay = [jnp.exp(lg * (cs - 1.0 - pos_col)) for lg in lgs]
    chunk_decay = [jnp.exp(jnp.full((1, 1), lg * cs, F32)) for lg in lgs]

    scale = RET_QK ** -0.5
    dn_nt = (((1,), (1,)), ((), ()))
    dn_tn = (((0,), (0,)), ((), ()))
    qs = [q_ref[:, h * RET_QK:(h + 1) * RET_QK] for h in heads]
    ks = [k_ref[:, h * RET_QK:(h + 1) * RET_QK] for h in heads]
    vs = [v_ref[:, h * RET_V:(h + 1) * RET_V] for h in heads]
    intra = [lax.dot_general(q, k, dn_nt, preferred_element_type=F32) * scale * d
             for q, k, d in zip(qs, ks, intra_decay)]
    states = [state_ref[h] for h in heads]
    cross = [jnp.dot(q, s.astype(BF16), preferred_element_type=F32) * d
             for q, s, d in zip(qs, states, query_decay)]
    outs = [jnp.dot(a.astype(BF16), v, preferred_element_type=F32) + x for a, v, x in zip(intra, vs, cross)]
    kds = [(k.astype(F32) * (scale * d)).astype(BF16) for k, d in zip(ks, key_decay)]
    upd = [lax.dot_general(kd, v, dn_tn, preferred_element_type=F32) for kd, v in zip(kds, vs)]
    for h in heads:
        state_ref[h] = chunk_decay[h] * states[h] + upd[h]
    for h in heads:
        out = outs[h]
        o = out * lax.rsqrt(jnp.mean(out * out, axis=-1, keepdims=True) + RET_EPS)
        g = g_ref[:, h * RET_V:(h + 1) * RET_V]
        y_ref[:, h * RET_V:(h + 1) * RET_V] = (g * jax.nn.sigmoid(g) * o).astype(y_ref.dtype)


def _retention_mixer(x2, bsz, seq, idx, b_w_in, layer):
    t = bsz * seq
    qk_w = RET_HEADS * RET_QK
    v_w = RET_HEADS * RET_V
    qkv = _matmul(x2, b_w_in, idx, 0, 2 * qk_w + v_w, BF16, f"ret_qkv_l{layer}")
    gate = _matmul(x2, b_w_in, idx, 2 * qk_w + v_w, v_w, F32, f"ret_gate_l{layer}")
    nc = seq // RET_CHUNK
    log_gamma = jnp.log1p(-jnp.exp2(-5.0 - jnp.arange(RET_HEADS, dtype=F32)))
    grid_spec = pltpu.PrefetchScalarGridSpec(
        num_scalar_prefetch=1,
        grid=(bsz, nc),
        in_specs=[pl.BlockSpec((RET_CHUNK, qk_w), lambda b, c, lg: (b * nc + c, 0)),
                  pl.BlockSpec((RET_CHUNK, qk_w), lambda b, c, lg: (b * nc + c, 1)),
                  pl.BlockSpec((RET_CHUNK, v_w), lambda b, c, lg: (b * nc + c, 1)),
                  pl.BlockSpec((RET_CHUNK, v_w), lambda b, c, lg: (b * nc + c, 0))],
        out_specs=pl.BlockSpec((RET_CHUNK, v_w), lambda b, c, lg: (b * nc + c, 0)),
        scratch_shapes=[pltpu.VMEM((RET_HEADS, RET_QK, RET_V), F32)],
    )
    return pl.pallas_call(
        _ret_body,
        grid_spec=grid_spec,
        out_shape=jax.ShapeDtypeStruct((t, v_w), BF16),
        compiler_params=_params(("parallel", "arbitrary")),
        name=f"retention_l{layer}",
    )(log_gamma, qkv, qkv, qkv, gate)


def _attn_body(*refs, dilation, has_prev):
    if has_prev:
        q_ref, kc_ref, kp_ref, vc_ref, vp_ref, o_ref, st_ref = refs
    else:
        q_ref, kc_ref, vc_ref, o_ref, st_ref = refs
    n = pl.program_id(2)
    nw = q_ref.shape[0]
    dh = ATTN_HEAD_DIM
    scale = dh ** -0.5
    qi = lax.broadcasted_iota(I32, (nw, nw), 0)
    ki = lax.broadcasted_iota(I32, (nw, nw), 1)
    steps_c = qi - ki
    valid_c = steps_c >= 0
    dist_c = (steps_c * dilation).astype(F32)
    q = q_ref[...]
    kc = kc_ref[...]
    vc = vc_ref[...]
    lane = lax.broadcasted_iota(I32, (nw, LANES), 1)
    dn = (((1,), (1,)), ((), ()))
    heads = range(ATTN_HEADS)
    sls = [slice(h * dh, (h + 1) * dh) for h in heads]
    slopes = [2.0 ** (-8.0 * (h + 1) / ATTN_HEADS) for h in heads]
    sc_c = [lax.dot_general(q[:, sl], kc[:, sl], dn, preferred_element_type=F32) for sl in sls]
    sc_c = [jnp.where(valid_c, s * scale - sp * dist_c, -jnp.inf) for s, sp in zip(sc_c, slopes)]
    ms = [jnp.max(a, axis=-1, keepdims=True) for a in sc_c]
    if has_prev:
        steps_p = steps_c + nw
        valid_p = steps_p <= jnp.where(n > 0, nw, -1)
        dist_p = (steps_p * dilation).astype(F32)
        kp = kp_ref[...]
        vp = vp_ref[...]
        sc_p = [lax.dot_general(q[:, sl], kp[:, sl], dn, preferred_element_type=F32) for sl in sls]
        sc_p = [jnp.where(valid_p, s * scale - sp * dist_p, -jnp.inf) for s, sp in zip(sc_p, slopes)]
        ms = [jnp.maximum(m, jnp.max(b, axis=-1, keepdims=True)) for m, b in zip(ms, sc_p)]
    pc = [jnp.exp(a - m) for a, m in zip(sc_c, ms)]
    sums = [jnp.sum(a, axis=-1, keepdims=True) for a in pc]
    outs = [jnp.dot(a.astype(BF16), vc[:, sl], preferred_element_type=F32) for a, sl in zip(pc, sls)]
    if has_prev:
        pp = [jnp.exp(b - m) for b, m in zip(sc_p, ms)]
        sums = [s + jnp.sum(b, axis=-1, keepdims=True) for s, b in zip(sums, pp)]
        outs = [o + jnp.dot(b.astype(BF16), vp[:, sl], preferred_element_type=F32)
                for o, b, sl in zip(outs, pp, sls)]
    stats = jnp.zeros((nw, LANES), F32)
    for h, sl, o, m, sm in zip(heads, sls, outs, ms, sums):
        o_ref[:, sl] = o
        stats = jnp.where(lane == h, m, stats)
        stats = jnp.where(lane == ATTN_HEADS + h, sm, stats)
    st_ref[...] = stats


def _merge_body(o0_ref, o1_ref, o2_ref, s0_ref, s1_ref, s2_ref, y_ref):
    tm = y_ref.shape[0]
    stats = [s0_ref[...], s1_ref[...], s2_ref[...]]
    lane = lax.broadcasted_iota(I32, (tm, LANES), 1)
    maxes = [s for s in stats]
    sums = [pltpu.roll(s, LANES - ATTN_HEADS, 1) for s in stats]
    m_all = jnp.maximum(jnp.maximum(maxes[0], maxes[1]), maxes[2])
    wgt = [jnp.exp(m - m_all) for m in maxes]
    den = wgt[0] * sums[0] + wgt[1] * sums[1] + wgt[2] * sums[2]
    er = lax.broadcasted_iota(I32, (LANES, D_MODEL), 0)
    ec = lax.broadcasted_iota(I32, (LANES, D_MODEL), 1)
    expand = (ec // ATTN_HEAD_DIM == er).astype(F32)
    acc = jnp.zeros((tm, D_MODEL), F32)
    for w, o_ref in zip(wgt, (o0_ref, o1_ref, o2_ref)):
        coef = jnp.where(lane < ATTN_HEADS, w / den, 0.0)
        acc = acc + jnp.dot(coef, expand, preferred_element_type=F32,
                            precision=lax.Precision.HIGHEST) * o_ref[...]
    y_ref[...] = acc


def _attention_mixer(x2, bsz, seq, idx, c_w_in, layer):
    t = bsz * seq
    w = D_MODEL
    n_groups = len(ATTN_PATTERNS)
    outs, stats = [], []
    for g, (window, dilation) in enumerate(ATTN_PATTERNS):
        nw = window // dilation
        sub = seq // dilation
        assert nw == 128 and sub % nw == 0
        nb = sub // nw

        def to_sub(a):
            return a.reshape(bsz, sub, dilation, -1).transpose(0, 2, 1, 3).reshape(t, -1)

        def from_sub(a):
            return a.reshape(bsz, dilation, sub, -1).transpose(0, 2, 1, 3).reshape(t, -1)

        xs = x2 if dilation == 1 else to_sub(x2)
        qkv = _matmul(xs, c_w_in, idx, g * w, 3 * w, BF16, f"attn_qkv_g{g}_l{layer}",
                      col_block_stride=n_groups)
        cur = lambda slab: pl.BlockSpec((nw, w), lambda b, r, n, slab=slab: ((b * dilation + r) * nb + n, slab))
        prev = lambda slab: pl.BlockSpec(
            (nw, w), lambda b, r, n, slab=slab: ((b * dilation + r) * nb + jnp.maximum(n - 1, 0), slab))
        has_prev = nb > 1
        in_specs = [cur(0), cur(1), prev(1), cur(2), prev(2)] if has_prev else [cur(0), cur(1), cur(2)]
        o, st = pl.pallas_call(
            functools.partial(_attn_body, dilation=dilation, has_prev=has_prev),
            grid=(bsz, dilation, nb),
            in_specs=in_specs,
            out_specs=[pl.BlockSpec((nw, w), lambda b, r, n: ((b * dilation + r) * nb + n, 0)),
                       pl.BlockSpec((nw, LANES), lambda b, r, n: ((b * dilation + r) * nb + n, 0))],
            out_shape=[jax.ShapeDtypeStruct((t, w), F32), jax.ShapeDtypeStruct((t, LANES), F32)],
            compiler_params=_params(("parallel", "parallel", "parallel")),
            name=f"attn_g{g}_l{layer}",
        )(*([qkv] * len(in_specs)))
        outs.append(o if dilation == 1 else from_sub(o))
        stats.append(st if dilation == 1 else from_sub(st))
    tm = ROW_TM
    return pl.pallas_call(
        _merge_body,
        grid=(t // tm,),
        in_specs=[pl.BlockSpec((tm, w), lambda i: (i, 0))] * 3 + [pl.BlockSpec((tm, LANES), lambda i: (i, 0))] * 3,
        out_specs=pl.BlockSpec((tm, w), lambda i: (i, 0)),
        out_shape=jax.ShapeDtypeStruct((t, w), F32),
        compiler_params=_params(("parallel",)),
        name=f"attn_merge_l{layer}",
    )(*outs, *stats)


def kernel(x, a_w_in, a_conv_w, a_conv_b, a_w_rgate, a_b_rgate, a_w_igate, a_b_igate, a_lambda, a_w_out,
           b_w_in, b_w_out, c_w_in, c_w_out, ln_gain, ln_bias, moe_w_router, moe_b_router, moe_w_gu,
           moe_b_gu, moe_w_down, moe_b_down):
    bsz, seq, d = x.shape
    t = bsz * seq
    x2 = x.reshape(t, d)
    b_gu4 = moe_b_gu[:, :, None, :]
    b_down4 = moe_b_down[:, :, None, :]
    for layer in range(DEPTH):
        kind, idx = layer % N_MIXERS, layer // N_MIXERS
        if kind == 0:
            y = _rglru_mixer(x2, bsz, seq, idx, a_w_in, a_conv_w, a_conv_b, a_w_rgate, a_b_rgate,
                             a_w_igate, a_b_igate, a_lambda, layer)
            w_out = a_w_out
        elif kind == 1:
            y = _retention_mixer(x2, bsz, seq, idx, b_w_in, layer)
            w_out = b_w_out
        else:
            y = _attention_mixer(x2, bsz, seq, idx, c_w_in, layer)
            w_out = c_w_out
        xt, route_f, route_i, counts = _post_mixer(
            y, w_out, idx, x2, ln_gain[layer, 0][None], ln_bias[layer, 0][None],
            moe_w_router[layer], moe_b_router[layer][None], f"post_mixer_l{layer}")
        block_exp, n_used, src_rows, dst_rows = _route_tables(route_i, counts, t, f"row_table_l{layer}")
        y_tk = _moe_experts(xt, block_exp, n_used, src_rows, dst_rows,
                            moe_w_gu, b_gu4, moe_w_down, b_down4, layer, t)
        x2 = _combine(y_tk, route_f, xt, ln_gain[layer, 1][None], ln_bias[layer, 1][None], t,
                      f"combine_l{layer}")
    return x2.reshape(bsz, seq, d)
```

```python
import functools
import math

import jax
import jax.numpy as jnp
from jax import lax
from jax.experimental import pallas as pl
from jax.experimental.pallas import tpu as pltpu

F32 = jnp.float32
BF16 = jnp.bfloat16
I32 = jnp.int32

D_MODEL = 1024
DEPTH = 4
N_MIXERS = 3
DEEPNORM_ALPHA = (2.0 * DEPTH) ** 0.25
LN_EPS = 1e-5
LRU_HEADS = 8
LRU_BLOCK = D_MODEL // LRU_HEADS
CONV_WIDTH = 4
LRU_C = 8.0
RET_HEADS = 4
RET_QK = D_MODEL // RET_HEADS
RET_V = 2 * RET_QK
RET_CHUNK = 128
RET_EPS = 1e-6
ATTN_PATTERNS = ((128, 1), (512, 4), (2048, 16))
ATTN_HEADS = 16
ATTN_HEAD_DIM = D_MODEL // ATTN_HEADS
N_EXPERTS = 32
TOP_K = 4
D_FF = D_MODEL
SWIGLU_LIMIT = 7.0
SWIGLU_ALPHA = 1.702

SUBLANES = 8
LANES = 128
LANE_TILES = D_MODEL // LANES
VMEM_LIMIT = 56 * 1024 * 1024

MOE_TM = 256
ROW_TM = 256


def _params(sem, vmem=VMEM_LIMIT):
    return pltpu.CompilerParams(dimension_semantics=sem, vmem_limit_bytes=vmem)


def _mm_body(x_ref, w_ref, o_ref, wbf_ref):
    @pl.when(pl.program_id(1) == 0)
    def _():
        wbf_ref[...] = w_ref[...].astype(BF16)

    o_ref[...] = jnp.dot(x_ref[...].astype(BF16), wbf_ref[...],
                         preferred_element_type=F32).astype(o_ref.dtype)


def _matmul(x, w_stack, idx, col0, ncols, out_dtype, name, tm=1024, tn=1024, col_block_stride=1):
    t, k = x.shape
    assert col0 % tn == 0 and ncols % tn == 0 and t % tm == 0
    return pl.pallas_call(
        _mm_body,
        grid=(ncols // tn, t // tm),
        in_specs=[pl.BlockSpec((tm, k), lambda n, m: (m, 0)),
                  pl.BlockSpec((None, k, tn), lambda n, m: (idx, 0, col0 // tn + n * col_block_stride))],
        out_specs=pl.BlockSpec((tm, tn), lambda n, m: (m, n)),
        out_shape=jax.ShapeDtypeStruct((t, ncols), out_dtype),
        scratch_shapes=[pltpu.VMEM((k, tn), BF16)],
        compiler_params=_params(("arbitrary", "arbitrary")),
        name=name,
    )(x, w_stack)


def _layer_norm(z, gain, bias):
    mu = jnp.mean(z, axis=-1, keepdims=True)
    zc = z - mu
    var = jnp.mean(zc * zc, axis=-1, keepdims=True)
    return zc * lax.rsqrt(var + LN_EPS) * gain + bias


def _store_token_tiles(ref, val):
    tm = val.shape[0]
    for c in range(LANE_TILES):
        ref[pl.ds(c, tm, stride=LANE_TILES), :] = val[:, c * LANES:(c + 1) * LANES]


def _load_token_tiles(ref, tm):
    return jnp.concatenate([ref[pl.ds(c, tm, stride=LANE_TILES), :] for c in range(LANE_TILES)], axis=-1)


def _post_body(y_ref, w_ref, x_ref, g_ref, b_ref, wr_ref, br_ref,
               xt_ref, rf_ref, ri_ref, cnt_ref, wbf_ref, wrh_ref, wrl_ref, run_ref):
    i = pl.program_id(0)
    tm = x_ref.shape[0]

    @pl.when(i == 0)
    def _():
        wbf_ref[...] = w_ref[...].astype(BF16)
        wr = wr_ref[...]
        wrh = wr.astype(BF16)
        wrh_ref[...] = wrh
        wrl_ref[...] = (wr - wrh.astype(F32)).astype(BF16)
        run_ref[...] = jnp.zeros_like(run_ref)

    z = DEEPNORM_ALPHA * x_ref[...] + jnp.dot(y_ref[...].astype(BF16), wbf_ref[...],
                                              preferred_element_type=F32)
    xn = _layer_norm(z, g_ref[...], b_ref[...])
    _store_token_tiles(xt_ref, xn)

    xh = xn.astype(BF16)
    xl = (xn - xh.astype(F32)).astype(BF16)
    logits_tm = (jnp.dot(xh, wrh_ref[...], preferred_element_type=F32)
                 + jnp.dot(xl, wrh_ref[...], preferred_element_type=F32)
                 + jnp.dot(xh, wrl_ref[...], preferred_element_type=F32) + br_ref[...])
    logits = jnp.transpose(logits_tm)[:N_EXPERTS, :]
    row_e = lax.broadcasted_iota(I32, (N_EXPERTS, tm), 0).astype(F32)
    work = logits
    top_l, top_e, onehots = [], [], []
    for _ in range(TOP_K):
        m = jnp.max(work, axis=0, keepdims=True)
        idx = jnp.min(jnp.where(work == m, row_e, float(N_EXPERTS)), axis=0, keepdims=True)
        oh = row_e == idx
        top_l.append(m)
        top_e.append(idx)
        onehots.append(oh)
        work = jnp.where(oh, -jnp.inf, work)
    exps = [jnp.exp(l - top_l[0]) for l in top_l]
    denom = exps[0] + exps[1] + exps[2] + exps[3]
    gates = [e / denom for e in exps]

    sel = jnp.zeros((N_EXPERTS, tm), F32)
    for oh in onehots:
        sel = sel + oh.astype(F32)
    ri = lax.broadcasted_iota(I32, (tm, tm), 0)
    ci = lax.broadcasted_iota(I32, (tm, tm), 1)
    earlier = (ri < ci).astype(BF16)
    before = jnp.dot(sel.astype(BF16), earlier, preferred_element_type=F32) + run_ref[...]
    ranks = [jnp.sum(jnp.where(oh, before, 0.0), axis=0, keepdims=True) for oh in onehots]
    run_ref[...] = run_ref[...] + jnp.sum(sel, axis=1, keepdims=True)
    cnt_ref[...] = run_ref[...]

    row8 = lax.broadcasted_iota(I32, (2 * TOP_K, tm), 0)
    gate_rows = jnp.zeros((2 * TOP_K, tm), F32)
    rint = jnp.zeros((2 * TOP_K, tm), I32)
    for k in range(TOP_K):
        gate_rows = jnp.where(row8 == k, gates[k], gate_rows)
        rint = jnp.where(row8 == k, top_e[k].astype(I32), rint)
        rint = jnp.where(row8 == TOP_K + k, ranks[k].astype(I32), rint)
    ri_ref[...] = rint
    pick = (lax.broadcasted_iota(I32, (2 * TOP_K, LANES), 0)
            == lax.broadcasted_iota(I32, (2 * TOP_K, LANES), 1)).astype(F32)
    rf_ref[...] = lax.dot_general(gate_rows, pick, (((0,), (0,)), ((), ())), preferred_element_type=F32,
                                  precision=lax.Precision.HIGHEST)


def _post_mixer(y, w_out_stack, idx, x, gain, bias, w_router, b_router, name):
    t, kd = y.shape
    tm = ROW_TM
    return pl.pallas_call(
        _post_body,
        grid=(t // tm,),
        in_specs=[pl.BlockSpec((tm, kd), lambda i: (i, 0)),
                  pl.BlockSpec((None, kd, D_MODEL), lambda i: (idx, 0, 0)),
                  pl.BlockSpec((tm, D_MODEL), lambda i: (i, 0)),
                  pl.BlockSpec((1, D_MODEL), lambda i: (0, 0)),
                  pl.BlockSpec((1, D_MODEL), lambda i: (0, 0)),
                  pl.BlockSpec((D_MODEL, LANES), lambda i: (0, 0)),
                  pl.BlockSpec((1, LANES), lambda i: (0, 0))],
        out_specs=[pl.BlockSpec((tm * LANE_TILES, LANES), lambda i: (i, 0)),
                   pl.BlockSpec((tm, LANES), lambda i: (i, 0)),
                   pl.BlockSpec((2 * TOP_K, tm), lambda i: (0, i)),
                   pl.BlockSpec((N_EXPERTS, 1), lambda i: (0, 0))],
        out_shape=[jax.ShapeDtypeStruct((t * LANE_TILES, LANES), F32),
                   jax.ShapeDtypeStruct((t, LANES), F32),
                   jax.ShapeDtypeStruct((2 * TOP_K, t), I32),
                   jax.ShapeDtypeStruct((N_EXPERTS, 1), F32)],
        scratch_shapes=[pltpu.VMEM((kd, D_MODEL), BF16), pltpu.VMEM((D_MODEL, LANES), BF16),
                        pltpu.VMEM((D_MODEL, LANES), BF16), pltpu.VMEM((N_EXPERTS, 1), F32)],
        compiler_params=_params(("arbitrary",)),
        name=name,
    )(y, w_out_stack, x, gain, bias,
      jnp.pad(w_router, ((0, 0), (0, LANES - N_EXPERTS))),
      jnp.pad(b_router.reshape(1, N_EXPERTS), ((0, 0), (0, LANES - N_EXPERTS))))


def _gather_copy(x_hbm, xbuf, sem, src_row, r):
    return pltpu.make_async_copy(
        x_hbm.at[pl.ds(pl.multiple_of(src_row, SUBLANES), SUBLANES), :],
        xbuf.at[pl.ds(pl.multiple_of(r * SUBLANES, SUBLANES), SUBLANES), :],
        sem)


def _scatter_copy(ybuf, y_hbm, sem, dst_row, r):
    return pltpu.make_async_copy(
        ybuf.at[pl.ds(pl.multiple_of(r * SUBLANES, SUBLANES), SUBLANES), :],
        y_hbm.at[pl.ds(pl.multiple_of(dst_row, SUBLANES), SUBLANES), :],
        sem)


def _moe_body(be_ref, nu_ref, src_ref, srcn_ref, dst_ref, x_hbm, wgu_ref, bgu_ref, wd_ref, bd_ref,
              y_hbm, xbuf, ybuf, wgu_bf, wd_bf, gsem, ssem):
    i = pl.program_id(0)
    nb = pl.num_programs(0)
    nused = nu_ref[0]
    slot = i % 2
    rows = MOE_TM * LANE_TILES
    group = 32

    def gather_all(idx_ref, s):
        def body(g, c):
            for q in range(group):
                r = g * group + q
                _gather_copy(x_hbm, xbuf.at[s], gsem.at[s], idx_ref[0, r], r).start(priority=q % 2)
            return c
        lax.fori_loop(0, MOE_TM // group, body, 0)

    def scatter_all(idx_ref, s):
        def body(g, c):
            for q in range(group):
                r = g * group + q
                _scatter_copy(ybuf.at[s], y_hbm, ssem.at[s], idx_ref[0, r], r).start(priority=q % 2)
            return c
        lax.fori_loop(0, MOE_TM // group, body, 0)

    def wait_scatter(s):
        pltpu.make_async_copy(ybuf.at[s], y_hbm.at[pl.ds(0, rows), :], ssem.at[s]).wait()

    @pl.when(i == 0)
    def _():
        ybuf[...] = jnp.zeros_like(ybuf)
        tail = y_hbm.at[pl.ds(y_hbm.shape[0] - 2 * rows, 2 * rows), :]
        for s in range(2):
            cp = pltpu.make_async_copy(ybuf.at[s], tail.at[pl.ds(s * rows, rows), :], ssem.at[s])
            cp.start()
            cp.wait()

    @pl.when(i < nused)
    def _():
        @pl.when(i == 0)
        def _():
            gather_all(src_ref, 0)

        @pl.when(i + 1 < nused)
        def _():
            gather_all(srcn_ref, 1 - slot)

        changed = jnp.logical_or(i == 0, be_ref[i] != be_ref[jnp.maximum(i - 1, 0)])

        @pl.when(changed)
        def _():
            wgu_bf[...] = wgu_ref[...].astype(BF16)
            wd_bf[...] = wd_ref[...].astype(BF16)

        pltpu.make_async_copy(x_hbm.at[pl.ds(0, rows), :], xbuf.at[slot], gsem.at[slot]).wait()
        x = _load_token_tiles(xbuf.at[slot], MOE_TM).astype(BF16)
        gu = jnp.dot(x, wgu_bf[...], preferred_element_type=F32) + bgu_ref[...]
        gate = jnp.minimum(gu[:, :D_FF], SWIGLU_LIMIT)
        up = jnp.clip(gu[:, D_FF:], -SWIGLU_LIMIT, SWIGLU_LIMIT)
        hidden = (up + 1.0) * gate * jax.nn.sigmoid(SWIGLU_ALPHA * gate)
        y = jnp.dot(hidden.astype(BF16), wd_bf[...], preferred_element_type=F32) + bd_ref[...]

        @pl.when(i >= 2)
        def _():
            wait_scatter(slot)

        _store_token_tiles(ybuf.at[slot], y)
        scatter_all(dst_ref, slot)

    @pl.when(i == nb - 1)
    def _():
        @pl.when(nused >= 1)
        def _():
            wait_scatter((nused - 1) % 2)

        @pl.when(nused >= 2)
        def _():
            wait_scatter(nused % 2)


def _moe_experts(xt, block_exp, n_used, src_rows, dst_rows, w_gu, b_gu, w_down, b_down, layer, n_tok):
    nb = block_exp.shape[0]
    rows = MOE_TM * LANE_TILES
    n_out_rows = (TOP_K * n_tok + 2 * MOE_TM) * LANE_TILES
    smem_blk = lambda f: pl.BlockSpec((None, 1, MOE_TM), f, memory_space=pltpu.SMEM)
    grid_spec = pltpu.PrefetchScalarGridSpec(
        num_scalar_prefetch=2,
        grid=(nb,),
        in_specs=[smem_blk(lambda i, be, nu: (i, 0, 0)),
                  smem_blk(lambda i, be, nu: (jnp.minimum(i + 1, nb - 1), 0, 0)),
                  smem_blk(lambda i, be, nu: (i, 0, 0)),
                  pl.BlockSpec(memory_space=pl.ANY),
                  pl.BlockSpec((None, None, D_MODEL, 2 * D_FF), lambda i, be, nu: (layer, be[i], 0, 0)),
                  pl.BlockSpec((None, None, 1, 2 * D_FF), lambda i, be, nu: (layer, be[i], 0, 0)),
                  pl.BlockSpec((None, None, D_FF, D_MODEL), lambda i, be, nu: (layer, be[i], 0, 0)),
                  pl.BlockSpec((None, None, 1, D_MODEL), lambda i, be, nu: (layer, be[i], 0, 0))],
        out_specs=pl.BlockSpec(memory_space=pl.ANY),
        scratch_shapes=[pltpu.VMEM((2, rows, LANES), F32),
                        pltpu.VMEM((2, rows, LANES), F32),
                        pltpu.VMEM((D_MODEL, 2 * D_FF), BF16),
                        pltpu.VMEM((D_FF, D_MODEL), BF16),
                        pltpu.SemaphoreType.DMA((2,)),
                        pltpu.SemaphoreType.DMA((2,))],
    )
    return pl.pallas_call(
        _moe_body,
        grid_spec=grid_spec,
        out_shape=jax.ShapeDtypeStruct((n_out_rows, LANES), F32),
        compiler_params=_params(("arbitrary",)),
        name=f"moe_experts_l{layer}",
    )(block_exp, n_used, src_rows, src_rows, dst_rows, xt, w_gu, b_gu, w_down, b_down)


ROW_TABLE_CHUNK = 4096


def _row_table_body(lo_ref, hi_ref, dest_ref, tbl_ref, *, n_assign):
    i = pl.program_id(0)

    @pl.when(i == 0)
    def _():
        def fill_range(e, c):
            def fill(r, c2):
                tbl_ref[r] = n_assign + jnp.bitwise_and(r, 2 * MOE_TM - 1)
                return c2
            lax.fori_loop(lo_ref[e], hi_ref[e], fill, 0)
            return c
        lax.fori_loop(0, lo_ref.shape[0], fill_range, 0)

    batch = 16

    def body(g, c):
        base = g * batch
        rows = [dest_ref[0, base + q] for q in range(batch)]
        first = i * ROW_TABLE_CHUNK + base
        for q, r in enumerate(rows):
            tbl_ref[r] = first + q
        return c
    lax.fori_loop(0, ROW_TABLE_CHUNK // batch, body, 0)


def _row_table(dest, pad_lo, pad_hi, n_rows, name):
    n_assign = dest.shape[0]
    steps = n_assign // ROW_TABLE_CHUNK
    grid_spec = pltpu.PrefetchScalarGridSpec(
        num_scalar_prefetch=2,
        grid=(steps,),
        in_specs=[pl.BlockSpec((None, 1, ROW_TABLE_CHUNK), lambda i, lo, hi: (i, 0, 0),
                               memory_space=pltpu.SMEM)],
        out_specs=pl.BlockSpec(memory_space=pltpu.SMEM),
    )
    return pl.pallas_call(
        functools.partial(_row_table_body, n_assign=n_assign),
        grid_spec=grid_spec,
        out_shape=jax.ShapeDtypeStruct((n_rows,), I32),
        compiler_params=_params(("arbitrary",)),
        name=name,
    )(pad_lo, pad_hi, dest.reshape(steps, 1, ROW_TABLE_CHUNK))


def _route_tables(route_i, counts, n_tok, name):
    n_assign = n_tok * TOP_K
    nb = n_assign // MOE_TM + N_EXPERTS
    counts = counts.reshape(N_EXPERTS).astype(I32)
    padded = ((counts + MOE_TM - 1) // MOE_TM) * MOE_TM
    pad_end = jnp.cumsum(padded)
    pad_start = pad_end - padded
    eid = route_i[:TOP_K]
    rank = route_i[TOP_K:]
    expert_ids = jnp.arange(N_EXPERTS, dtype=I32)
    start_of = jnp.sum(jnp.where(eid[..., None] == expert_ids, pad_start, 0), axis=-1)
    dest = (start_of + rank).reshape(n_assign)
    pad_lo = jnp.concatenate([pad_start + counts, pad_end[-1:]]).astype(I32)
    pad_hi = jnp.concatenate([pad_end, jnp.full((1,), nb * MOE_TM, I32)]).astype(I32)
    row_assign = _row_table(dest, pad_lo, pad_hi, nb * MOE_TM, name)
    src_rows = jnp.where(row_assign < n_assign, row_assign % n_tok, 0) * LANE_TILES
    dst_rows = row_assign * LANE_TILES
    block_start = jnp.arange(nb, dtype=I32) * MOE_TM
    block_exp = jnp.minimum(jnp.sum((pad_end[None, :] <= block_start[:, None]).astype(I32), axis=1),
                            N_EXPERTS - 1)
    n_used = (pad_end[-1:] // MOE_TM).astype(I32)
    return block_exp, n_used, src_rows.reshape(nb, 1, MOE_TM), dst_rows.reshape(nb, 1, MOE_TM)


def _combine_body(y0_ref, y1_ref, y2_ref, y3_ref, rf_ref, xt_ref, g_ref, b_ref, o_ref):
    tm = o_ref.shape[0]
    rf = rf_ref[...]
    parts = []
    for c in range(LANE_TILES):
        acc = DEEPNORM_ALPHA * xt_ref[pl.ds(c, tm, stride=LANE_TILES), :]
        for k, yk in enumerate((y0_ref, y1_ref, y2_ref, y3_ref)):
            acc = acc + rf[:, k:k + 1] * yk[pl.ds(c, tm, stride=LANE_TILES), :]
        parts.append(acc)
    z = jnp.concatenate(parts, axis=-1)
    o_ref[...] = _layer_norm(z, g_ref[...], b_ref[...])


def _combine(y_tk, route_f, xt, gain, bias, n_tok, name):
    tm = ROW_TM
    nblk = n_tok // tm
    yspec = lambda k: pl.BlockSpec((tm * LANE_TILES, LANES), lambda i: (k * nblk + i, 0))
    return pl.pallas_call(
        _combine_body,
        grid=(nblk,),
        in_specs=[yspec(0), yspec(1), yspec(2), yspec(3),
                  pl.BlockSpec((tm, LANES), lambda i: (i, 0)),
                  pl.BlockSpec((tm * LANE_TILES, LANES), lambda i: (i, 0)),
                  pl.BlockSpec((1, D_MODEL), lambda i: (0, 0)),
                  pl.BlockSpec((1, D_MODEL), lambda i: (0, 0))],
        out_specs=pl.BlockSpec((tm, D_MODEL), lambda i: (i, 0)),
        out_shape=jax.ShapeDtypeStruct((n_tok, D_MODEL), F32),
        compiler_params=_params(("parallel",)),
        name=name,
    )(y_tk, y_tk, y_tk, y_tk, route_f, xt, gain, bias)


def _gelu_tanh(x):
    return 0.5 * x * (1.0 + jnp.tanh(math.sqrt(2.0 / math.pi) * (x + 0.044715 * (x * x * x))))


LRU_ROWS = 256


def _lru_body(rec_ref, g_ref, cw_ref, cb_ref, wr_ref, br_ref, wi_ref, bi_ref, lam_ref, y_ref,
              ext_ref, a_ref, b_ref):
    seq, ct = rec_ref.shape
    heads = ct // LRU_BLOCK
    ext_ref[0:SUBLANES, :] = jnp.zeros((SUBLANES, ct), F32)
    ext_ref[SUBLANES:, :] = rec_ref[...]
    neg_lam = -lam_ref[...]
    softplus = jnp.maximum(neg_lam, 0.0) + jnp.log1p(jnp.exp(-jnp.abs(neg_lam)))
    wr = [wr_ref[h].astype(BF16) for h in range(heads)]
    wi = [wi_ref[h].astype(BF16) for h in range(heads)]

    for c in range(seq // LRU_ROWS):
        r0 = c * LRU_ROWS
        conv = cb_ref[...] + cw_ref[CONV_WIDTH - 1:CONV_WIDTH, :] * rec_ref[pl.ds(r0, LRU_ROWS), :]
        for j in range(CONV_WIDTH - 1):
            conv = conv + cw_ref[j:j + 1, :] * ext_ref[pl.ds(r0 + SUBLANES - (CONV_WIDTH - 1) + j, LRU_ROWS), :]
        xb = conv.astype(BF16)
        r_parts = [jnp.dot(xb[:, h * LRU_BLOCK:(h + 1) * LRU_BLOCK], wr[h], preferred_element_type=F32)
                   for h in range(heads)]
        i_parts = [jnp.dot(xb[:, h * LRU_BLOCK:(h + 1) * LRU_BLOCK], wi[h], preferred_element_type=F32)
                   for h in range(heads)]
        r = jax.nn.sigmoid(jnp.concatenate(r_parts, axis=-1) + br_ref[...])
        ig = jax.nn.sigmoid(jnp.concatenate(i_parts, axis=-1) + bi_ref[...])
        log_a = -LRU_C * r * softplus
        a = jnp.exp(log_a)
        a_ref[pl.ds(r0, LRU_ROWS), :] = a
        b_ref[pl.ds(r0, LRU_ROWS), :] = jnp.sqrt(-jnp.tanh(log_a) * (a * a + 1.0)) * (ig * conv)

    _lru_scan_body(a_ref, b_ref, g_ref, y_ref)


def _lru_scan_body(a_ref, b_ref, g_ref, y_ref):
    seq, ct = a_ref.shape
    row = lax.broadcasted_iota(I32, (SUBLANES, ct), 0)

    def step(gi, carry):
        off = pl.multiple_of(gi * SUBLANES, SUBLANES)
        a = a_ref[pl.ds(off, SUBLANES), :]
        b = b_ref[pl.ds(off, SUBLANES), :]
        for d in (1, 2, 4):
            keep = row >= d
            b = jnp.where(keep, a * pltpu.roll(b, d, 0) + b, b)
            a = jnp.where(keep, a * pltpu.roll(a, d, 0), a)
        h = a * carry + b
        y_ref[pl.ds(off, SUBLANES), :] = _gelu_tanh(g_ref[pl.ds(off, SUBLANES), :]) * h
        return jnp.broadcast_to(h[SUBLANES - 1:SUBLANES, :], (SUBLANES, ct))

    lax.fori_loop(0, seq // SUBLANES, step, jnp.zeros((SUBLANES, ct), F32), unroll=4)


def _rglru_mixer(x2, bsz, seq, idx, a_w_in, conv_w, conv_b, w_rgate, b_rgate, w_igate, b_igate, lam, layer):
    t = bsz * seq
    w = D_MODEL
    u = _matmul(x2, a_w_in, idx, 0, 2 * w, F32, f"lru_in_l{layer}")
    ct = 2 * LRU_BLOCK
    nct = w // ct
    hpt = ct // LRU_BLOCK
    vec = lambda: pl.BlockSpec((1, ct), lambda b, j: (0, j))
    return pl.pallas_call(
        _lru_body,
        grid=(bsz, nct),
        in_specs=[pl.BlockSpec((seq, ct), lambda b, j: (b, nct + j)),
                  pl.BlockSpec((seq, ct), lambda b, j: (b, j)),
                  pl.BlockSpec((CONV_WIDTH, ct), lambda b, j: (0, j)),
                  vec(),
                  pl.BlockSpec((hpt, LRU_BLOCK, LRU_BLOCK), lambda b, j: (j, 0, 0)),
                  vec(),
                  pl.BlockSpec((hpt, LRU_BLOCK, LRU_BLOCK), lambda b, j: (j, 0, 0)),
                  vec(), vec()],
        out_specs=pl.BlockSpec((seq, ct), lambda b, j: (b, j)),
        out_shape=jax.ShapeDtypeStruct((t, w), F32),
        scratch_shapes=[pltpu.VMEM((seq + SUBLANES, ct), F32),
                        pltpu.VMEM((seq, ct), F32),
                        pltpu.VMEM((seq, ct), F32)],
        compiler_params=_params(("parallel", "parallel")),
        name=f"lru_l{layer}",
    )(u, u, conv_w[idx], conv_b[idx][None], w_rgate[idx], b_rgate[idx][None],
      w_igate[idx], b_igate[idx][None], lam[idx][None])


def _ret_body(lg_ref, q_ref, k_ref, v_ref, g_ref, y_ref, state_ref):
    c = pl.program_id(1)
    cs = RET_CHUNK
    heads = range(RET_HEADS)

    @pl.when(c == 0)
    def _():
        state_ref[...] = jnp.zeros_like(state_ref)

    qi = lax.broadcasted_iota(I32, (cs, cs), 0)
    ki = lax.broadcasted_iota(I32, (cs, cs), 1)
    rel = (qi - ki).astype(F32)
    pos_col = lax.broadcasted_iota(I32, (cs, 1), 0).astype(F32)
    lgs = [lg_ref[h] for h in heads]
    intra_decay = [jnp.where(rel >= 0, jnp.exp(lg * jnp.maximum(rel, 0.0)), 0.0) for lg in lgs]
    query_decay = [jnp.exp(lg * (pos_col + 1.0)) for lg in lgs]
    key_decay = [jnp.exp(lg * (cs - 1.0 - pos_col)) for lg in lgs]
    chunk_decay = [jnp.exp(jnp.full((1, 1), lg * cs, F32)) for lg in lgs]

    scale = RET_QK ** -0.5
    dn_nt = (((1,), (1,)), ((), ()))
    dn_tn = (((0,), (0,)), ((), ()))
    qs = [q_ref[:, h * RET_QK:(h + 1) * RET_QK] for h in heads]
    ks = [k_ref[:, h * RET_QK:(h + 1) * RET_QK] for h in heads]
    vs = [v_ref[:, h * RET_V:(h + 1) * RET_V] for h in heads]
    intra = [lax.dot_general(q, k, dn_nt, preferred_element_type=F32) * scale * d
             for q, k, d in zip(qs, ks, intra_decay)]
    states = [state_ref[h] for h in heads]
    cross = [jnp.dot(q, s.astype(BF16), preferred_element_type=F32) * d
             for q, s, d in zip(qs, states, query_decay)]
    outs = [jnp.dot(a.astype(BF16), v, preferred_element_type=F32) + x for a, v, x in zip(intra, vs, cross)]
    kds = [(k.astype(F32) * (scale * d)).astype(BF16) for k, d in zip(ks, key_decay)]
    upd = [lax.dot_general(kd, v, dn_tn, preferred_element_type=F32) for kd, v in zip(kds, vs)]
    for h in heads:
        state_ref[h] = chunk_decay[h] * states[h] + upd[h]
    for h in heads:
        out = outs[h]
        o = out * lax.rsqrt(jnp.mean(out * out, axis=-1, keepdims=True) + RET_EPS)
        g = g_ref[:, h * RET_V:(h + 1) * RET_V]
        y_ref[:, h * RET_V:(h + 1) * RET_V] = (g * jax.nn.sigmoid(g) * o).astype(y_ref.dtype)


def _retention_mixer(x2, bsz, seq, idx, b_w_in, layer):
    t = bsz * seq
    qk_w = RET_HEADS * RET_QK
    v_w = RET_HEADS * RET_V
    qkv = _matmul(x2, b_w_in, idx, 0, 2 * qk_w + v_w, BF16, f"ret_qkv_l{layer}")
    gate = _matmul(x2, b_w_in, idx, 2 * qk_w + v_w, v_w, F32, f"ret_gate_l{layer}")
    nc = seq // RET_CHUNK
    log_gamma = jnp.log1p(-jnp.exp2(-5.0 - jnp.arange(RET_HEADS, dtype=F32)))
    grid_spec = pltpu.PrefetchScalarGridSpec(
        num_scalar_prefetch=1,
        grid=(bsz, nc),
        in_specs=[pl.BlockSpec((RET_CHUNK, qk_w), lambda b, c, lg: (b * nc + c, 0)),
                  pl.BlockSpec((RET_CHUNK, qk_w), lambda b, c, lg: (b * nc + c, 1)),
                  pl.BlockSpec((RET_CHUNK, v_w), lambda b, c, lg: (b * nc + c, 1)),
                  pl.BlockSpec((RET_CHUNK, v_w), lambda b, c, lg: (b * nc + c, 0))],
        out_specs=pl.BlockSpec((RET_CHUNK, v_w), lambda b, c, lg: (b * nc + c, 0)),
        scratch_shapes=[pltpu.VMEM((RET_HEADS, RET_QK, RET_V), F32)],
    )
    return pl.pallas_call(
        _ret_body,
        grid_spec=grid_spec,
        out_shape=jax.ShapeDtypeStruct((t, v_w), BF16),
        compiler_params=_params(("parallel", "arbitrary")),
        name=f"retention_l{layer}",
    )(log_gamma, qkv, qkv, qkv, gate)


def _attn_body(*refs, dilation, has_prev):
    if has_prev:
        q_ref, kc_ref, kp_ref, vc_ref, vp_ref, o_ref, st_ref = refs
    else:
        q_ref, kc_ref, vc_ref, o_ref, st_ref = refs
    n = pl.program_id(2)
    nw = q_ref.shape[0]
    dh = ATTN_HEAD_DIM
    scale = dh ** -0.5
    qi = lax.broadcasted_iota(I32, (nw, nw), 0)
    ki = lax.broadcasted_iota(I32, (nw, nw), 1)
    steps_c = qi - ki
    valid_c = steps_c >= 0
    dist_c = (steps_c * dilation).astype(F32)
    q = q_ref[...]
    kc = kc_ref[...]
    vc = vc_ref[...]
    lane = lax.broadcasted_iota(I32, (nw, LANES), 1)
    dn = (((1,), (1,)), ((), ()))
    heads = range(ATTN_HEADS)
    sls = [slice(h * dh, (h + 1) * dh) for h in heads]
    slopes = [2.0 ** (-8.0 * (h + 1) / ATTN_HEADS) for h in heads]
    sc_c = [lax.dot_general(q[:, sl], kc[:, sl], dn, preferred_element_type=F32) for sl in sls]
    sc_c = [jnp.where(valid_c, s * scale - sp * dist_c, -jnp.inf) for s, sp in zip(sc_c, slopes)]
    ms = [jnp.max(a, axis=-1, keepdims=True) for a in sc_c]
    if has_prev:
        steps_p = steps_c + nw
        valid_p = steps_p <= jnp.where(n > 0, nw, -1)
        dist_p = (steps_p * dilation).astype(F32)
        kp = kp_ref[...]
        vp = vp_ref[...]
        sc_p = [lax.dot_general(q[:, sl], kp[:, sl], dn, preferred_element_type=F32) for sl in sls]
        sc_p = [jnp.where(valid_p, s * scale - sp * dist_p, -jnp.inf) for s, sp in zip(sc_p, slopes)]
        ms = [jnp.maximum(m, jnp.max(b, axis=-1, keepdims=True)) for m, b in zip(ms, sc_p)]
    pc = [jnp.exp(a - m) for a, m in zip(sc_c, ms)]
    sums = [jnp.sum(a, axis=-1, keepdims=True) for a in pc]
    outs = [jnp.dot(a.astype(BF16), vc[:, sl], preferred_element_type=F32) for a, sl in zip(pc, sls)]
    if has_prev:
        pp = [jnp.exp(b - m) for b, m in zip(sc_p, ms)]
        sums = [s + jnp.sum(b, axis=-1, keepdims=True) for s, b in zip(sums, pp)]
        outs = [o + jnp.dot(b.astype(BF16), vp[:, sl], preferred_element_type=F32)
                for o, b, sl in zip(outs, pp, sls)]
    stats = jnp.zeros((nw, LANES), F32)
    for h, sl, o, m, sm in zip(heads, sls, outs, ms, sums):
        o_ref[:, sl] = o
        stats = jnp.where(lane == h, m, stats)
        stats = jnp.where(lane == ATTN_HEADS + h, sm, stats)
    st_ref[...] = stats


def _merge_body(o0_ref, o1_ref, o2_ref, s0_ref, s1_ref, s2_ref, y_ref):
    tm = y_ref.shape[0]
    stats = [s0_ref[...], s1_ref[...], s2_ref[...]]
    lane = lax.broadcasted_iota(I32, (tm, LANES), 1)
    maxes = [s for s in stats]
    sums = [pltpu.roll(s, LANES - ATTN_HEADS, 1) for s in stats]
    m_all = jnp.maximum(jnp.maximum(maxes[0], maxes[1]), maxes[2])
    wgt = [jnp.exp(m - m_all) for m in maxes]
    den = wgt[0] * sums[0] + wgt[1] * sums[1] + wgt[2] * sums[2]
    er = lax.broadcasted_iota(I32, (LANES, D_MODEL), 0)
    ec = lax.broadcasted_iota(I32, (LANES, D_MODEL), 1)
    expand = (ec // ATTN_HEAD_DIM == er).astype(F32)
    acc = jnp.zeros((tm, D_MODEL), F32)
    for w, o_ref in zip(wgt, (o0_ref, o1_ref, o2_ref)):
        coef = jnp.where(lane < ATTN_HEADS, w / den, 0.0)
        acc = acc + jnp.dot(coef, expand, preferred_element_type=F32,
                            precision=lax.Precision.HIGHEST) * o_ref[...]
    y_ref[...] = acc


def _attention_mixer(x2, bsz, seq, idx, c_w_in, layer):
    t = bsz * seq
    w = D_MODEL
    n_groups = len(ATTN_PATTERNS)
    outs, stats = [], []
    for g, (window, dilation) in enumerate(ATTN_PATTERNS):
        nw = window // dilation
        sub = seq // dilation
        assert nw == 128 and sub % nw == 0
        nb = sub // nw

        def to_sub(a):
            return a.reshape(bsz, sub, dilation, -1).transpose(0, 2, 1, 3).reshape(t, -1)

        def from_sub(a):
            return a.reshape(bsz, dilation, sub, -1).transpose(0, 2, 1, 3).reshape(t, -1)

        xs = x2 if dilation == 1 else to_sub(x2)
        qkv = _matmul(xs, c_w_in, idx, g * w, 3 * w, BF16, f"attn_qkv_g{g}_l{layer}",
                      col_block_stride=n_groups)
        cur = lambda slab: pl.BlockSpec((nw, w), lambda b, r, n, slab=slab: ((b * dilation + r) * nb + n, slab))
        prev = lambda slab: pl.BlockSpec(
            (nw, w), lambda b, r, n, slab=slab: ((b * dilation + r) * nb + jnp.maximum(n - 1, 0), slab))
        has_prev = nb > 1
        in_specs = [cur(0), cur(1), prev(1), cur(2), prev(2)] if has_prev else [cur(0), cur(1), cur(2)]
        o, st = pl.pallas_call(
            functools.partial(_attn_body, dilation=dilation, has_prev=has_prev),
            grid=(bsz, dilation, nb),
            in_specs=in_specs,
            out_specs=[pl.BlockSpec((nw, w), lambda b, r, n: ((b * dilation + r) * nb + n, 0)),
                       pl.BlockSpec((nw, LANES), lambda b, r, n: ((b * dilation + r) * nb + n, 0))],
            out_shape=[jax.ShapeDtypeStruct((t, w), F32), jax.ShapeDtypeStruct((t, LANES), F32)],
            compiler_params=_params(("parallel", "parallel", "parallel")),
            name=f"attn_g{g}_l{layer}",
        )(*([qkv] * len(in_specs)))
        outs.append(o if dilation == 1 else from_sub(o))
        stats.append(st if dilation == 1 else from_sub(st))
    tm = ROW_TM
    return pl.pallas_call(
        _merge_body,
        grid=(t // tm,),
        in_specs=[pl.BlockSpec((tm, w), lambda i: (i, 0))] * 3 + [pl.BlockSpec((tm, LANES), lambda i: (i, 0))] * 3,
        out_specs=pl.BlockSpec((tm, w), lambda i: (i, 0)),
        out_shape=jax.ShapeDtypeStruct((t, w), F32),
        compiler_params=_params(("parallel",)),
        name=f"attn_merge_l{layer}",
    )(*outs, *stats)


def kernel(x, a_w_in, a_conv_w, a_conv_b, a_w_rgate, a_b_rgate, a_w_igate, a_b_igate, a_lambda, a_w_out,
           b_w_in, b_w_out, c_w_in, c_w_out, ln_gain, ln_bias, moe_w_router, moe_b_router, moe_w_gu,
           moe_b_gu, moe_w_down, moe_b_down):
    bsz, seq, d = x.shape
    t = bsz * seq
    x2 = x.reshape(t, d)
    b_gu4 = moe_b_gu[:, :, None, :]
    b_down4 = moe_b_down[:, :, None, :]
    for layer in range(DEPTH):
        kind, idx = layer % N_MIXERS, layer // N_MIXERS
        if kind == 0:
            y = _rglru_mixer(x2, bsz, seq, idx, a_w_in, a_conv_w, a_conv_b, a_w_rgate, a_b_rgate,
                             a_w_igate, a_b_igate, a_lambda, layer)
            w_out = a_w_out
        elif kind == 1:
            y = _retention_mixer(x2, bsz, seq, idx, b_w_in, layer)
            w_out = b_w_out
        else:
            y = _attention_mixer(x2, bsz, seq, idx, c_w_in, layer)
            w_out = c_w_out
        xt, route_f, route_i, counts = _post_mixer(
            y, w_out, idx, x2, ln_gain[layer, 0][None], ln_bias[layer, 0][None],
            moe_w_router[layer], moe_b_router[layer][None], f"post_mixer_l{layer}")
        block_exp, n_used, src_rows, dst_rows = _route_tables(route_i, counts, t, f"row_table_l{layer}")
        y_tk = _moe_experts(xt, block_exp, n_used, src_rows, dst_rows,
                            moe_w_gu, b_gu4, moe_w_down, b_down4, layer, t)
        x2 = _combine(y_tk, route_f, xt, ln_gain[layer, 1][None], ln_bias[layer, 1][None], t,
                      f"combine_l{layer}")
    return x2.reshape(bsz, seq, d)
```

```python
import functools
import math

import jax
import jax.numpy as jnp
from jax import lax
from jax.experimental import pallas as pl
from jax.experimental.pallas import tpu as pltpu

F32 = jnp.float32
BF16 = jnp.bfloat16
I32 = jnp.int32

D_MODEL = 1024
DEPTH = 4
N_MIXERS = 3
DEEPNORM_ALPHA = (2.0 * DEPTH) ** 0.25
LN_EPS = 1e-5
LRU_HEADS = 8
LRU_BLOCK = D_MODEL // LRU_HEADS
CONV_WIDTH = 4
LRU_C = 8.0
RET_HEADS = 4
RET_QK = D_MODEL // RET_HEADS
RET_V = 2 * RET_QK
RET_CHUNK = 128
RET_EPS = 1e-6
ATTN_PATTERNS = ((128, 1), (512, 4), (2048, 16))
ATTN_HEADS = 16
ATTN_HEAD_DIM = D_MODEL // ATTN_HEADS
N_EXPERTS = 32
TOP_K = 4
D_FF = D_MODEL
SWIGLU_LIMIT = 7.0
SWIGLU_ALPHA = 1.702

SUBLANES = 8
LANES = 128
LANE_TILES = D_MODEL // LANES
VMEM_LIMIT = 56 * 1024 * 1024

MOE_TM = 256
ROW_DMA_PRIORITY = 1
ROW_TM = 256


def _params(sem, vmem=VMEM_LIMIT):
    return pltpu.CompilerParams(dimension_semantics=sem, vmem_limit_bytes=vmem)


def _mm_body(x_ref, w_ref, o_ref, wbf_ref):
    @pl.when(pl.program_id(1) == 0)
    def _():
        wbf_ref[...] = w_ref[...].astype(BF16)

    o_ref[...] = jnp.dot(x_ref[...].astype(BF16), wbf_ref[...],
                         preferred_element_type=F32).astype(o_ref.dtype)


def _matmul(x, w_stack, idx, col0, ncols, out_dtype, name, tm=1024, tn=1024, col_block_stride=1):
    t, k = x.shape
    assert col0 % tn == 0 and ncols % tn == 0 and t % tm == 0
    return pl.pallas_call(
        _mm_body,
        grid=(ncols // tn, t // tm),
        in_specs=[pl.BlockSpec((tm, k), lambda n, m: (m, 0)),
                  pl.BlockSpec((None, k, tn), lambda n, m: (idx, 0, col0 // tn + n * col_block_stride))],
        out_specs=pl.BlockSpec((tm, tn), lambda n, m: (m, n)),
        out_shape=jax.ShapeDtypeStruct((t, ncols), out_dtype),
        scratch_shapes=[pltpu.VMEM((k, tn), BF16)],
        compiler_params=_params(("arbitrary", "arbitrary")),
        name=name,
    )(x, w_stack)


def _layer_norm(z, gain, bias):
    mu = jnp.mean(z, axis=-1, keepdims=True)
    zc = z - mu
    var = jnp.mean(zc * zc, axis=-1, keepdims=True)
    return zc * lax.rsqrt(var + LN_EPS) * gain + bias


def _store_token_tiles(ref, val):
    tm = val.shape[0]
    for c in range(LANE_TILES):
        ref[pl.ds(c, tm, stride=LANE_TILES), :] = val[:, c * LANES:(c + 1) * LANES]


def _load_token_tiles(ref, tm):
    return jnp.concatenate([ref[pl.ds(c, tm, stride=LANE_TILES), :] for c in range(LANE_TILES)], axis=-1)


def _post_body(y_ref, w_ref, x_ref, g_ref, b_ref, wr_ref, br_ref,
               xt_ref, rf_ref, ri_ref, cnt_ref, wbf_ref, wrh_ref, wrl_ref, run_ref):
    i = pl.program_id(0)
    tm = x_ref.shape[0]

    @pl.when(i == 0)
    def _():
        wbf_ref[...] = w_ref[...].astype(BF16)
        wr = wr_ref[...]
        wrh = wr.astype(BF16)
        wrh_ref[...] = wrh
        wrl_ref[...] = (wr - wrh.astype(F32)).astype(BF16)
        run_ref[...] = jnp.zeros_like(run_ref)

    z = DEEPNORM_ALPHA * x_ref[...] + jnp.dot(y_ref[...].astype(BF16), wbf_ref[...],
                                              preferred_element_type=F32)
    xn = _layer_norm(z, g_ref[...], b_ref[...])
    _store_token_tiles(xt_ref, xn)

    xh = xn.astype(BF16)
    xl = (xn - xh.astype(F32)).astype(BF16)
    logits_tm = (jnp.dot(xh, wrh_ref[...], preferred_element_type=F32)
                 + jnp.dot(xl, wrh_ref[...], preferred_element_type=F32)
                 + jnp.dot(xh, wrl_ref[...], preferred_element_type=F32) + br_ref[...])
    logits = jnp.transpose(logits_tm)[:N_EXPERTS, :]
    row_e = lax.broadcasted_iota(I32, (N_EXPERTS, tm), 0).astype(F32)
    work = logits
    top_l, top_e, onehots = [], [], []
    for _ in range(TOP_K):
        m = jnp.max(work, axis=0, keepdims=True)
        idx = jnp.min(jnp.where(work == m, row_e, float(N_EXPERTS)), axis=0, keepdims=True)
        oh = row_e == idx
        top_l.append(m)
        top_e.append(idx)
        onehots.append(oh)
        work = jnp.where(oh, -jnp.inf, work)
    exps = [jnp.exp(l - top_l[0]) for l in top_l]
    denom = exps[0] + exps[1] + exps[2] + exps[3]
    gates = [e / denom for e in exps]

    sel = jnp.zeros((N_EXPERTS, tm), F32)
    for oh in onehots:
        sel = sel + oh.astype(F32)
    ri = lax.broadcasted_iota(I32, (tm, tm), 0)
    ci = lax.broadcasted_iota(I32, (tm, tm), 1)
    earlier = (ri < ci).astype(BF16)
    before = jnp.dot(sel.astype(BF16), earlier, preferred_element_type=F32) + run_ref[...]
    ranks = [jnp.sum(jnp.where(oh, before, 0.0), axis=0, keepdims=True) for oh in onehots]
    run_ref[...] = run_ref[...] + jnp.sum(sel, axis=1, keepdims=True)
    cnt_ref[...] = run_ref[...]

    row8 = lax.broadcasted_iota(I32, (2 * TOP_K, tm), 0)
    gate_rows = jnp.zeros((2 * TOP_K, tm), F32)
    rint = jnp.zeros((2 * TOP_K, tm), I32)
    for k in range(TOP_K):
        gate_rows = jnp.where(row8 == k, gates[k], gate_rows)
        rint = jnp.where(row8 == k, top_e[k].astype(I32), rint)
        rint = jnp.where(row8 == TOP_K + k, ranks[k].astype(I32), rint)
    ri_ref[...] = rint
    pick = (lax.broadcasted_iota(I32, (2 * TOP_K, LANES), 0)
            == lax.broadcasted_iota(I32, (2 * TOP_K, LANES), 1)).astype(F32)
    rf_ref[...] = lax.dot_general(gate_rows, pick, (((0,), (0,)), ((), ())), preferred_element_type=F32,
                                  precision=lax.Precision.HIGHEST)


def _post_mixer(y, w_out_stack, idx, x, gain, bias, w_router, b_router, name):
    t, kd = y.shape
    tm = ROW_TM
    return pl.pallas_call(
        _post_body,
        grid=(t // tm,),
        in_specs=[pl.BlockSpec((tm, kd), lambda i: (i, 0)),
                  pl.BlockSpec((None, kd, D_MODEL), lambda i: (idx, 0, 0)),
                  pl.BlockSpec((tm, D_MODEL), lambda i: (i, 0)),
                  pl.BlockSpec((1, D_MODEL), lambda i: (0, 0)),
                  pl.BlockSpec((1, D_MODEL), lambda i: (0, 0)),
                  pl.BlockSpec((D_MODEL, LANES), lambda i: (0, 0)),
                  pl.BlockSpec((1, LANES), lambda i: (0, 0))],
        out_specs=[pl.BlockSpec((tm * LANE_TILES, LANES), lambda i: (i, 0)),
                   pl.BlockSpec((tm, LANES), lambda i: (i, 0)),
                   pl.BlockSpec((2 * TOP_K, tm), lambda i: (0, i)),
                   pl.BlockSpec((N_EXPERTS, 1), lambda i: (0, 0))],
        out_shape=[jax.ShapeDtypeStruct((t * LANE_TILES, LANES), F32),
                   jax.ShapeDtypeStruct((t, LANES), F32),
                   jax.ShapeDtypeStruct((2 * TOP_K, t), I32),
                   jax.ShapeDtypeStruct((N_EXPERTS, 1), F32)],
        scratch_shapes=[pltpu.VMEM((kd, D_MODEL), BF16), pltpu.VMEM((D_MODEL, LANES), BF16),
                        pltpu.VMEM((D_MODEL, LANES), BF16), pltpu.VMEM((N_EXPERTS, 1), F32)],
        compiler_params=_params(("arbitrary",)),
        name=name,
    )(y, w_out_stack, x, gain, bias,
      jnp.pad(w_router, ((0, 0), (0, LANES - N_EXPERTS))),
      jnp.pad(b_router.reshape(1, N_EXPERTS), ((0, 0), (0, LANES - N_EXPERTS))))


def _gather_copy(x_hbm, xbuf, sem, src_row, r):
    return pltpu.make_async_copy(
        x_hbm.at[pl.ds(pl.multiple_of(src_row, SUBLANES), SUBLANES), :],
        xbuf.at[pl.ds(pl.multiple_of(r * SUBLANES, SUBLANES), SUBLANES), :],
        sem)


def _scatter_copy(ybuf, y_hbm, sem, dst_row, r):
    return pltpu.make_async_copy(
        ybuf.at[pl.ds(pl.multiple_of(r * SUBLANES, SUBLANES), SUBLANES), :],
        y_hbm.at[pl.ds(pl.multiple_of(dst_row, SUBLANES), SUBLANES), :],
        sem)


def _moe_body(be_ref, nu_ref, src_ref, srcn_ref, dst_ref, x_hbm, wgu_ref, bgu_ref, wd_ref, bd_ref,
              y_hbm, xbuf, ybuf, wgu_bf, wd_bf, gsem, ssem):
    i = pl.program_id(0)
    nb = pl.num_programs(0)
    nused = nu_ref[0]
    slot = i % 2
    rows = MOE_TM * LANE_TILES
    group = 32

    def gather_all(idx_ref, s):
        def body(g, c):
            for q in range(group):
                r = g * group + q
                _gather_copy(x_hbm, xbuf.at[s], gsem.at[s], idx_ref[0, r], r).start(priority=ROW_DMA_PRIORITY)
            return c
        lax.fori_loop(0, MOE_TM // group, body, 0)

    def scatter_all(idx_ref, s):
        def body(g, c):
            for q in range(group):
                r = g * group + q
                _scatter_copy(ybuf.at[s], y_hbm, ssem.at[s], idx_ref[0, r], r).start(priority=ROW_DMA_PRIORITY)
            return c
        lax.fori_loop(0, MOE_TM // group, body, 0)

    def wait_scatter(s):
        pltpu.make_async_copy(ybuf.at[s], y_hbm.at[pl.ds(0, rows), :], ssem.at[s]).wait()

    @pl.when(i == 0)
    def _():
        ybuf[...] = jnp.zeros_like(ybuf)
        tail = y_hbm.at[pl.ds(y_hbm.shape[0] - 2 * rows, 2 * rows), :]
        for s in range(2):
            cp = pltpu.make_async_copy(ybuf.at[s], tail.at[pl.ds(s * rows, rows), :], ssem.at[s])
            cp.start()
            cp.wait()

    @pl.when(i < nused)
    def _():
        @pl.when(i == 0)
        def _():
            gather_all(src_ref, 0)

        @pl.when(i + 1 < nused)
        def _():
            gather_all(srcn_ref, 1 - slot)

        changed = jnp.logical_or(i == 0, be_ref[i] != be_ref[jnp.maximum(i - 1, 0)])

        @pl.when(changed)
        def _():
            wgu_bf[...] = wgu_ref[...].astype(BF16)
            wd_bf[...] = wd_ref[...].astype(BF16)

        pltpu.make_async_copy(x_hbm.at[pl.ds(0, rows), :], xbuf.at[slot], gsem.at[slot]).wait()
        x = _load_token_tiles(xbuf.at[slot], MOE_TM).astype(BF16)
        gu = jnp.dot(x, wgu_bf[...], preferred_element_type=F32) + bgu_ref[...]
        gate = jnp.minimum(gu[:, :D_FF], SWIGLU_LIMIT)
        up = jnp.clip(gu[:, D_FF:], -SWIGLU_LIMIT, SWIGLU_LIMIT)
        hidden = (up + 1.0) * gate * jax.nn.sigmoid(SWIGLU_ALPHA * gate)
        y = jnp.dot(hidden.astype(BF16), wd_bf[...], preferred_element_type=F32) + bd_ref[...]

        @pl.when(i >= 2)
        def _():
            wait_scatter(slot)

        _store_token_tiles(ybuf.at[slot], y)
        scatter_all(dst_ref, slot)

    @pl.when(i == nb - 1)
    def _():
        @pl.when(nused >= 1)
        def _():
            wait_scatter((nused - 1) % 2)

        @pl.when(nused >= 2)
        def _():
            wait_scatter(nused % 2)


def _moe_experts(xt, block_exp, n_used, src_rows, dst_rows, w_gu, b_gu, w_down, b_down, layer, n_tok):
    nb = block_exp.shape[0]
    rows = MOE_TM * LANE_TILES
    n_out_rows = (TOP_K * n_tok + 2 * MOE_TM) * LANE_TILES
    smem_blk = lambda f: pl.BlockSpec((None, 1, MOE_TM), f, memory_space=pltpu.SMEM)
    grid_spec = pltpu.PrefetchScalarGridSpec(
        num_scalar_prefetch=2,
        grid=(nb,),
        in_specs=[smem_blk(lambda i, be, nu: (i, 0, 0)),
                  smem_blk(lambda i, be, nu: (jnp.minimum(i + 1, nb - 1), 0, 0)),
                  smem_blk(lambda i, be, nu: (i, 0, 0)),
                  pl.BlockSpec(memory_space=pl.ANY),
                  pl.BlockSpec((None, None, D_MODEL, 2 * D_FF), lambda i, be, nu: (layer, be[i], 0, 0)),
                  pl.BlockSpec((None, None, 1, 2 * D_FF), lambda i, be, nu: (layer, be[i], 0, 0)),
                  pl.BlockSpec((None, None, D_FF, D_MODEL), lambda i, be, nu: (layer, be[i], 0, 0)),
                  pl.BlockSpec((None, None, 1, D_MODEL), lambda i, be, nu: (layer, be[i], 0, 0))],
        out_specs=pl.BlockSpec(memory_space=pl.ANY),
        scratch_shapes=[pltpu.VMEM((2, rows, LANES), F32),
                        pltpu.VMEM((2, rows, LANES), F32),
                        pltpu.VMEM((D_MODEL, 2 * D_FF), BF16),
                        pltpu.VMEM((D_FF, D_MODEL), BF16),
                        pltpu.SemaphoreType.DMA((2,)),
                        pltpu.SemaphoreType.DMA((2,))],
    )
    return pl.pallas_call(
        _moe_body,
        grid_spec=grid_spec,
        out_shape=jax.ShapeDtypeStruct((n_out_rows, LANES), F32),
        compiler_params=_params(("arbitrary",)),
        name=f"moe_experts_l{layer}",
    )(block_exp, n_used, src_rows, src_rows, dst_rows, xt, w_gu, b_gu, w_down, b_down)


ROW_TABLE_CHUNK = 4096


def _row_table_body(lo_ref, hi_ref, dest_ref, tbl_ref, *, n_assign):
    i = pl.program_id(0)

    @pl.when(i == 0)
    def _():
        def fill_range(e, c):
            def fill(r, c2):
                tbl_ref[r] = n_assign + jnp.bitwise_and(r, 2 * MOE_TM - 1)
                return c2
            lax.fori_loop(lo_ref[e], hi_ref[e], fill, 0)
            return c
        lax.fori_loop(0, lo_ref.shape[0], fill_range, 0)

    batch = 16

    def body(g, c):
        base = g * batch
        rows = [dest_ref[0, base + q] for q in range(batch)]
        first = i * ROW_TABLE_CHUNK + base
        for q, r in enumerate(rows):
            tbl_ref[r] = first + q
        return c
    lax.fori_loop(0, ROW_TABLE_CHUNK // batch, body, 0)


def _row_table(dest, pad_lo, pad_hi, n_rows, name):
    n_assign = dest.shape[0]
    steps = n_assign // ROW_TABLE_CHUNK
    grid_spec = pltpu.PrefetchScalarGridSpec(
        num_scalar_prefetch=2,
        grid=(steps,),
        in_specs=[pl.BlockSpec((None, 1, ROW_TABLE_CHUNK), lambda i, lo, hi: (i, 0, 0),
                               memory_space=pltpu.SMEM)],
        out_specs=pl.BlockSpec(memory_space=pltpu.SMEM),
    )
    return pl.pallas_call(
        functools.partial(_row_table_body, n_assign=n_assign),
        grid_spec=grid_spec,
        out_shape=jax.ShapeDtypeStruct((n_rows,), I32),
        compiler_params=_params(("arbitrary",)),
        name=name,
    )(pad_lo, pad_hi, dest.reshape(steps, 1, ROW_TABLE_CHUNK))


def _route_tables(route_i, counts, n_tok, name):
    n_assign = n_tok * TOP_K
    nb = n_assign // MOE_TM + N_EXPERTS
    counts = counts.reshape(N_EXPERTS).astype(I32)
    padded = ((counts + MOE_TM - 1) // MOE_TM) * MOE_TM
    pad_end = jnp.cumsum(padded)
    pad_start = pad_end - padded
    eid = route_i[:TOP_K]
    rank = route_i[TOP_K:]
    expert_ids = jnp.arange(N_EXPERTS, dtype=I32)
    start_of = jnp.sum(jnp.where(eid[..., None] == expert_ids, pad_start, 0), axis=-1)
    dest = (start_of + rank).reshape(n_assign)
    pad_lo = jnp.concatenate([pad_start + counts, pad_end[-1:]]).astype(I32)
    pad_hi = jnp.concatenate([pad_end, jnp.full((1,), nb * MOE_TM, I32)]).astype(I32)
    row_assign = _row_table(dest, pad_lo, pad_hi, nb * MOE_TM, name)
    src_rows = jnp.where(row_assign < n_assign, row_assign % n_tok, 0) * LANE_TILES
    dst_rows = row_assign * LANE_TILES
    block_start = jnp.arange(nb, dtype=I32) * MOE_TM
    block_exp = jnp.minimum(jnp.sum((pad_end[None, :] <= block_start[:, None]).astype(I32), axis=1),
                            N_EXPERTS - 1)
    n_used = (pad_end[-1:] // MOE_TM).astype(I32)
    return block_exp, n_used, src_rows.reshape(nb, 1, MOE_TM), dst_rows.reshape(nb, 1, MOE_TM)


def _combine_body(y0_ref, y1_ref, y2_ref, y3_ref, rf_ref, xt_ref, g_ref, b_ref, o_ref, obf_ref):
    tm = o_ref.shape[0]
    rf = rf_ref[...]
    parts = []
    for c in range(LANE_TILES):
        acc = DEEPNORM_ALPHA * xt_ref[pl.ds(c, tm, stride=LANE_TILES), :]
        for k, yk in enumerate((y0_ref, y1_ref, y2_ref, y3_ref)):
            acc = acc + rf[:, k:k + 1] * yk[pl.ds(c, tm, stride=LANE_TILES), :]
        parts.append(acc)
    z = jnp.concatenate(parts, axis=-1)
    out = _layer_norm(z, g_ref[...], b_ref[...])
    o_ref[...] = out
    obf_ref[...] = out.astype(BF16)


def _combine(y_tk, route_f, xt, gain, bias, n_tok, name):
    tm = ROW_TM
    nblk = n_tok // tm
    yspec = lambda k: pl.BlockSpec((tm * LANE_TILES, LANES), lambda i: (k * nblk + i, 0))
    return pl.pallas_call(
        _combine_body,
        grid=(nblk,),
        in_specs=[yspec(0), yspec(1), yspec(2), yspec(3),
                  pl.BlockSpec((tm, LANES), lambda i: (i, 0)),
                  pl.BlockSpec((tm * LANE_TILES, LANES), lambda i: (i, 0)),
                  pl.BlockSpec((1, D_MODEL), lambda i: (0, 0)),
                  pl.BlockSpec((1, D_MODEL), lambda i: (0, 0))],
        out_specs=[pl.BlockSpec((tm, D_MODEL), lambda i: (i, 0)),
                   pl.BlockSpec((tm, D_MODEL), lambda i: (i, 0))],
        out_shape=[jax.ShapeDtypeStruct((n_tok, D_MODEL), F32),
                   jax.ShapeDtypeStruct((n_tok, D_MODEL), BF16)],
        compiler_params=_params(("parallel",)),
        name=name,
    )(y_tk, y_tk, y_tk, y_tk, route_f, xt, gain, bias)


def _gelu_tanh(x):
    return 0.5 * x * (1.0 + jnp.tanh(math.sqrt(2.0 / math.pi) * (x + 0.044715 * (x * x * x))))


LRU_ROWS = 256


def _lru_body(rec_ref, g_ref, cw_ref, cb_ref, wr_ref, br_ref, wi_ref, bi_ref, lam_ref, y_ref,
              ext_ref, a_ref, b_ref):
    seq, ct = rec_ref.shape
    heads = ct // LRU_BLOCK
    ext_ref[0:SUBLANES, :] = jnp.zeros((SUBLANES, ct), F32)
    ext_ref[SUBLANES:, :] = rec_ref[...]
    neg_lam = -lam_ref[...]
    softplus = jnp.maximum(neg_lam, 0.0) + jnp.log1p(jnp.exp(-jnp.abs(neg_lam)))
    wr = [wr_ref[h].astype(BF16) for h in range(heads)]
    wi = [wi_ref[h].astype(BF16) for h in range(heads)]

    for c in range(seq // LRU_ROWS):
        r0 = c * LRU_ROWS
        conv = cb_ref[...] + cw_ref[CONV_WIDTH - 1:CONV_WIDTH, :] * rec_ref[pl.ds(r0, LRU_ROWS), :]
        for j in range(CONV_WIDTH - 1):
            conv = conv + cw_ref[j:j + 1, :] * ext_ref[pl.ds(r0 + SUBLANES - (CONV_WIDTH - 1) + j, LRU_ROWS), :]
        xb = conv.astype(BF16)
        r_parts = [jnp.dot(xb[:, h * LRU_BLOCK:(h + 1) * LRU_BLOCK], wr[h], preferred_element_type=F32)
                   for h in range(heads)]
        i_parts = [jnp.dot(xb[:, h * LRU_BLOCK:(h + 1) * LRU_BLOCK], wi[h], preferred_element_type=F32)
                   for h in range(heads)]
        r = jax.nn.sigmoid(jnp.concatenate(r_parts, axis=-1) + br_ref[...])
        ig = jax.nn.sigmoid(jnp.concatenate(i_parts, axis=-1) + bi_ref[...])
        log_a = -LRU_C * r * softplus
        a = jnp.exp(log_a)
        a_ref[pl.ds(r0, LRU_ROWS), :] = a
        b_ref[pl.ds(r0, LRU_ROWS), :] = jnp.sqrt(-jnp.tanh(log_a) * (a * a + 1.0)) * (ig * conv)

    _lru_scan_body(a_ref, b_ref, g_ref, y_ref)


def _lru_scan_body(a_ref, b_ref, g_ref, y_ref):
    seq, ct = a_ref.shape
    row = lax.broadcasted_iota(I32, (SUBLANES, ct), 0)

    def step(gi, carry):
        off = pl.multiple_of(gi * SUBLANES, SUBLANES)
        a = a_ref[pl.ds(off, SUBLANES), :]
        b = b_ref[pl.ds(off, SUBLANES), :]
        for d in (1, 2, 4):
            keep = row >= d
            b = jnp.where(keep, a * pltpu.roll(b, d, 0) + b, b)
            a = jnp.where(keep, a * pltpu.roll(a, d, 0), a)
        h = a * carry + b
        y_ref[pl.ds(off, SUBLANES), :] = _gelu_tanh(g_ref[pl.ds(off, SUBLANES), :]) * h
        return jnp.broadcast_to(h[SUBLANES - 1:SUBLANES, :], (SUBLANES, ct))

    lax.fori_loop(0, seq // SUBLANES, step, jnp.zeros((SUBLANES, ct), F32), unroll=4)


def _rglru_mixer(x2, bsz, seq, idx, a_w_in, conv_w, conv_b, w_rgate, b_rgate, w_igate, b_igate, lam, layer):
    t = bsz * seq
    w = D_MODEL
    u = _matmul(x2, a_w_in, idx, 0, 2 * w, F32, f"lru_in_l{layer}")
    ct = 2 * LRU_BLOCK
    nct = w // ct
    hpt = ct // LRU_BLOCK
    vec = lambda: pl.BlockSpec((1, ct), lambda b, j: (0, j))
    return pl.pallas_call(
        _lru_body,
        grid=(bsz, nct),
        in_specs=[pl.BlockSpec((seq, ct), lambda b, j: (b, nct + j)),
                  pl.BlockSpec((seq, ct), lambda b, j: (b, j)),
                  pl.BlockSpec((CONV_WIDTH, ct), lambda b, j: (0, j)),
                  vec(),
                  pl.BlockSpec((hpt, LRU_BLOCK, LRU_BLOCK), lambda b, j: (j, 0, 0)),
                  vec(),
                  pl.BlockSpec((hpt, LRU_BLOCK, LRU_BLOCK), lambda b, j: (j, 0, 0)),
                  vec(), vec()],
        out_specs=pl.BlockSpec((seq, ct), lambda b, j: (b, j)),
        out_shape=jax.ShapeDtypeStruct((t, w), F32),
        scratch_shapes=[pltpu.VMEM((seq + SUBLANES, ct), F32),
                        pltpu.VMEM((seq, ct), F32),
                        pltpu.VMEM((seq, ct), F32)],
        compiler_params=_params(("parallel", "parallel")),
        name=f"lru_l{layer}",
    )(u, u, conv_w[idx], conv_b[idx][None], w_rgate[idx], b_rgate[idx][None],
      w_igate[idx], b_igate[idx][None], lam[idx][None])


def _ret_body(lg_ref, q_ref, k_ref, v_ref, g_ref, y_ref, state_ref):
    c = pl.program_id(1)
    cs = RET_CHUNK
    heads = range(RET_HEADS)

    @pl.when(c == 0)
    def _():
        state_ref[...] = jnp.zeros_like(state_ref)

    qi = lax.broadcasted_iota(I32, (cs, cs), 0)
    ki = lax.broadcasted_iota(I32, (cs, cs), 1)
    rel = (qi - ki).astype(F32)
    pos_col = lax.broadcasted_iota(I32, (cs, 1), 0).astype(F32)
    lgs = [lg_ref[h] for h in heads]
    intra_decay = [jnp.where(rel >= 0, jnp.exp(lg * jnp.maximum(rel, 0.0)), 0.0) for lg in lgs]
    query_decay = [jnp.exp(lg * (pos_col + 1.0)) for lg in lgs]
    key_decay = [jnp.exp(lg * (cs - 1.0 - pos_col)) for lg in lgs]
    chunk_decay = [jnp.exp(jnp.full((1, 1), lg * cs, F32)) for lg in lgs]

    scale = RET_QK ** -0.5
    dn_nt = (((1,), (1,)), ((), ()))
    dn_tn = (((0,), (0,)), ((), ()))
    qs = [q_ref[:, h * RET_QK:(h + 1) * RET_QK] for h in heads]
    ks = [k_ref[:, h * RET_QK:(h + 1) * RET_QK] for h in heads]
    vs = [v_ref[:, h * RET_V:(h + 1) * RET_V] for h in heads]
    intra = [lax.dot_general(q, k, dn_nt, preferred_element_type=F32) * scale * d
             for q, k, d in zip(qs, ks, intra_decay)]
    states = [state_ref[h] for h in heads]
    cross = [jnp.dot(q, s.astype(BF16), preferred_element_type=F32) * d
             for q, s, d in zip(qs, states, query_decay)]
    outs = [jnp.dot(a.astype(BF16), v, preferred_element_type=F32) + x for a, v, x in zip(intra, vs, cross)]
    kds = [(k.astype(F32) * (scale * d)).astype(BF16) for k, d in zip(ks, key_decay)]
    upd = [lax.dot_general(kd, v, dn_tn, preferred_element_type=F32) for kd, v in zip(kds, vs)]
    for h in heads:
        state_ref[h] = chunk_decay[h] * states[h] + upd[h]
    for h in heads:
        out = outs[h]
        o = out * lax.rsqrt(jnp.mean(out * out, axis=-1, keepdims=True) + RET_EPS)
        g = g_ref[:, h * RET_V:(h + 1) * RET_V]
        y_ref[:, h * RET_V:(h + 1) * RET_V] = (g * jax.nn.sigmoid(g) * o).astype(y_ref.dtype)


def _retention_mixer(x2, bsz, seq, idx, b_w_in, layer):
    t = bsz * seq
    qk_w = RET_HEADS * RET_QK
    v_w = RET_HEADS * RET_V
    qkv = _matmul(x2, b_w_in, idx, 0, 2 * qk_w + v_w, BF16, f"ret_qkv_l{layer}")
    gate = _matmul(x2, b_w_in, idx, 2 * qk_w + v_w, v_w, F32, f"ret_gate_l{layer}")
    nc = seq // RET_CHUNK
    log_gamma = jnp.log1p(-jnp.exp2(-5.0 - jnp.arange(RET_HEADS, dtype=F32)))
    grid_spec = pltpu.PrefetchScalarGridSpec(
        num_scalar_prefetch=1,
        grid=(bsz, nc),
        in_specs=[pl.BlockSpec((RET_CHUNK, qk_w), lambda b, c, lg: (b * nc + c, 0)),
                  pl.BlockSpec((RET_CHUNK, qk_w), lambda b, c, lg: (b * nc + c, 1)),
                  pl.BlockSpec((RET_CHUNK, v_w), lambda b, c, lg: (b * nc + c, 1)),
                  pl.BlockSpec((RET_CHUNK, v_w), lambda b, c, lg: (b * nc + c, 0))],
        out_specs=pl.BlockSpec((RET_CHUNK, v_w), lambda b, c, lg: (b * nc + c, 0)),
        scratch_shapes=[pltpu.VMEM((RET_HEADS, RET_QK, RET_V), F32)],
    )
    return pl.pallas_call(
        _ret_body,
        grid_spec=grid_spec,
        out_shape=jax.ShapeDtypeStruct((t, v_w), BF16),
        compiler_params=_params(("parallel", "arbitrary")),
        name=f"retention_l{layer}",
    )(log_gamma, qkv, qkv, qkv, gate)


def _attn_body(*refs, dilation, has_prev):
    if has_prev:
        q_ref, kc_ref, kp_ref, vc_ref, vp_ref, o_ref, st_ref = refs
    else:
        q_ref, kc_ref, vc_ref, o_ref, st_ref = refs
    n = pl.program_id(2)
    nw = q_ref.shape[0]
    dh = ATTN_HEAD_DIM
    scale = dh ** -0.5
    qi = lax.broadcasted_iota(I32, (nw, nw), 0)
    ki = lax.broadcasted_iota(I32, (nw, nw), 1)
    steps_c = qi - ki
    valid_c = steps_c >= 0
    dist_c = (steps_c * dilation).astype(F32)
    q = q_ref[...]
    kc = kc_ref[...]
    vc = vc_ref[...]
    lane = lax.broadcasted_iota(I32, (nw, LANES), 1)
    dn = (((1,), (1,)), ((), ()))
    heads = range(ATTN_HEADS)
    sls = [slice(h * dh, (h + 1) * dh) for h in heads]
    slopes = [2.0 ** (-8.0 * (h + 1) / ATTN_HEADS) for h in heads]
    sc_c = [lax.dot_general(q[:, sl], kc[:, sl], dn, preferred_element_type=F32) for sl in sls]
    sc_c = [jnp.where(valid_c, s * scale - sp * dist_c, -jnp.inf) for s, sp in zip(sc_c, slopes)]
    ms = [jnp.max(a, axis=-1, keepdims=True) for a in sc_c]
    if has_prev:
        steps_p = steps_c + nw
        valid_p = steps_p <= jnp.where(n > 0, nw, -1)
        dist_p = (steps_p * dilation).astype(F32)
        kp = kp_ref[...]
        vp = vp_ref[...]
        sc_p = [lax.dot_general(q[:, sl], kp[:, sl], dn, preferred_element_type=F32) for sl in sls]
        sc_p = [jnp.where(valid_p, s * scale - sp * dist_p, -jnp.inf) for s, sp in zip(sc_p, slopes)]
        ms = [jnp.maximum(m, jnp.max(b, axis=-1, keepdims=True)) for m, b in zip(ms, sc_p)]
    pc = [jnp.exp(a - m) for a, m in zip(sc_c, ms)]
    sums = [jnp.sum(a, axis=-1, keepdims=True) for a in pc]
    outs = [jnp.dot(a.astype(BF16), vc[:, sl], preferred_element_type=F32) for a, sl in zip(pc, sls)]
    if has_prev:
        pp = [jnp.exp(b - m) for b, m in zip(sc_p, ms)]
        sums = [s + jnp.sum(b, axis=-1, keepdims=True) for s, b in zip(sums, pp)]
        outs = [o + jnp.dot(b.astype(BF16), vp[:, sl], preferred_element_type=F32)
                for o, b, sl in zip(outs, pp, sls)]
    stats = jnp.zeros((nw, LANES), F32)
    for h, sl, o, m, sm in zip(heads, sls, outs, ms, sums):
        o_ref[:, sl] = o.astype(o_ref.dtype)
        stats = jnp.where(lane == h, m, stats)
        stats = jnp.where(lane == ATTN_HEADS + h, sm, stats)
    st_ref[...] = stats


def _merge_body(o0_ref, o1_ref, o2_ref, s0_ref, s1_ref, s2_ref, y_ref):
    tm = y_ref.shape[0]
    stats = [s0_ref[...], s1_ref[...], s2_ref[...]]
    lane = lax.broadcasted_iota(I32, (tm, LANES), 1)
    maxes = [s for s in stats]
    sums = [pltpu.roll(s, LANES - ATTN_HEADS, 1) for s in stats]
    m_all = jnp.maximum(jnp.maximum(maxes[0], maxes[1]), maxes[2])
    wgt = [jnp.exp(m - m_all) for m in maxes]
    den = wgt[0] * sums[0] + wgt[1] * sums[1] + wgt[2] * sums[2]
    er = lax.broadcasted_iota(I32, (LANES, D_MODEL), 0)
    ec = lax.broadcasted_iota(I32, (LANES, D_MODEL), 1)
    expand = (ec // ATTN_HEAD_DIM == er).astype(F32)
    acc = jnp.zeros((tm, D_MODEL), F32)
    for w, o_ref in zip(wgt, (o0_ref, o1_ref, o2_ref)):
        coef = jnp.where(lane < ATTN_HEADS, w / den, 0.0)
        acc = acc + jnp.dot(coef, expand, preferred_element_type=F32,
                            precision=lax.Precision.HIGHEST) * o_ref[...].astype(F32)
    y_ref[...] = acc.astype(y_ref.dtype)


def _attention_mixer(x2, bsz, seq, idx, c_w_in, layer):
    t = bsz * seq
    w = D_MODEL
    n_groups = len(ATTN_PATTERNS)
    outs, stats = [], []
    for g, (window, dilation) in enumerate(ATTN_PATTERNS):
        nw = window // dilation
        sub = seq // dilation
        assert nw == 128 and sub % nw == 0
        nb = sub // nw

        def to_sub(a):
            return a.reshape(bsz, sub, dilation, -1).transpose(0, 2, 1, 3).reshape(t, -1)

        def from_sub(a):
            return a.reshape(bsz, dilation, sub, -1).transpose(0, 2, 1, 3).reshape(t, -1)

        xs = x2 if dilation == 1 else to_sub(x2)
        qkv = _matmul(xs, c_w_in, idx, g * w, 3 * w, BF16, f"attn_qkv_g{g}_l{layer}",
                      col_block_stride=n_groups)
        cur = lambda slab: pl.BlockSpec((nw, w), lambda b, r, n, slab=slab: ((b * dilation + r) * nb + n, slab))
        prev = lambda slab: pl.BlockSpec(
            (nw, w), lambda b, r, n, slab=slab: ((b * dilation + r) * nb + jnp.maximum(n - 1, 0), slab))
        has_prev = nb > 1
        in_specs = [cur(0), cur(1), prev(1), cur(2), prev(2)] if has_prev else [cur(0), cur(1), cur(2)]
        o, st = pl.pallas_call(
            functools.partial(_attn_body, dilation=dilation, has_prev=has_prev),
            grid=(bsz, dilation, nb),
            in_specs=in_specs,
            out_specs=[pl.BlockSpec((nw, w), lambda b, r, n: ((b * dilation + r) * nb + n, 0)),
                       pl.BlockSpec((nw, LANES), lambda b, r, n: ((b * dilation + r) * nb + n, 0))],
            out_shape=[jax.ShapeDtypeStruct((t, w), BF16), jax.ShapeDtypeStruct((t, LANES), F32)],
            compiler_params=_params(("parallel", "parallel", "parallel")),
            name=f"attn_g{g}_l{layer}",
        )(*([qkv] * len(in_specs)))
        outs.append(o if dilation == 1 else from_sub(o))
        stats.append(st if dilation == 1 else from_sub(st))
    tm = ROW_TM
    return pl.pallas_call(
        _merge_body,
        grid=(t // tm,),
        in_specs=[pl.BlockSpec((tm, w), lambda i: (i, 0))] * 3 + [pl.BlockSpec((tm, LANES), lambda i: (i, 0))] * 3,
        out_specs=pl.BlockSpec((tm, w), lambda i: (i, 0)),
        out_shape=jax.ShapeDtypeStruct((t, w), BF16),
        compiler_params=_params(("parallel",)),
        name=f"attn_merge_l{layer}",
    )(*outs, *stats)


def kernel(x, a_w_in, a_conv_w, a_conv_b, a_w_rgate, a_b_rgate, a_w_igate, a_b_igate, a_lambda, a_w_out,
           b_w_in, b_w_out, c_w_in, c_w_out, ln_gain, ln_bias, moe_w_router, moe_b_router, moe_w_gu,
           moe_b_gu, moe_w_down, moe_b_down):
    bsz, seq, d = x.shape
    t = bsz * seq
    x2 = x.reshape(t, d)
    x_mxu = x2
    b_gu4 = moe_b_gu[:, :, None, :]
    b_down4 = moe_b_down[:, :, None, :]
    for layer in range(DEPTH):
        kind, idx = layer % N_MIXERS, layer // N_MIXERS
        if kind == 0:
            y = _rglru_mixer(x_mxu, bsz, seq, idx, a_w_in, a_conv_w, a_conv_b, a_w_rgate, a_b_rgate,
                             a_w_igate, a_b_igate, a_lambda, layer)
            w_out = a_w_out
        elif kind == 1:
            y = _retention_mixer(x_mxu, bsz, seq, idx, b_w_in, layer)
            w_out = b_w_out
        else:
            y = _attention_mixer(x_mxu, bsz, seq, idx, c_w_in, layer)
            w_out = c_w_out
        xt, route_f, route_i, counts = _post_mixer(
            y, w_out, idx, x2, ln_gain[layer, 0][None], ln_bias[layer, 0][None],
            moe_w_router[layer], moe_b_router[layer][None], f"post_mixer_l{layer}")
        block_exp, n_used, src_rows, dst_rows = _route_tables(route_i, counts, t, f"row_table_l{layer}")
        y_tk = _moe_experts(xt, block_exp, n_used, src_rows, dst_rows,
                            moe_w_gu, b_gu4, moe_w_down, b_down4, layer, t)
        x2, x_mxu = _combine(y_tk, route_f, xt, ln_gain[layer, 1][None], ln_bias[layer, 1][None], t,
                             f"combine_l{layer}")
    return x2.reshape(bsz, seq, d)
```

```python
import functools
import math

import jax
import jax.numpy as jnp
from jax import lax
from jax.experimental import pallas as pl
from jax.experimental.pallas import tpu as pltpu

F32 = jnp.float32
BF16 = jnp.bfloat16
I32 = jnp.int32

D_MODEL = 1024
DEPTH = 4
N_MIXERS = 3
DEEPNORM_ALPHA = (2.0 * DEPTH) ** 0.25
LN_EPS = 1e-5
LRU_HEADS = 8
LRU_BLOCK = D_MODEL // LRU_HEADS
CONV_WIDTH = 4
LRU_C = 8.0
RET_HEADS = 4
RET_QK = D_MODEL // RET_HEADS
RET_V = 2 * RET_QK
RET_CHUNK = 128
RET_EPS = 1e-6
ATTN_PATTERNS = ((128, 1), (512, 4), (2048, 16))
ATTN_HEADS = 16
ATTN_HEAD_DIM = D_MODEL // ATTN_HEADS
N_EXPERTS = 32
TOP_K = 4
D_FF = D_MODEL
SWIGLU_LIMIT = 7.0
SWIGLU_ALPHA = 1.702

SUBLANES = 8
LANES = 128
LANE_TILES = D_MODEL // LANES
VMEM_LIMIT = 56 * 1024 * 1024

MOE_TM = 256
ROW_TM = 256


def _params(sem, vmem=VMEM_LIMIT):
    return pltpu.CompilerParams(dimension_semantics=sem, vmem_limit_bytes=vmem)


def _mm_body(x_ref, w_ref, o_ref, wbf_ref):
    @pl.when(pl.program_id(1) == 0)
    def _():
        wbf_ref[...] = w_ref[...].astype(BF16)

    o_ref[...] = jnp.dot(x_ref[...].astype(BF16), wbf_ref[...],
                         preferred_element_type=F32).astype(o_ref.dtype)


def _matmul(x, w_stack, idx, col0, ncols, out_dtype, name, tm=1024, tn=1024, col_block_stride=1):
    t, k = x.shape
    assert col0 % tn == 0 and ncols % tn == 0 and t % tm == 0
    return pl.pallas_call(
        _mm_body,
        grid=(ncols // tn, t // tm),
        in_specs=[pl.BlockSpec((tm, k), lambda n, m: (m, 0)),
                  pl.BlockSpec((None, k, tn), lambda n, m: (idx, 0, col0 // tn + n * col_block_stride))],
        out_specs=pl.BlockSpec((tm, tn), lambda n, m: (m, n)),
        out_shape=jax.ShapeDtypeStruct((t, ncols), out_dtype),
        scratch_shapes=[pltpu.VMEM((k, tn), BF16)],
        compiler_params=_params(("arbitrary", "arbitrary")),
        name=name,
    )(x, w_stack)


def _layer_norm(z, gain, bias):
    mu = jnp.mean(z, axis=-1, keepdims=True)
    zc = z - mu
    var = jnp.mean(zc * zc, axis=-1, keepdims=True)
    return zc * lax.rsqrt(var + LN_EPS) * gain + bias


def _store_token_tiles(ref, val):
    tm = val.shape[0]
    for c in range(LANE_TILES):
        ref[pl.ds(c, tm, stride=LANE_TILES), :] = val[:, c * LANES:(c + 1) * LANES]


def _load_token_tiles(ref, tm):
    return jnp.concatenate([ref[pl.ds(c, tm, stride=LANE_TILES), :] for c in range(LANE_TILES)], axis=-1)


def _post_body(y_ref, w_ref, x_ref, g_ref, b_ref, wr_ref, br_ref,
               xt_ref, rf_ref, ri_ref, cnt_ref, wbf_ref, wrh_ref, wrl_ref, run_ref):
    i = pl.program_id(0)
    tm = x_ref.shape[0]

    @pl.when(i == 0)
    def _():
        wbf_ref[...] = w_ref[...].astype(BF16)
        wr = wr_ref[...]
        wrh = wr.astype(BF16)
        wrh_ref[...] = wrh
        wrl_ref[...] = (wr - wrh.astype(F32)).astype(BF16)
        run_ref[...] = jnp.zeros_like(run_ref)

    z = DEEPNORM_ALPHA * x_ref[...] + jnp.dot(y_ref[...].astype(BF16), wbf_ref[...],
                                              preferred_element_type=F32)
    xn = _layer_norm(z, g_ref[...], b_ref[...])
    _store_token_tiles(xt_ref, xn)

    xh = xn.astype(BF16)
    xl = (xn - xh.astype(F32)).astype(BF16)
    logits_tm = (jnp.dot(xh, wrh_ref[...], preferred_element_type=F32)
                 + jnp.dot(xl, wrh_ref[...], preferred_element_type=F32)
                 + jnp.dot(xh, wrl_ref[...], preferred_element_type=F32) + br_ref[...])
    logits = jnp.transpose(logits_tm)[:N_EXPERTS, :]
    row_e = lax.broadcasted_iota(I32, (N_EXPERTS, tm), 0).astype(F32)
    work = logits
    top_l, top_e, onehots = [], [], []
    for _ in range(TOP_K):
        m = jnp.max(work, axis=0, keepdims=True)
        idx = jnp.min(jnp.where(work == m, row_e, float(N_EXPERTS)), axis=0, keepdims=True)
        oh = row_e == idx
        top_l.append(m)
        top_e.append(idx)
        onehots.append(oh)
        work = jnp.where(oh, -jnp.inf, work)
    exps = [jnp.exp(l - top_l[0]) for l in top_l]
    denom = exps[0] + exps[1] + exps[2] + exps[3]
    gates = [e / denom for e in exps]

    sel = jnp.zeros((N_EXPERTS, tm), F32)
    for oh in onehots:
        sel = sel + oh.astype(F32)
    ri = lax.broadcasted_iota(I32, (tm, tm), 0)
    ci = lax.broadcasted_iota(I32, (tm, tm), 1)
    earlier = (ri < ci).astype(BF16)
    before = jnp.dot(sel.astype(BF16), earlier, preferred_element_type=F32) + run_ref[...]
    ranks = [jnp.sum(jnp.where(oh, before, 0.0), axis=0, keepdims=True) for oh in onehots]
    run_ref[...] = run_ref[...] + jnp.sum(sel, axis=1, keepdims=True)
    cnt_ref[...] = run_ref[...]

    row8 = lax.broadcasted_iota(I32, (2 * TOP_K, tm), 0)
    gate_rows = jnp.zeros((2 * TOP_K, tm), F32)
    rint = jnp.zeros((2 * TOP_K, tm), I32)
    for k in range(TOP_K):
        gate_rows = jnp.where(row8 == k, gates[k], gate_rows)
        rint = jnp.where(row8 == k, top_e[k].astype(I32), rint)
        rint = jnp.where(row8 == TOP_K + k, ranks[k].astype(I32), rint)
    ri_ref[...] = rint
    pick = (lax.broadcasted_iota(I32, (2 * TOP_K, LANES), 0)
            == lax.broadcasted_iota(I32, (2 * TOP_K, LANES), 1)).astype(F32)
    rf_ref[...] = lax.dot_general(gate_rows, pick, (((0,), (0,)), ((), ())), preferred_element_type=F32,
                                  precision=lax.Precision.HIGHEST)


def _post_mixer(y, w_out_stack, idx, x, gain, bias, w_router, b_router, name):
    t, kd = y.shape
    tm = ROW_TM
    return pl.pallas_call(
        _post_body,
        grid=(t // tm,),
        in_specs=[pl.BlockSpec((tm, kd), lambda i: (i, 0)),
                  pl.BlockSpec((None, kd, D_MODEL), lambda i: (idx, 0, 0)),
                  pl.BlockSpec((tm, D_MODEL), lambda i: (i, 0)),
                  pl.BlockSpec((1, D_MODEL), lambda i: (0, 0)),
                  pl.BlockSpec((1, D_MODEL), lambda i: (0, 0)),
                  pl.BlockSpec((D_MODEL, LANES), lambda i: (0, 0)),
                  pl.BlockSpec((1, LANES), lambda i: (0, 0))],
        out_specs=[pl.BlockSpec((tm * LANE_TILES, LANES), lambda i: (i, 0)),
                   pl.BlockSpec((tm, LANES), lambda i: (i, 0)),
                   pl.BlockSpec((2 * TOP_K, tm), lambda i: (0, i)),
                   pl.BlockSpec((N_EXPERTS, 1), lambda i: (0, 0))],
        out_shape=[jax.ShapeDtypeStruct((t * LANE_TILES, LANES), F32),
                   jax.ShapeDtypeStruct((t, LANES), F32),
                   jax.ShapeDtypeStruct((2 * TOP_K, t), I32),
                   jax.ShapeDtypeStruct((N_EXPERTS, 1), F32)],
        scratch_shapes=[pltpu.VMEM((kd, D_MODEL), BF16), pltpu.VMEM((D_MODEL, LANES), BF16),
                        pltpu.VMEM((D_MODEL, LANES), BF16), pltpu.VMEM((N_EXPERTS, 1), F32)],
        compiler_params=_params(("arbitrary",)),
        name=name,
    )(y, w_out_stack, x, gain, bias,
      jnp.pad(w_router, ((0, 0), (0, LANES - N_EXPERTS))),
      jnp.pad(b_router.reshape(1, N_EXPERTS), ((0, 0), (0, LANES - N_EXPERTS))))


def _gather_copy(x_hbm, xbuf, sem, src_row, r):
    return pltpu.make_async_copy(
        x_hbm.at[pl.ds(pl.multiple_of(src_row, SUBLANES), SUBLANES), :],
        xbuf.at[pl.ds(pl.multiple_of(r * SUBLANES, SUBLANES), SUBLANES), :],
        sem)


def _scatter_copy(ybuf, y_hbm, sem, dst_row, r):
    return pltpu.make_async_copy(
        ybuf.at[pl.ds(pl.multiple_of(r * SUBLANES, SUBLANES), SUBLANES), :],
        y_hbm.at[pl.ds(pl.multiple_of(dst_row, SUBLANES), SUBLANES), :],
        sem)


def _moe_body(be_ref, nu_ref, src_ref, srcn_ref, dst_ref, x_hbm, wgu_ref, bgu_ref, wd_ref, bd_ref,
              y_hbm, xbuf, ybuf, wgu_bf, wd_bf, gsem, ssem):
    i = pl.program_id(0)
    nb = pl.num_programs(0)
    nused = nu_ref[0]
    slot = i % 2
    rows = MOE_TM * LANE_TILES
    group = 32

    def gather_all(idx_ref, s):
        def body(g, c):
            for q in range(group):
                r = g * group + q
                _gather_copy(x_hbm, xbuf.at[s], gsem.at[s], idx_ref[0, r], r).start(priority=q % 2)
            return c
        lax.fori_loop(0, MOE_TM // group, body, 0)

    def scatter_all(idx_ref, s):
        def body(g, c):
            for q in range(group):
                r = g * group + q
                _scatter_copy(ybuf.at[s], y_hbm, ssem.at[s], idx_ref[0, r], r).start(priority=q % 2)
            return c
        lax.fori_loop(0, MOE_TM // group, body, 0)

    def wait_scatter(s):
        pltpu.make_async_copy(ybuf.at[s], y_hbm.at[pl.ds(0, rows), :], ssem.at[s]).wait()

    @pl.when(i == 0)
    def _():
        ybuf[...] = jnp.zeros_like(ybuf)
        tail = y_hbm.at[pl.ds(y_hbm.shape[0] - 2 * rows, 2 * rows), :]
        for s in range(2):
            cp = pltpu.make_async_copy(ybuf.at[s], tail.at[pl.ds(s * rows, rows), :], ssem.at[s])
            cp.start()
            cp.wait()

    @pl.when(i < nused)
    def _():
        @pl.when(i == 0)
        def _():
            gather_all(src_ref, 0)

        @pl.when(i + 1 < nused)
        def _():
            gather_all(srcn_ref, 1 - slot)

        changed = jnp.logical_or(i == 0, be_ref[i] != be_ref[jnp.maximum(i - 1, 0)])

        @pl.when(changed)
        def _():
            wgu_bf[...] = wgu_ref[...].astype(BF16)
            wd_bf[...] = wd_ref[...].astype(BF16)

        pltpu.make_async_copy(x_hbm.at[pl.ds(0, rows), :], xbuf.at[slot], gsem.at[slot]).wait()
        x = _load_token_tiles(xbuf.at[slot], MOE_TM).astype(BF16)
        gu = jnp.dot(x, wgu_bf[...], preferred_element_type=F32) + bgu_ref[...]
        gate = jnp.minimum(gu[:, :D_FF], SWIGLU_LIMIT)
        up = jnp.clip(gu[:, D_FF:], -SWIGLU_LIMIT, SWIGLU_LIMIT)
        hidden = (up + 1.0) * gate * jax.nn.sigmoid(SWIGLU_ALPHA * gate)
        y = jnp.dot(hidden.astype(BF16), wd_bf[...], preferred_element_type=F32) + bd_ref[...]

        @pl.when(i >= 2)
        def _():
            wait_scatter(slot)

        _store_token_tiles(ybuf.at[slot], y)
        scatter_all(dst_ref, slot)

    @pl.when(i == nb - 1)
    def _():
        @pl.when(nused >= 1)
        def _():
            wait_scatter((nused - 1) % 2)

        @pl.when(nused >= 2)
        def _():
            wait_scatter(nused % 2)


def _moe_experts(xt, block_exp, n_used, src_rows, dst_rows, w_gu, b_gu, w_down, b_down, layer, n_tok):
    nb = block_exp.shape[0]
    rows = MOE_TM * LANE_TILES
    n_out_rows = (TOP_K * n_tok + 2 * MOE_TM) * LANE_TILES
    smem_blk = lambda f: pl.BlockSpec((None, 1, MOE_TM), f, memory_space=pltpu.SMEM)
    grid_spec = pltpu.PrefetchScalarGridSpec(
        num_scalar_prefetch=2,
        grid=(nb,),
        in_specs=[smem_blk(lambda i, be, nu: (i, 0, 0)),
                  smem_blk(lambda i, be, nu: (jnp.minimum(i + 1, nb - 1), 0, 0)),
                  smem_blk(lambda i, be, nu: (i, 0, 0)),
                  pl.BlockSpec(memory_space=pl.ANY),
                  pl.BlockSpec((None, None, D_MODEL, 2 * D_FF), lambda i, be, nu: (layer, be[i], 0, 0)),
                  pl.BlockSpec((None, None, 1, 2 * D_FF), lambda i, be, nu: (layer, be[i], 0, 0)),
                  pl.BlockSpec((None, None, D_FF, D_MODEL), lambda i, be, nu: (layer, be[i], 0, 0)),
                  pl.BlockSpec((None, None, 1, D_MODEL), lambda i, be, nu: (layer, be[i], 0, 0))],
        out_specs=pl.BlockSpec(memory_space=pl.ANY),
        scratch_shapes=[pltpu.VMEM((2, rows, LANES), F32),
                        pltpu.VMEM((2, rows, LANES), F32),
                        pltpu.VMEM((D_MODEL, 2 * D_FF), BF16),
                        pltpu.VMEM((D_FF, D_MODEL), BF16),
                        pltpu.SemaphoreType.DMA((2,)),
                        pltpu.SemaphoreType.DMA((2,))],
    )
    return pl.pallas_call(
        _moe_body,
        grid_spec=grid_spec,
        out_shape=jax.ShapeDtypeStruct((n_out_rows, LANES), F32),
        compiler_params=_params(("arbitrary",)),
        name=f"moe_experts_l{layer}",
    )(block_exp, n_used, src_rows, src_rows, dst_rows, xt, w_gu, b_gu, w_down, b_down)


ROW_TABLE_CHUNK = 4096


def _row_table_body(lo_ref, hi_ref, dest_ref, tbl_ref, *, n_assign):
    i = pl.program_id(0)

    @pl.when(i == 0)
    def _():
        def fill_range(e, c):
            def fill(r, c2):
                tbl_ref[r] = n_assign + jnp.bitwise_and(r, 2 * MOE_TM - 1)
                return c2
            lax.fori_loop(lo_ref[e], hi_ref[e], fill, 0)
            return c
        lax.fori_loop(0, lo_ref.shape[0], fill_range, 0)

    batch = 16

    def body(g, c):
        base = g * batch
        rows = [dest_ref[0, base + q] for q in range(batch)]
        first = i * ROW_TABLE_CHUNK + base
        for q, r in enumerate(rows):
            tbl_ref[r] = first + q
        return c
    lax.fori_loop(0, ROW_TABLE_CHUNK // batch, body, 0)


def _row_table(dest, pad_lo, pad_hi, n_rows, name):
    n_assign = dest.shape[0]
    steps = n_assign // ROW_TABLE_CHUNK
    grid_spec = pltpu.PrefetchScalarGridSpec(
        num_scalar_prefetch=2,
        grid=(steps,),
        in_specs=[pl.BlockSpec((None, 1, ROW_TABLE_CHUNK), lambda i, lo, hi: (i, 0, 0),
                               memory_space=pltpu.SMEM)],
        out_specs=pl.BlockSpec(memory_space=pltpu.SMEM),
    )
    return pl.pallas_call(
        functools.partial(_row_table_body, n_assign=n_assign),
        grid_spec=grid_spec,
        out_shape=jax.ShapeDtypeStruct((n_rows,), I32),
        compiler_params=_params(("arbitrary",)),
        name=name,
    )(pad_lo, pad_hi, dest.reshape(steps, 1, ROW_TABLE_CHUNK))


def _route_tables(route_i, counts, n_tok, name):
    n_assign = n_tok * TOP_K
    nb = n_assign // MOE_TM + N_EXPERTS
    counts = counts.reshape(N_EXPERTS).astype(I32)
    padded = ((counts + MOE_TM - 1) // MOE_TM) * MOE_TM
    pad_end = jnp.cumsum(padded)
    pad_start = pad_end - padded
    eid = route_i[:TOP_K]
    rank = route_i[TOP_K:]
    expert_ids = jnp.arange(N_EXPERTS, dtype=I32)
    start_of = jnp.sum(jnp.where(eid[..., None] == expert_ids, pad_start, 0), axis=-1)
    dest = (start_of + rank).reshape(n_assign)
    pad_lo = jnp.concatenate([pad_start + counts, pad_end[-1:]]).astype(I32)
    pad_hi = jnp.concatenate([pad_end, jnp.full((1,), nb * MOE_TM, I32)]).astype(I32)
    row_assign = _row_table(dest, pad_lo, pad_hi, nb * MOE_TM, name)
    src_rows = jnp.where(row_assign < n_assign, row_assign % n_tok, 0) * LANE_TILES
    dst_rows = row_assign * LANE_TILES
    block_start = jnp.arange(nb, dtype=I32) * MOE_TM
    block_exp = jnp.minimum(jnp.sum((pad_end[None, :] <= block_start[:, None]).astype(I32), axis=1),
                            N_EXPERTS - 1)
    n_used = (pad_end[-1:] // MOE_TM).astype(I32)
    return block_exp, n_used, src_rows.reshape(nb, 1, MOE_TM), dst_rows.reshape(nb, 1, MOE_TM)


def _combine_body(y0_ref, y1_ref, y2_ref, y3_ref, rf_ref, xt_ref, g_ref, b_ref, o_ref, obf_ref):
    tm = o_ref.shape[0]
    rf = rf_ref[...]
    parts = []
    for c in range(LANE_TILES):
        acc = DEEPNORM_ALPHA * xt_ref[pl.ds(c, tm, stride=LANE_TILES), :]
        for k, yk in enumerate((y0_ref, y1_ref, y2_ref, y3_ref)):
            acc = acc + rf[:, k:k + 1] * yk[pl.ds(c, tm, stride=LANE_TILES), :]
        parts.append(acc)
    z = jnp.concatenate(parts, axis=-1)
    out = _layer_norm(z, g_ref[...], b_ref[...])
    o_ref[...] = out
    obf_ref[...] = out.astype(BF16)


def _combine(y_tk, route_f, xt, gain, bias, n_tok, name):
    tm = ROW_TM
    nblk = n_tok // tm
    yspec = lambda k: pl.BlockSpec((tm * LANE_TILES, LANES), lambda i: (k * nblk + i, 0))
    return pl.pallas_call(
        _combine_body,
        grid=(nblk,),
        in_specs=[yspec(0), yspec(1), yspec(2), yspec(3),
                  pl.BlockSpec((tm, LANES), lambda i: (i, 0)),
                  pl.BlockSpec((tm * LANE_TILES, LANES), lambda i: (i, 0)),
                  pl.BlockSpec((1, D_MODEL), lambda i: (0, 0)),
                  pl.BlockSpec((1, D_MODEL), lambda i: (0, 0))],
        out_specs=[pl.BlockSpec((tm, D_MODEL), lambda i: (i, 0)),
                   pl.BlockSpec((tm, D_MODEL), lambda i: (i, 0))],
        out_shape=[jax.ShapeDtypeStruct((n_tok, D_MODEL), F32),
                   jax.ShapeDtypeStruct((n_tok, D_MODEL), BF16)],
        compiler_params=_params(("parallel",)),
        name=name,
    )(y_tk, y_tk, y_tk, y_tk, route_f, xt, gain, bias)


def _gelu_tanh(x):
    return 0.5 * x * (1.0 + jnp.tanh(math.sqrt(2.0 / math.pi) * (x + 0.044715 * (x * x * x))))


LRU_ROWS = 256


def _lru_body(rec_ref, g_ref, cw_ref, cb_ref, wr_ref, br_ref, wi_ref, bi_ref, lam_ref, y_ref,
              ext_ref, a_ref, b_ref):
    seq, ct = rec_ref.shape
    heads = ct // LRU_BLOCK
    ext_ref[0:SUBLANES, :] = jnp.zeros((SUBLANES, ct), F32)
    ext_ref[SUBLANES:, :] = rec_ref[...]
    neg_lam = -lam_ref[...]
    softplus = jnp.maximum(neg_lam, 0.0) + jnp.log1p(jnp.exp(-jnp.abs(neg_lam)))
    wr = [wr_ref[h].astype(BF16) for h in range(heads)]
    wi = [wi_ref[h].astype(BF16) for h in range(heads)]

    for c in range(seq // LRU_ROWS):
        r0 = c * LRU_ROWS
        conv = cb_ref[...] + cw_ref[CONV_WIDTH - 1:CONV_WIDTH, :] * rec_ref[pl.ds(r0, LRU_ROWS), :]
        for j in range(CONV_WIDTH - 1):
            conv = conv + cw_ref[j:j + 1, :] * ext_ref[pl.ds(r0 + SUBLANES - (CONV_WIDTH - 1) + j, LRU_ROWS), :]
        xb = conv.astype(BF16)
        r_parts = [jnp.dot(xb[:, h * LRU_BLOCK:(h + 1) * LRU_BLOCK], wr[h], preferred_element_type=F32)
                   for h in range(heads)]
        i_parts = [jnp.dot(xb[:, h * LRU_BLOCK:(h + 1) * LRU_BLOCK], wi[h], preferred_element_type=F32)
                   for h in range(heads)]
        r = jax.nn.sigmoid(jnp.concatenate(r_parts, axis=-1) + br_ref[...])
        ig = jax.nn.sigmoid(jnp.concatenate(i_parts, axis=-1) + bi_ref[...])
        log_a = -LRU_C * r * softplus
        a = jnp.exp(log_a)
        a_ref[pl.ds(r0, LRU_ROWS), :] = a
        b_ref[pl.ds(r0, LRU_ROWS), :] = jnp.sqrt(-jnp.tanh(log_a) * (a * a + 1.0)) * (ig * conv)

    _lru_scan_body(a_ref, b_ref, g_ref, y_ref)


def _lru_scan_body(a_ref, b_ref, g_ref, y_ref):
    seq, ct = a_ref.shape
    row = lax.broadcasted_iota(I32, (SUBLANES, ct), 0)

    def step(gi, carry):
        off = pl.multiple_of(gi * SUBLANES, SUBLANES)
        a = a_ref[pl.ds(off, SUBLANES), :]
        b = b_ref[pl.ds(off, SUBLANES), :]
        for d in (1, 2, 4):
            keep = row >= d
            b = jnp.where(keep, a * pltpu.roll(b, d, 0) + b, b)
            a = jnp.where(keep, a * pltpu.roll(a, d, 0), a)
        h = a * carry + b
        y_ref[pl.ds(off, SUBLANES), :] = _gelu_tanh(g_ref[pl.ds(off, SUBLANES), :]) * h
        return jnp.broadcast_to(h[SUBLANES - 1:SUBLANES, :], (SUBLANES, ct))

    lax.fori_loop(0, seq // SUBLANES, step, jnp.zeros((SUBLANES, ct), F32), unroll=4)


def _rglru_mixer(x2, bsz, seq, idx, a_w_in, conv_w, conv_b, w_rgate, b_rgate, w_igate, b_igate, lam, layer):
    t = bsz * seq
    w = D_MODEL
    u = _matmul(x2, a_w_in, idx, 0, 2 * w, F32, f"lru_in_l{layer}")
    ct = 2 * LRU_BLOCK
    nct = w // ct
    hpt = ct // LRU_BLOCK
    vec = lambda: pl.BlockSpec((1, ct), lambda b, j: (0, j))
    return pl.pallas_call(
        _lru_body,
        grid=(bsz, nct),
        in_specs=[pl.BlockSpec((seq, ct), lambda b, j: (b, nct + j)),
                  pl.BlockSpec((seq, ct), lambda b, j: (b, j)),
                  pl.BlockSpec((CONV_WIDTH, ct), lambda b, j: (0, j)),
                  vec(),
                  pl.BlockSpec((hpt, LRU_BLOCK, LRU_BLOCK), lambda b, j: (j, 0, 0)),
                  vec(),
                  pl.BlockSpec((hpt, LRU_BLOCK, LRU_BLOCK), lambda b, j: (j, 0, 0)),
                  vec(), vec()],
        out_specs=pl.BlockSpec((seq, ct), lambda b, j: (b, j)),
        out_shape=jax.ShapeDtypeStruct((t, w), F32),
        scratch_shapes=[pltpu.VMEM((seq + SUBLANES, ct), F32),
                        pltpu.VMEM((seq, ct), F32),
                        pltpu.VMEM((seq, ct), F32)],
        compiler_params=_params(("parallel", "parallel")),
        name=f"lru_l{layer}",
    )(u, u, conv_w[idx], conv_b[idx][None], w_rgate[idx], b_rgate[idx][None],
      w_igate[idx], b_igate[idx][None], lam[idx][None])


def _ret_body(lg_ref, q_ref, k_ref, v_ref, g_ref, y_ref, state_ref):
    c = pl.program_id(1)
    cs = RET_CHUNK
    heads = range(RET_HEADS)

    @pl.when(c == 0)
    def _():
        state_ref[...] = jnp.zeros_like(state_ref)

    qi = lax.broadcasted_iota(I32, (cs, cs), 0)
    ki = lax.broadcasted_iota(I32, (cs, cs), 1)
    rel = (qi - ki).astype(F32)
    pos_col = lax.broadcasted_iota(I32, (cs, 1), 0).astype(F32)
    lgs = [lg_ref[h] for h in heads]
    intra_decay = [jnp.where(rel >= 0, jnp.exp(lg * jnp.maximum(rel, 0.0)), 0.0) for lg in lgs]
    query_decay = [jnp.exp(lg * (pos_col + 1.0)) for lg in lgs]
    key_decay = [jnp.exp(lg * (cs - 1.0 - pos_col)) for lg in lgs]
    chunk_decay = [jnp.exp(jnp.full((1, 1), lg * cs, F32)) for lg in lgs]

    scale = RET_QK ** -0.5
    dn_nt = (((1,), (1,)), ((), ()))
    dn_tn = (((0,), (0,)), ((), ()))
    qs = [q_ref[:, h * RET_QK:(h + 1) * RET_QK] for h in heads]
    ks = [k_ref[:, h * RET_QK:(h + 1) * RET_QK] for h in heads]
    vs = [v_ref[:, h * RET_V:(h + 1) * RET_V] for h in heads]
    intra = [lax.dot_general(q, k, dn_nt, preferred_element_type=F32) * scale * d
             for q, k, d in zip(qs, ks, intra_decay)]
    states = [state_ref[h] for h in heads]
    cross = [jnp.dot(q, s.astype(BF16), preferred_element_type=F32) * d
             for q, s, d in zip(qs, states, query_decay)]
    outs = [jnp.dot(a.astype(BF16), v, preferred_element_type=F32) + x for a, v, x in zip(intra, vs, cross)]
    kds = [(k.astype(F32) * (scale * d)).astype(BF16) for k, d in zip(ks, key_decay)]
    upd = [lax.dot_general(kd, v, dn_tn, preferred_element_type=F32) for kd, v in zip(kds, vs)]
    for h in heads:
        state_ref[h] = chunk_decay[h] * states[h] + upd[h]
    for h in heads:
        out = outs[h]
        o = out * lax.rsqrt(jnp.mean(out * out, axis=-1, keepdims=True) + RET_EPS)
        g = g_ref[:, h * RET_V:(h + 1) * RET_V]
        y_ref[:, h * RET_V:(h + 1) * RET_V] = (g * jax.nn.sigmoid(g) * o).astype(y_ref.dtype)


def _retention_mixer(x2, bsz, seq, idx, b_w_in, layer):
    t = bsz * seq
    qk_w = RET_HEADS * RET_QK
    v_w = RET_HEADS * RET_V
    qkv = _matmul(x2, b_w_in, idx, 0, 2 * qk_w + v_w, BF16, f"ret_qkv_l{layer}")
    gate = _matmul(x2, b_w_in, idx, 2 * qk_w + v_w, v_w, F32, f"ret_gate_l{layer}")
    nc = seq // RET_CHUNK
    log_gamma = jnp.log1p(-jnp.exp2(-5.0 - jnp.arange(RET_HEADS, dtype=F32)))
    grid_spec = pltpu.PrefetchScalarGridSpec(
        num_scalar_prefetch=1,
        grid=(bsz, nc),
        in_specs=[pl.BlockSpec((RET_CHUNK, qk_w), lambda b, c, lg: (b * nc + c, 0)),
                  pl.BlockSpec((RET_CHUNK, qk_w), lambda b, c, lg: (b * nc + c, 1)),
                  pl.BlockSpec((RET_CHUNK, v_w), lambda b, c, lg: (b * nc + c, 1)),
                  pl.BlockSpec((RET_CHUNK, v_w), lambda b, c, lg: (b * nc + c, 0))],
        out_specs=pl.BlockSpec((RET_CHUNK, v_w), lambda b, c, lg: (b * nc + c, 0)),
        scratch_shapes=[pltpu.VMEM((RET_HEADS, RET_QK, RET_V), F32)],
    )
    return pl.pallas_call(
        _ret_body,
        grid_spec=grid_spec,
        out_shape=jax.ShapeDtypeStruct((t, v_w), BF16),
        compiler_params=_params(("parallel", "arbitrary")),
        name=f"retention_l{layer}",
    )(log_gamma, qkv, qkv, qkv, gate)


def _attn_body(*refs, dilation, has_prev):
    if has_prev:
        q_ref, kc_ref, kp_ref, vc_ref, vp_ref, o_ref, st_ref = refs
    else:
        q_ref, kc_ref, vc_ref, o_ref, st_ref = refs
    n = pl.program_id(2)
    nw = q_ref.shape[0]
    dh = ATTN_HEAD_DIM
    scale = dh ** -0.5
    qi = lax.broadcasted_iota(I32, (nw, nw), 0)
    ki = lax.broadcasted_iota(I32, (nw, nw), 1)
    steps_c = qi - ki
    valid_c = steps_c >= 0
    dist_c = (steps_c * dilation).astype(F32)
    q = q_ref[...]
    kc = kc_ref[...]
    vc = vc_ref[...]
    lane = lax.broadcasted_iota(I32, (nw, LANES), 1)
    dn = (((1,), (1,)), ((), ()))
    heads = range(ATTN_HEADS)
    sls = [slice(h * dh, (h + 1) * dh) for h in heads]
    slopes = [2.0 ** (-8.0 * (h + 1) / ATTN_HEADS) for h in heads]
    sc_c = [lax.dot_general(q[:, sl], kc[:, sl], dn, preferred_element_type=F32) for sl in sls]
    sc_c = [jnp.where(valid_c, s * scale - sp * dist_c, -jnp.inf) for s, sp in zip(sc_c, slopes)]
    if not has_prev:
        ms = [jnp.max(a, axis=-1, keepdims=True) for a in sc_c]
    if has_prev:
        steps_p = steps_c + nw
        valid_p = steps_p <= jnp.where(n > 0, nw, -1)
        dist_p = (steps_p * dilation).astype(F32)
        kp = kp_ref[...]
        vp = vp_ref[...]
        sc_p = [lax.dot_general(q[:, sl], kp[:, sl], dn, preferred_element_type=F32) for sl in sls]
        sc_p = [jnp.where(valid_p, s * scale - sp * dist_p, -jnp.inf) for s, sp in zip(sc_p, slopes)]
        ms = [jnp.max(jnp.maximum(a, b), axis=-1, keepdims=True) for a, b in zip(sc_c, sc_p)]
    pc = [jnp.exp(a - m) for a, m in zip(sc_c, ms)]
    outs = [jnp.dot(a.astype(BF16), vc[:, sl], preferred_element_type=F32) for a, sl in zip(pc, sls)]
    if has_prev:
        pp = [jnp.exp(b - m) for b, m in zip(sc_p, ms)]
        sums = [jnp.sum(a + b, axis=-1, keepdims=True) for a, b in zip(pc, pp)]
        outs = [o + jnp.dot(b.astype(BF16), vp[:, sl], preferred_element_type=F32)
                for o, b, sl in zip(outs, pp, sls)]
    else:
        sums = [jnp.sum(a, axis=-1, keepdims=True) for a in pc]
    stats = jnp.zeros((nw, LANES), F32)
    for h, sl, o, m, sm in zip(heads, sls, outs, ms, sums):
        o_ref[:, sl] = o.astype(o_ref.dtype)
        stats = jnp.where(lane == h, m, stats)
        stats = jnp.where(lane == ATTN_HEADS + h, sm, stats)
    st_ref[...] = stats


def _merge_body(o0_ref, o1_ref, o2_ref, s0_ref, s1_ref, s2_ref, y_ref):
    tm = y_ref.shape[0]
    stats = [s0_ref[...], s1_ref[...], s2_ref[...]]
    lane = lax.broadcasted_iota(I32, (tm, LANES), 1)
    maxes = [s for s in stats]
    sums = [pltpu.roll(s, LANES - ATTN_HEADS, 1) for s in stats]
    m_all = jnp.maximum(jnp.maximum(maxes[0], maxes[1]), maxes[2])
    wgt = [jnp.exp(m - m_all) for m in maxes]
    den = wgt[0] * sums[0] + wgt[1] * sums[1] + wgt[2] * sums[2]
    er = lax.broadcasted_iota(I32, (LANES, D_MODEL), 0)
    ec = lax.broadcasted_iota(I32, (LANES, D_MODEL), 1)
    expand = (ec // ATTN_HEAD_DIM == er).astype(BF16)
    acc = jnp.zeros((tm, D_MODEL), F32)
    for w, o_ref in zip(wgt, (o0_ref, o1_ref, o2_ref)):
        coef = jnp.where(lane < ATTN_HEADS, w / den, 0.0)
        hi = coef.astype(BF16)
        lo = (coef - hi.astype(F32)).astype(BF16)
        spread = (jnp.dot(hi, expand, preferred_element_type=F32)
                  + jnp.dot(lo, expand, preferred_element_type=F32))
        acc = acc + spread * o_ref[...].astype(F32)
    y_ref[...] = acc.astype(y_ref.dtype)


def _attention_mixer(x2, bsz, seq, idx, c_w_in, layer):
    t = bsz * seq
    w = D_MODEL
    n_groups = len(ATTN_PATTERNS)
    outs, stats = [], []
    for g, (window, dilation) in enumerate(ATTN_PATTERNS):
        nw = window // dilation
        sub = seq // dilation
        assert nw == 128 and sub % nw == 0
        nb = sub // nw

        def to_sub(a):
            return a.reshape(bsz, sub, dilation, -1).transpose(0, 2, 1, 3).reshape(t, -1)

        def from_sub(a):
            return a.reshape(bsz, dilation, sub, -1).transpose(0, 2, 1, 3).reshape(t, -1)

        xs = x2 if dilation == 1 else to_sub(x2)
        qkv = _matmul(xs, c_w_in, idx, g * w, 3 * w, BF16, f"attn_qkv_g{g}_l{layer}",
                      col_block_stride=n_groups)
        cur = lambda slab: pl.BlockSpec((nw, w), lambda b, r, n, slab=slab: ((b * dilation + r) * nb + n, slab))
        prev = lambda slab: pl.BlockSpec(
            (nw, w), lambda b, r, n, slab=slab: ((b * dilation + r) * nb + jnp.maximum(n - 1, 0), slab))
        has_prev = nb > 1
        in_specs = [cur(0), cur(1), prev(1), cur(2), prev(2)] if has_prev else [cur(0), cur(1), cur(2)]
        o, st = pl.pallas_call(
            functools.partial(_attn_body, dilation=dilation, has_prev=has_prev),
            grid=(bsz, dilation, nb),
            in_specs=in_specs,
            out_specs=[pl.BlockSpec((nw, w), lambda b, r, n: ((b * dilation + r) * nb + n, 0)),
                       pl.BlockSpec((nw, LANES), lambda b, r, n: ((b * dilation + r) * nb + n, 0))],
            out_shape=[jax.ShapeDtypeStruct((t, w), BF16), jax.ShapeDtypeStruct((t, LANES), F32)],
            compiler_params=_params(("parallel", "parallel", "parallel")),
            name=f"attn_g{g}_l{layer}",
        )(*([qkv] * len(in_specs)))
        outs.append(o if dilation == 1 else from_sub(o))
        stats.append(st if dilation == 1 else from_sub(st))
    tm = ROW_TM
    return pl.pallas_call(
        _merge_body,
        grid=(t // tm,),
        in_specs=[pl.BlockSpec((tm, w), lambda i: (i, 0))] * 3 + [pl.BlockSpec((tm, LANES), lambda i: (i, 0))] * 3,
        out_specs=pl.BlockSpec((tm, w), lambda i: (i, 0)),
        out_shape=jax.ShapeDtypeStruct((t, w), BF16),
        compiler_params=_params(("parallel",)),
        name=f"attn_merge_l{layer}",
    )(*outs, *stats)


def kernel(x, a_w_in, a_conv_w, a_conv_b, a_w_rgate, a_b_rgate, a_w_igate, a_b_igate, a_lambda, a_w_out,
           b_w_in, b_w_out, c_w_in, c_w_out, ln_gain, ln_bias, moe_w_router, moe_b_router, moe_w_gu,
           moe_b_gu, moe_w_down, moe_b_down):
    bsz, seq, d = x.shape
    t = bsz * seq
    x2 = x.reshape(t, d)
    x_mxu = x2
    b_gu4 = moe_b_gu[:, :, None, :]
    b_down4 = moe_b_down[:, :, None, :]
    for layer in range(DEPTH):
        kind, idx = layer % N_MIXERS, layer // N_MIXERS
        if kind == 0:
            y = _rglru_mixer(x_mxu, bsz, seq, idx, a_w_in, a_conv_w, a_conv_b, a_w_rgate, a_b_rgate,
                             a_w_igate, a_b_igate, a_lambda, layer)
            w_out = a_w_out
        elif kind == 1:
            y = _retention_mixer(x_mxu, bsz, seq, idx, b_w_in, layer)
            w_out = b_w_out
        else:
            y = _attention_mixer(x_mxu, bsz, seq, idx, c_w_in, layer)
            w_out = c_w_out
        xt, route_f, route_i, counts = _post_mixer(
            y, w_out, idx, x2, ln_gain[layer, 0][None], ln_bias[layer, 0][None],
            moe_w_router[layer], moe_b_router[layer][None], f"post_mixer_l{layer}")
        block_exp, n_used, src_rows, dst_rows = _route_tables(route_i, counts, t, f"row_table_l{layer}")
        y_tk = _moe_experts(xt, block_exp, n_used, src_rows, dst_rows,
                            moe_w_gu, b_gu4, moe_w_down, b_down4, layer, t)
        x2, x_mxu = _combine(y_tk, route_f, xt, ln_gain[layer, 1][None], ln_bias[layer, 1][None], t,
                             f"combine_l{layer}")
    return x2.reshape(bsz, seq, d)
```

```python
import functools
import math

import jax
import jax.numpy as jnp
from jax import lax
from jax.experimental import pallas as pl
from jax.experimental.pallas import tpu as pltpu

F32 = jnp.float32
BF16 = jnp.bfloat16
I32 = jnp.int32

D_MODEL = 1024
DEPTH = 4
N_MIXERS = 3
DEEPNORM_ALPHA = (2.0 * DEPTH) ** 0.25
LN_EPS = 1e-5
LRU_HEADS = 8
LRU_BLOCK = D_MODEL // LRU_HEADS
CONV_WIDTH = 4
LRU_C = 8.0
RET_HEADS = 4
RET_QK = D_MODEL // RET_HEADS
RET_V = 2 * RET_QK
RET_CHUNK = 128
RET_EPS = 1e-6
ATTN_PATTERNS = ((128, 1), (512, 4), (2048, 16))
ATTN_HEADS = 16
ATTN_HEAD_DIM = D_MODEL // ATTN_HEADS
N_EXPERTS = 32
TOP_K = 4
D_FF = D_MODEL
SWIGLU_LIMIT = 7.0
SWIGLU_ALPHA = 1.702

SUBLANES = 8
LANES = 128
LANE_TILES = D_MODEL // LANES
VMEM_LIMIT = 56 * 1024 * 1024

MOE_TM = 256
ROW_TM = 256


def _params(sem, vmem=VMEM_LIMIT):
    return pltpu.CompilerParams(dimension_semantics=sem, vmem_limit_bytes=vmem)


def _mm_body(x_ref, w_ref, o_ref, wbf_ref):
    @pl.when(pl.program_id(1) == 0)
    def _():
        wbf_ref[...] = w_ref[...].astype(BF16)

    o_ref[...] = jnp.dot(x_ref[...].astype(BF16), wbf_ref[...],
                         preferred_element_type=F32).astype(o_ref.dtype)


def _matmul(x, w_stack, idx, col0, ncols, out_dtype, name, tm=1024, tn=1024, col_block_stride=1):
    t, k = x.shape
    assert col0 % tn == 0 and ncols % tn == 0 and t % tm == 0
    return pl.pallas_call(
        _mm_body,
        grid=(ncols // tn, t // tm),
        in_specs=[pl.BlockSpec((tm, k), lambda n, m: (m, 0)),
                  pl.BlockSpec((None, k, tn), lambda n, m: (idx, 0, col0 // tn + n * col_block_stride))],
        out_specs=pl.BlockSpec((tm, tn), lambda n, m: (m, n)),
        out_shape=jax.ShapeDtypeStruct((t, ncols), out_dtype),
        scratch_shapes=[pltpu.VMEM((k, tn), BF16)],
        compiler_params=_params(("arbitrary", "arbitrary")),
        name=name,
    )(x, w_stack)


def _layer_norm(z, gain, bias):
    mu = jnp.mean(z, axis=-1, keepdims=True)
    zc = z - mu
    var = jnp.mean(zc * zc, axis=-1, keepdims=True)
    return zc * lax.rsqrt(var + LN_EPS) * gain + bias


def _store_token_tiles(ref, val):
    tm = val.shape[0]
    for c in range(LANE_TILES):
        ref[pl.ds(c, tm, stride=LANE_TILES), :] = val[:, c * LANES:(c + 1) * LANES]


def _load_token_tiles(ref, tm):
    return jnp.concatenate([ref[pl.ds(c, tm, stride=LANE_TILES), :] for c in range(LANE_TILES)], axis=-1)


def _post_body(y_ref, w_ref, x_ref, g_ref, b_ref, wr_ref, br_ref,
               xt_ref, rf_ref, ri_ref, cnt_ref, wbf_ref, wrh_ref, wrl_ref, run_ref):
    i = pl.program_id(0)
    tm = x_ref.shape[0]

    @pl.when(i == 0)
    def _():
        wbf_ref[...] = w_ref[...].astype(BF16)
        wr = wr_ref[...]
        wrh = wr.astype(BF16)
        wrh_ref[...] = wrh
        wrl_ref[...] = (wr - wrh.astype(F32)).astype(BF16)
        run_ref[...] = jnp.zeros_like(run_ref)

    z = DEEPNORM_ALPHA * x_ref[...] + jnp.dot(y_ref[...].astype(BF16), wbf_ref[...],
                                              preferred_element_type=F32)
    xn = _layer_norm(z, g_ref[...], b_ref[...])
    _store_token_tiles(xt_ref, xn)

    xh = xn.astype(BF16)
    xl = (xn - xh.astype(F32)).astype(BF16)
    logits_tm = (jnp.dot(xh, wrh_ref[...], preferred_element_type=F32)
                 + jnp.dot(xl, wrh_ref[...], preferred_element_type=F32)
                 + jnp.dot(xh, wrl_ref[...], preferred_element_type=F32) + br_ref[...])
    logits = jnp.transpose(logits_tm)[:N_EXPERTS, :]
    row_e = lax.broadcasted_iota(I32, (N_EXPERTS, tm), 0).astype(F32)
    work = logits
    top_l, top_e, onehots = [], [], []
    for _ in range(TOP_K):
        m = jnp.max(work, axis=0, keepdims=True)
        idx = jnp.min(jnp.where(work == m, row_e, float(N_EXPERTS)), axis=0, keepdims=True)
        oh = row_e == idx
        top_l.append(m)
        top_e.append(idx)
        onehots.append(oh)
        work = jnp.where(oh, -jnp.inf, work)
    exps = [jnp.exp(l - top_l[0]) for l in top_l]
    denom = exps[0] + exps[1] + exps[2] + exps[3]
    gates = [e / denom for e in exps]

    sel = jnp.zeros((N_EXPERTS, tm), F32)
    for oh in onehots:
        sel = sel + oh.astype(F32)
    ri = lax.broadcasted_iota(I32, (tm, tm), 0)
    ci = lax.broadcasted_iota(I32, (tm, tm), 1)
    earlier = (ri < ci).astype(BF16)
    before = jnp.dot(sel.astype(BF16), earlier, preferred_element_type=F32) + run_ref[...]
    ranks = [jnp.sum(jnp.where(oh, before, 0.0), axis=0, keepdims=True) for oh in onehots]
    run_ref[...] = run_ref[...] + jnp.sum(sel, axis=1, keepdims=True)
    cnt_ref[...] = run_ref[...]

    row8 = lax.broadcasted_iota(I32, (2 * TOP_K, tm), 0)
    gate_rows = jnp.zeros((2 * TOP_K, tm), F32)
    rint = jnp.zeros((2 * TOP_K, tm), I32)
    for k in range(TOP_K):
        gate_rows = jnp.where(row8 == k, gates[k], gate_rows)
        rint = jnp.where(row8 == k, top_e[k].astype(I32), rint)
        rint = jnp.where(row8 == TOP_K + k, ranks[k].astype(I32), rint)
    ri_ref[...] = rint
    pick = (lax.broadcasted_iota(I32, (2 * TOP_K, LANES), 0)
            == lax.broadcasted_iota(I32, (2 * TOP_K, LANES), 1)).astype(F32)
    rf_ref[...] = lax.dot_general(gate_rows, pick, (((0,), (0,)), ((), ())), preferred_element_type=F32,
                                  precision=lax.Precision.HIGHEST)


def _post_mixer(y, w_out_stack, idx, x, gain, bias, w_router, b_router, name):
    t, kd = y.shape
    tm = ROW_TM
    return pl.pallas_call(
        _post_body,
        grid=(t // tm,),
        in_specs=[pl.BlockSpec((tm, kd), lambda i: (i, 0)),
                  pl.BlockSpec((None, kd, D_MODEL), lambda i: (idx, 0, 0)),
                  pl.BlockSpec((tm, D_MODEL), lambda i: (i, 0)),
                  pl.BlockSpec((1, D_MODEL), lambda i: (0, 0)),
                  pl.BlockSpec((1, D_MODEL), lambda i: (0, 0)),
                  pl.BlockSpec((D_MODEL, LANES), lambda i: (0, 0)),
                  pl.BlockSpec((1, LANES), lambda i: (0, 0))],
        out_specs=[pl.BlockSpec((tm * LANE_TILES, LANES), lambda i: (i, 0)),
                   pl.BlockSpec((tm, LANES), lambda i: (i, 0)),
                   pl.BlockSpec((2 * TOP_K, tm), lambda i: (0, i)),
                   pl.BlockSpec((N_EXPERTS, 1), lambda i: (0, 0))],
        out_shape=[jax.ShapeDtypeStruct((t * LANE_TILES, LANES), F32),
                   jax.ShapeDtypeStruct((t, LANES), F32),
                   jax.ShapeDtypeStruct((2 * TOP_K, t), I32),
                   jax.ShapeDtypeStruct((N_EXPERTS, 1), F32)],
        scratch_shapes=[pltpu.VMEM((kd, D_MODEL), BF16), pltpu.VMEM((D_MODEL, LANES), BF16),
                        pltpu.VMEM((D_MODEL, LANES), BF16), pltpu.VMEM((N_EXPERTS, 1), F32)],
        compiler_params=_params(("arbitrary",)),
        name=name,
    )(y, w_out_stack, x, gain, bias,
      jnp.pad(w_router, ((0, 0), (0, LANES - N_EXPERTS))),
      jnp.pad(b_router.reshape(1, N_EXPERTS), ((0, 0), (0, LANES - N_EXPERTS))))


def _gather_copy(x_hbm, xbuf, sem, src_row, r):
    return pltpu.make_async_copy(
        x_hbm.at[pl.ds(pl.multiple_of(src_row, SUBLANES), SUBLANES), :],
        xbuf.at[pl.ds(pl.multiple_of(r * SUBLANES, SUBLANES), SUBLANES), :],
        sem)


def _scatter_copy(ybuf, y_hbm, sem, dst_row, r):
    return pltpu.make_async_copy(
        ybuf.at[pl.ds(pl.multiple_of(r * SUBLANES, SUBLANES), SUBLANES), :],
        y_hbm.at[pl.ds(pl.multiple_of(dst_row, SUBLANES), SUBLANES), :],
        sem)


def _moe_body(be_ref, nu_ref, src_ref, srcn_ref, dst_ref, x_hbm, wgu_ref, bgu_ref, wd_ref, bd_ref,
              y_hbm, xbuf, ybuf, wgu_bf, wd_bf, gsem, ssem):
    i = pl.program_id(0)
    nb = pl.num_programs(0)
    nused = nu_ref[0]
    slot = i % 2
    rows = MOE_TM * LANE_TILES
    group = 32

    def gather_all(idx_ref, s):
        def body(g, c):
            for q in range(group):
                r = g * group + q
                _gather_copy(x_hbm, xbuf.at[s], gsem.at[s], idx_ref[0, r], r).start(priority=q % 2)
            return c
        lax.fori_loop(0, MOE_TM // group, body, 0)

    def scatter_all(idx_ref, s):
        def body(g, c):
            for q in range(group):
                r = g * group + q
                _scatter_copy(ybuf.at[s], y_hbm, ssem.at[s], idx_ref[0, r], r).start(priority=q % 2)
            return c
        lax.fori_loop(0, MOE_TM // group, body, 0)

    def wait_scatter(s):
        pltpu.make_async_copy(ybuf.at[s], y_hbm.at[pl.ds(0, rows), :], ssem.at[s]).wait()

    @pl.when(i == 0)
    def _():
        ybuf[...] = jnp.zeros_like(ybuf)
        tail = y_hbm.at[pl.ds(y_hbm.shape[0] - 2 * rows, 2 * rows), :]
        for s in range(2):
            cp = pltpu.make_async_copy(ybuf.at[s], tail.at[pl.ds(s * rows, rows), :], ssem.at[s])
            cp.start()
            cp.wait()

    @pl.when(i < nused)
    def _():
        @pl.when(i == 0)
        def _():
            gather_all(src_ref, 0)

        @pl.when(i + 1 < nused)
        def _():
            gather_all(srcn_ref, 1 - slot)

        changed = jnp.logical_or(i == 0, be_ref[i] != be_ref[jnp.maximum(i - 1, 0)])

        @pl.when(changed)
        def _():
            wgu_bf[...] = wgu_ref[...].astype(BF16)
            wd_bf[...] = wd_ref[...].astype(BF16)

        pltpu.make_async_copy(x_hbm.at[pl.ds(0, rows), :], xbuf.at[slot], gsem.at[slot]).wait()
        x = _load_token_tiles(xbuf.at[slot], MOE_TM).astype(BF16)
        gu = jnp.dot(x, wgu_bf[...], preferred_element_type=F32) + bgu_ref[...]
        gate = jnp.minimum(gu[:, :D_FF], SWIGLU_LIMIT)
        up = jnp.clip(gu[:, D_FF:], -SWIGLU_LIMIT, SWIGLU_LIMIT)
        hidden = (up + 1.0) * gate * jax.nn.sigmoid(SWIGLU_ALPHA * gate)
        y = jnp.dot(hidden.astype(BF16), wd_bf[...], preferred_element_type=F32) + bd_ref[...]

        @pl.when(i >= 2)
        def _():
            wait_scatter(slot)

        _store_token_tiles(ybuf.at[slot], y)
        scatter_all(dst_ref, slot)

    @pl.when(i == nb - 1)
    def _():
        @pl.when(nused >= 1)
        def _():
            wait_scatter((nused - 1) % 2)

        @pl.when(nused >= 2)
        def _():
            wait_scatter(nused % 2)


def _moe_experts(xt, block_exp, n_used, src_rows, dst_rows, w_gu, b_gu, w_down, b_down, layer, n_tok):
    nb = block_exp.shape[0]
    rows = MOE_TM * LANE_TILES
    n_out_rows = (TOP_K * n_tok + 2 * MOE_TM) * LANE_TILES
    smem_blk = lambda f: pl.BlockSpec((None, 1, MOE_TM), f, memory_space=pltpu.SMEM)
    grid_spec = pltpu.PrefetchScalarGridSpec(
        num_scalar_prefetch=2,
        grid=(nb,),
        in_specs=[smem_blk(lambda i, be, nu: (i, 0, 0)),
                  smem_blk(lambda i, be, nu: (jnp.minimum(i + 1, nb - 1), 0, 0)),
                  smem_blk(lambda i, be, nu: (i, 0, 0)),
                  pl.BlockSpec(memory_space=pl.ANY),
                  pl.BlockSpec((None, None, D_MODEL, 2 * D_FF), lambda i, be, nu: (layer, be[i], 0, 0)),
                  pl.BlockSpec((None, None, 1, 2 * D_FF), lambda i, be, nu: (layer, be[i], 0, 0)),
                  pl.BlockSpec((None, None, D_FF, D_MODEL), lambda i, be, nu: (layer, be[i], 0, 0)),
                  pl.BlockSpec((None, None, 1, D_MODEL), lambda i, be, nu: (layer, be[i], 0, 0))],
        out_specs=pl.BlockSpec(memory_space=pl.ANY),
        scratch_shapes=[pltpu.VMEM((2, rows, LANES), F32),
                        pltpu.VMEM((2, rows, LANES), F32),
                        pltpu.VMEM((D_MODEL, 2 * D_FF), BF16),
                        pltpu.VMEM((D_FF, D_MODEL), BF16),
                        pltpu.SemaphoreType.DMA((2,)),
                        pltpu.SemaphoreType.DMA((2,))],
    )
    return pl.pallas_call(
        _moe_body,
        grid_spec=grid_spec,
        out_shape=jax.ShapeDtypeStruct((n_out_rows, LANES), F32),
        compiler_params=_params(("arbitrary",)),
        name=f"moe_experts_l{layer}",
    )(block_exp, n_used, src_rows, src_rows, dst_rows, xt, w_gu, b_gu, w_down, b_down)


def _route_tables(route_i, counts, n_tok):
    n_assign = n_tok * TOP_K
    nb = n_assign // MOE_TM + N_EXPERTS
    counts = counts.reshape(N_EXPERTS).astype(I32)
    padded = ((counts + MOE_TM - 1) // MOE_TM) * MOE_TM
    pad_end = jnp.cumsum(padded)
    pad_start = pad_end - padded
    eid = route_i[:TOP_K]
    rank = route_i[TOP_K:]
    expert_ids = jnp.arange(N_EXPERTS, dtype=I32)
    start_of = jnp.sum(jnp.where(eid[..., None] == expert_ids, pad_start, 0), axis=-1)
    dest = (start_of + rank).reshape(n_assign)
    pad_lo = jnp.concatenate([pad_start + counts, pad_end[-1:]]).astype(I32)
    pad_hi = jnp.concatenate([pad_end, jnp.full((1,), nb * MOE_TM, I32)]).astype(I32)
    n_rows = nb * MOE_TM
    pad_cnt = pad_hi - pad_lo
    pad_cum = jnp.cumsum(pad_cnt)
    pad_i = jnp.arange(n_rows - n_assign, dtype=I32)
    range_id = jnp.sum((pad_cum[None, :] <= pad_i[:, None]).astype(I32), axis=1)
    range_ids = jnp.arange(pad_lo.shape[0], dtype=I32)
    offset = jnp.sum(jnp.where(range_id[:, None] == range_ids, pad_lo - (pad_cum - pad_cnt), 0), axis=1)
    pad_rows = pad_i + offset
    keys = jnp.concatenate([dest, pad_rows])
    vals = jnp.concatenate([jnp.arange(n_assign, dtype=I32), n_assign + pad_rows % (2 * MOE_TM)])
    _, row_assign = lax.sort((keys, vals), num_keys=1)
    src_rows = jnp.where(row_assign < n_assign, row_assign % n_tok, 0) * LANE_TILES
    dst_rows = row_assign * LANE_TILES
    block_start = jnp.arange(nb, dtype=I32) * MOE_TM
    block_exp = jnp.minimum(jnp.sum((pad_end[None, :] <= block_start[:, None]).astype(I32), axis=1),
                            N_EXPERTS - 1)
    n_used = (pad_end[-1:] // MOE_TM).astype(I32)
    return block_exp, n_used, src_rows.reshape(nb, 1, MOE_TM), dst_rows.reshape(nb, 1, MOE_TM)


def _combine_body(y0_ref, y1_ref, y2_ref, y3_ref, rf_ref, xt_ref, g_ref, b_ref, o_ref, obf_ref):
    tm = o_ref.shape[0]
    rf = rf_ref[...]
    parts = []
    for c in range(LANE_TILES):
        acc = DEEPNORM_ALPHA * xt_ref[pl.ds(c, tm, stride=LANE_TILES), :]
        for k, yk in enumerate((y0_ref, y1_ref, y2_ref, y3_ref)):
            acc = acc + rf[:, k:k + 1] * yk[pl.ds(c, tm, stride=LANE_TILES), :]
        parts.append(acc)
    z = jnp.concatenate(parts, axis=-1)
    out = _layer_norm(z, g_ref[...], b_ref[...])
    o_ref[...] = out
    obf_ref[...] = out.astype(BF16)


def _combine(y_tk, route_f, xt, gain, bias, n_tok, name):
    tm = ROW_TM
    nblk = n_tok // tm
    yspec = lambda k: pl.BlockSpec((tm * LANE_TILES, LANES), lambda i: (k * nblk + i, 0))
    return pl.pallas_call(
        _combine_body,
        grid=(nblk,),
        in_specs=[yspec(0), yspec(1), yspec(2), yspec(3),
                  pl.BlockSpec((tm, LANES), lambda i: (i, 0)),
                  pl.BlockSpec((tm * LANE_TILES, LANES), lambda i: (i, 0)),
                  pl.BlockSpec((1, D_MODEL), lambda i: (0, 0)),
                  pl.BlockSpec((1, D_MODEL), lambda i: (0, 0))],
        out_specs=[pl.BlockSpec((tm, D_MODEL), lambda i: (i, 0)),
                   pl.BlockSpec((tm, D_MODEL), lambda i: (i, 0))],
        out_shape=[jax.ShapeDtypeStruct((n_tok, D_MODEL), F32),
                   jax.ShapeDtypeStruct((n_tok, D_MODEL), BF16)],
        compiler_params=_params(("parallel",)),
        name=name,
    )(y_tk, y_tk, y_tk, y_tk, route_f, xt, gain, bias)


def _gelu_tanh(x):
    return 0.5 * x * (1.0 + jnp.tanh(math.sqrt(2.0 / math.pi) * (x + 0.044715 * (x * x * x))))


LRU_ROWS = 256


def _lru_body(rec_ref, g_ref, cw_ref, cb_ref, wr_ref, br_ref, wi_ref, bi_ref, lam_ref, y_ref,
              ext_ref, a_ref, b_ref):
    seq, ct = rec_ref.shape
    heads = ct // LRU_BLOCK
    ext_ref[0:SUBLANES, :] = jnp.zeros((SUBLANES, ct), F32)
    ext_ref[SUBLANES:, :] = rec_ref[...]
    neg_lam = -lam_ref[...]
    softplus = jnp.maximum(neg_lam, 0.0) + jnp.log1p(jnp.exp(-jnp.abs(neg_lam)))
    wr = [wr_ref[h].astype(BF16) for h in range(heads)]
    wi = [wi_ref[h].astype(BF16) for h in range(heads)]

    for c in range(seq // LRU_ROWS):
        r0 = c * LRU_ROWS
        conv = cb_ref[...] + cw_ref[CONV_WIDTH - 1:CONV_WIDTH, :] * rec_ref[pl.ds(r0, LRU_ROWS), :]
        for j in range(CONV_WIDTH - 1):
            conv = conv + cw_ref[j:j + 1, :] * ext_ref[pl.ds(r0 + SUBLANES - (CONV_WIDTH - 1) + j, LRU_ROWS), :]
        xb = conv.astype(BF16)
        r_parts = [jnp.dot(xb[:, h * LRU_BLOCK:(h + 1) * LRU_BLOCK], wr[h], preferred_element_type=F32)
                   for h in range(heads)]
        i_parts = [jnp.dot(xb[:, h * LRU_BLOCK:(h + 1) * LRU_BLOCK], wi[h], preferred_element_type=F32)
                   for h in range(heads)]
        r = jax.nn.sigmoid(jnp.concatenate(r_parts, axis=-1) + br_ref[...])
        ig = jax.nn.sigmoid(jnp.concatenate(i_parts, axis=-1) + bi_ref[...])
        log_a = -LRU_C * r * softplus
        a = jnp.exp(log_a)
        a_ref[pl.ds(r0, LRU_ROWS), :] = a
        b_ref[pl.ds(r0, LRU_ROWS), :] = jnp.sqrt(-jnp.tanh(log_a) * (a * a + 1.0)) * (ig * conv)

    _lru_scan_body(a_ref, b_ref, g_ref, y_ref)


def _lru_scan_body(a_ref, b_ref, g_ref, y_ref):
    seq, ct = a_ref.shape
    row = lax.broadcasted_iota(I32, (SUBLANES, ct), 0)

    def step(gi, carry):
        off = pl.multiple_of(gi * SUBLANES, SUBLANES)
        a = a_ref[pl.ds(off, SUBLANES), :]
        b = b_ref[pl.ds(off, SUBLANES), :]
        for d in (1, 2, 4):
            keep = row >= d
            b = jnp.where(keep, a * pltpu.roll(b, d, 0) + b, b)
            a = jnp.where(keep, a * pltpu.roll(a, d, 0), a)
        h = a * carry + b
        y_ref[pl.ds(off, SUBLANES), :] = _gelu_tanh(g_ref[pl.ds(off, SUBLANES), :]) * h
        return jnp.broadcast_to(h[SUBLANES - 1:SUBLANES, :], (SUBLANES, ct))

    lax.fori_loop(0, seq // SUBLANES, step, jnp.zeros((SUBLANES, ct), F32), unroll=4)


def _rglru_mixer(x2, bsz, seq, idx, a_w_in, conv_w, conv_b, w_rgate, b_rgate, w_igate, b_igate, lam, layer):
    t = bsz * seq
    w = D_MODEL
    u = _matmul(x2, a_w_in, idx, 0, 2 * w, F32, f"lru_in_l{layer}")
    ct = 2 * LRU_BLOCK
    nct = w // ct
    hpt = ct // LRU_BLOCK
    vec = lambda: pl.BlockSpec((1, ct), lambda b, j: (0, j))
    return pl.pallas_call(
        _lru_body,
        grid=(bsz, nct),
        in_specs=[pl.BlockSpec((seq, ct), lambda b, j: (b, nct + j)),
                  pl.BlockSpec((seq, ct), lambda b, j: (b, j)),
                  pl.BlockSpec((CONV_WIDTH, ct), lambda b, j: (0, j)),
                  vec(),
                  pl.BlockSpec((hpt, LRU_BLOCK, LRU_BLOCK), lambda b, j: (j, 0, 0)),
                  vec(),
                  pl.BlockSpec((hpt, LRU_BLOCK, LRU_BLOCK), lambda b, j: (j, 0, 0)),
                  vec(), vec()],
        out_specs=pl.BlockSpec((seq, ct), lambda b, j: (b, j)),
        out_shape=jax.ShapeDtypeStruct((t, w), F32),
        scratch_shapes=[pltpu.VMEM((seq + SUBLANES, ct), F32),
                        pltpu.VMEM((seq, ct), F32),
                        pltpu.VMEM((seq, ct), F32)],
        compiler_params=_params(("parallel", "parallel")),
        name=f"lru_l{layer}",
    )(u, u, conv_w[idx], conv_b[idx][None], w_rgate[idx], b_rgate[idx][None],
      w_igate[idx], b_igate[idx][None], lam[idx][None])


def _ret_body(lg_ref, q_ref, k_ref, v_ref, g_ref, y_ref, state_ref):
    c = pl.program_id(1)
    cs = RET_CHUNK
    heads = range(RET_HEADS)

    @pl.when(c == 0)
    def _():
        state_ref[...] = jnp.zeros_like(state_ref)

    qi = lax.broadcasted_iota(I32, (cs, cs), 0)
    ki = lax.broadcasted_iota(I32, (cs, cs), 1)
    rel = (qi - ki).astype(F32)
    pos_col = lax.broadcasted_iota(I32, (cs, 1), 0).astype(F32)
    lgs = [lg_ref[h] for h in heads]
    intra_decay = [jnp.where(rel >= 0, jnp.exp(lg * jnp.maximum(rel, 0.0)), 0.0) for lg in lgs]
    query_decay = [jnp.exp(lg * (pos_col + 1.0)) for lg in lgs]
    key_decay = [jnp.exp(lg * (cs - 1.0 - pos_col)) for lg in lgs]
    chunk_decay = [jnp.exp(jnp.full((1, 1), lg * cs, F32)) for lg in lgs]

    scale = RET_QK ** -0.5
    dn_nt = (((1,), (1,)), ((), ()))
    dn_tn = (((0,), (0,)), ((), ()))
    qs = [q_ref[:, h * RET_QK:(h + 1) * RET_QK] for h in heads]
    ks = [k_ref[:, h * RET_QK:(h + 1) * RET_QK] for h in heads]
    vs = [v_ref[:, h * RET_V:(h + 1) * RET_V] for h in heads]
    intra = [lax.dot_general(q, k, dn_nt, preferred_element_type=F32) * scale * d
             for q, k, d in zip(qs, ks, intra_decay)]
    states = [state_ref[h] for h in heads]
    cross = [jnp.dot(q, s.astype(BF16), preferred_element_type=F32) * d
             for q, s, d in zip(qs, states, query_decay)]
    outs = [jnp.dot(a.astype(BF16), v, preferred_element_type=F32) + x for a, v, x in zip(intra, vs, cross)]
    kds = [(k.astype(F32) * (scale * d)).astype(BF16) for k, d in zip(ks, key_decay)]
    upd = [lax.dot_general(kd, v, dn_tn, preferred_element_type=F32) for kd, v in zip(kds, vs)]
    for h in heads:
        state_ref[h] = chunk_decay[h] * states[h] + upd[h]
    for h in heads:
        out = outs[h]
        o = out * lax.rsqrt(jnp.mean(out * out, axis=-1, keepdims=True) + RET_EPS)
        g = g_ref[:, h * RET_V:(h + 1) * RET_V]
        y_ref[:, h * RET_V:(h + 1) * RET_V] = (g * jax.nn.sigmoid(g) * o).astype(y_ref.dtype)


def _retention_mixer(x2, bsz, seq, idx, b_w_in, layer):
    t = bsz * seq
    qk_w = RET_HEADS * RET_QK
    v_w = RET_HEADS * RET_V
    qkv = _matmul(x2, b_w_in, idx, 0, 2 * qk_w + v_w, BF16, f"ret_qkv_l{layer}")
    gate = _matmul(x2, b_w_in, idx, 2 * qk_w + v_w, v_w, F32, f"ret_gate_l{layer}")
    nc = seq // RET_CHUNK
    log_gamma = jnp.log1p(-jnp.exp2(-5.0 - jnp.arange(RET_HEADS, dtype=F32)))
    grid_spec = pltpu.PrefetchScalarGridSpec(
        num_scalar_prefetch=1,
        grid=(bsz, nc),
        in_specs=[pl.BlockSpec((RET_CHUNK, qk_w), lambda b, c, lg: (b * nc + c, 0)),
                  pl.BlockSpec((RET_CHUNK, qk_w), lambda b, c, lg: (b * nc + c, 1)),
                  pl.BlockSpec((RET_CHUNK, v_w), lambda b, c, lg: (b * nc + c, 1)),
                  pl.BlockSpec((RET_CHUNK, v_w), lambda b, c, lg: (b * nc + c, 0))],
        out_specs=pl.BlockSpec((RET_CHUNK, v_w), lambda b, c, lg: (b * nc + c, 0)),
        scratch_shapes=[pltpu.VMEM((RET_HEADS, RET_QK, RET_V), F32)],
    )
    return pl.pallas_call(
        _ret_body,
        grid_spec=grid_spec,
        out_shape=jax.ShapeDtypeStruct((t, v_w), BF16),
        compiler_params=_params(("parallel", "arbitrary")),
        name=f"retention_l{layer}",
    )(log_gamma, qkv, qkv, qkv, gate)


def _attn_body(*refs, dilation, has_prev):
    if has_prev:
        q_ref, kc_ref, kp_ref, vc_ref, vp_ref, o_ref, st_ref = refs
    else:
        q_ref, kc_ref, vc_ref, o_ref, st_ref = refs
    n = pl.program_id(2)
    nw = q_ref.shape[0]
    dh = ATTN_HEAD_DIM
    scale = dh ** -0.5
    qi = lax.broadcasted_iota(I32, (nw, nw), 0)
    ki = lax.broadcasted_iota(I32, (nw, nw), 1)
    steps_c = qi - ki
    valid_c = steps_c >= 0
    dist_c = (steps_c * dilation).astype(F32)
    q = q_ref[...]
    kc = kc_ref[...]
    vc = vc_ref[...]
    lane = lax.broadcasted_iota(I32, (nw, LANES), 1)
    dn = (((1,), (1,)), ((), ()))
    heads = range(ATTN_HEADS)
    sls = [slice(h * dh, (h + 1) * dh) for h in heads]
    slopes = [2.0 ** (-8.0 * (h + 1) / ATTN_HEADS) for h in heads]
    sc_c = [lax.dot_general(q[:, sl], kc[:, sl], dn, preferred_element_type=F32) for sl in sls]
    sc_c = [jnp.where(valid_c, s * scale - sp * dist_c, -jnp.inf) for s, sp in zip(sc_c, slopes)]
    if not has_prev:
        ms = [jnp.max(a, axis=-1, keepdims=True) for a in sc_c]
    if has_prev:
        steps_p = steps_c + nw
        valid_p = steps_p <= jnp.where(n > 0, nw, -1)
        dist_p = (steps_p * dilation).astype(F32)
        kp = kp_ref[...]
        vp = vp_ref[...]
        sc_p = [lax.dot_general(q[:, sl], kp[:, sl], dn, preferred_element_type=F32) for sl in sls]
        sc_p = [jnp.where(valid_p, s * scale - sp * dist_p, -jnp.inf) for s, sp in zip(sc_p, slopes)]
        ms = [jnp.max(jnp.maximum(a, b), axis=-1, keepdims=True) for a, b in zip(sc_c, sc_p)]
    pc = [jnp.exp(a - m) for a, m in zip(sc_c, ms)]
    outs = [jnp.dot(a.astype(BF16), vc[:, sl], preferred_element_type=F32) for a, sl in zip(pc, sls)]
    if has_prev:
        pp = [jnp.exp(b - m) for b, m in zip(sc_p, ms)]
        sums = [jnp.sum(a + b, axis=-1, keepdims=True) for a, b in zip(pc, pp)]
        outs = [o + jnp.dot(b.astype(BF16), vp[:, sl], preferred_element_type=F32)
                for o, b, sl in zip(outs, pp, sls)]
    else:
        sums = [jnp.sum(a, axis=-1, keepdims=True) for a in pc]
    stats = jnp.zeros((nw, LANES), F32)
    for h, sl, o, m, sm in zip(heads, sls, outs, ms, sums):
        o_ref[:, sl] = o.astype(o_ref.dtype)
        stats = jnp.where(lane == h, m, stats)
        stats = jnp.where(lane == ATTN_HEADS + h, sm, stats)
    st_ref[...] = stats


def _merge_body(o0_ref, o1_ref, o2_ref, s0_ref, s1_ref, s2_ref, y_ref):
    tm = y_ref.shape[0]
    stats = [s0_ref[...], s1_ref[...], s2_ref[...]]
    lane = lax.broadcasted_iota(I32, (tm, LANES), 1)
    maxes = [s for s in stats]
    sums = [pltpu.roll(s, LANES - ATTN_HEADS, 1) for s in stats]
    m_all = jnp.maximum(jnp.maximum(maxes[0], maxes[1]), maxes[2])
    wgt = [jnp.exp(m - m_all) for m in maxes]
    den = wgt[0] * sums[0] + wgt[1] * sums[1] + wgt[2] * sums[2]
    er = lax.broadcasted_iota(I32, (LANES, D_MODEL), 0)
    ec = lax.broadcasted_iota(I32, (LANES, D_MODEL), 1)
    expand = (ec // ATTN_HEAD_DIM == er).astype(BF16)
    acc = jnp.zeros((tm, D_MODEL), F32)
    for w, o_ref in zip(wgt, (o0_ref, o1_ref, o2_ref)):
        coef = jnp.where(lane < ATTN_HEADS, w / den, 0.0)
        hi = coef.astype(BF16)
        lo = (coef - hi.astype(F32)).astype(BF16)
        spread = (jnp.dot(hi, expand, preferred_element_type=F32)
                  + jnp.dot(lo, expand, preferred_element_type=F32))
        acc = acc + spread * o_ref[...].astype(F32)
    y_ref[...] = acc.astype(y_ref.dtype)


def _attention_mixer(x2, bsz, seq, idx, c_w_in, layer):
    t = bsz * seq
    w = D_MODEL
    n_groups = len(ATTN_PATTERNS)
    outs, stats = [], []
    for g, (window, dilation) in enumerate(ATTN_PATTERNS):
        nw = window // dilation
        sub = seq // dilation
        assert nw == 128 and sub % nw == 0
        nb = sub // nw

        def to_sub(a):
            return a.reshape(bsz, sub, dilation, -1).transpose(0, 2, 1, 3).reshape(t, -1)

        def from_sub(a):
            return a.reshape(bsz, dilation, sub, -1).transpose(0, 2, 1, 3).reshape(t, -1)

        xs = x2 if dilation == 1 else to_sub(x2)
        qkv = _matmul(xs, c_w_in, idx, g * w, 3 * w, BF16, f"attn_qkv_g{g}_l{layer}",
                      col_block_stride=n_groups)
        cur = lambda slab: pl.BlockSpec((nw, w), lambda b, r, n, slab=slab: ((b * dilation + r) * nb + n, slab))
        prev = lambda slab: pl.BlockSpec(
            (nw, w), lambda b, r, n, slab=slab: ((b * dilation + r) * nb + jnp.maximum(n - 1, 0), slab))
        has_prev = nb > 1
        in_specs = [cur(0), cur(1), prev(1), cur(2), prev(2)] if has_prev else [cur(0), cur(1), cur(2)]
        o, st = pl.pallas_call(
            functools.partial(_attn_body, dilation=dilation, has_prev=has_prev),
            grid=(bsz, dilation, nb),
            in_specs=in_specs,
            out_specs=[pl.BlockSpec((nw, w), lambda b, r, n: ((b * dilation + r) * nb + n, 0)),
                       pl.BlockSpec((nw, LANES), lambda b, r, n: ((b * dilation + r) * nb + n, 0))],
            out_shape=[jax.ShapeDtypeStruct((t, w), BF16), jax.ShapeDtypeStruct((t, LANES), F32)],
            compiler_params=_params(("parallel", "parallel", "parallel")),
            name=f"attn_g{g}_l{layer}",
        )(*([qkv] * len(in_specs)))
        outs.append(o if dilation == 1 else from_sub(o))
        stats.append(st if dilation == 1 else from_sub(st))
    tm = ROW_TM
    return pl.pallas_call(
        _merge_body,
        grid=(t // tm,),
        in_specs=[pl.BlockSpec((tm, w), lambda i: (i, 0))] * 3 + [pl.BlockSpec((tm, LANES), lambda i: (i, 0))] * 3,
        out_specs=pl.BlockSpec((tm, w), lambda i: (i, 0)),
        out_shape=jax.ShapeDtypeStruct((t, w), BF16),
        compiler_params=_params(("parallel",)),
        name=f"attn_merge_l{layer}",
    )(*outs, *stats)


def kernel(x, a_w_in, a_conv_w, a_conv_b, a_w_rgate, a_b_rgate, a_w_igate, a_b_igate, a_lambda, a_w_out,
           b_w_in, b_w_out, c_w_in, c_w_out, ln_gain, ln_bias, moe_w_router, moe_b_router, moe_w_gu,
           moe_b_gu, moe_w_down, moe_b_down):
    bsz, seq, d = x.shape
    t = bsz * seq
    x2 = x.reshape(t, d)
    x_mxu = x2
    b_gu4 = moe_b_gu[:, :, None, :]
    b_down4 = moe_b_down[:, :, None, :]
    for layer in range(DEPTH):
        kind, idx = layer % N_MIXERS, layer // N_MIXERS
        if kind == 0:
            y = _rglru_mixer(x_mxu, bsz, seq, idx, a_w_in, a_conv_w, a_conv_b, a_w_rgate, a_b_rgate,
                             a_w_igate, a_b_igate, a_lambda, layer)
            w_out = a_w_out
        elif kind == 1:
            y = _retention_mixer(x_mxu, bsz, seq, idx, b_w_in, layer)
            w_out = b_w_out
        else:
            y = _attention_mixer(x_mxu, bsz, seq, idx, c_w_in, layer)
            w_out = c_w_out
        xt, route_f, route_i, counts = _post_mixer(
            y, w_out, idx, x2, ln_gain[layer, 0][None], ln_bias[layer, 0][None],
            moe_w_router[layer], moe_b_router[layer][None], f"post_mixer_l{layer}")
        block_exp, n_used, src_rows, dst_rows = _route_tables(route_i, counts, t)
        y_tk = _moe_experts(xt, block_exp, n_used, src_rows, dst_rows,
                            moe_w_gu, b_gu4, moe_w_down, b_down4, layer, t)
        x2, x_mxu = _combine(y_tk, route_f, xt, ln_gain[layer, 1][None], ln_bias[layer, 1][None], t,
                             f"combine_l{layer}")
    return x2.reshape(bsz, seq, d)
```

```python
import functools
import math

import jax
import jax.numpy as jnp
from jax import lax
from jax.experimental import pallas as pl
from jax.experimental.pallas import tpu as pltpu

F32 = jnp.float32
BF16 = jnp.bfloat16
I32 = jnp.int32

D_MODEL = 1024
DEPTH = 4
N_MIXERS = 3
DEEPNORM_ALPHA = (2.0 * DEPTH) ** 0.25
LN_EPS = 1e-5
LRU_HEADS = 8
LRU_BLOCK = D_MODEL // LRU_HEADS
CONV_WIDTH = 4
LRU_C = 8.0
RET_HEADS = 4
RET_QK = D_MODEL // RET_HEADS
RET_V = 2 * RET_QK
RET_CHUNK = 128
RET_EPS = 1e-6
ATTN_PATTERNS = ((128, 1), (512, 4), (2048, 16))
ATTN_HEADS = 16
ATTN_HEAD_DIM = D_MODEL // ATTN_HEADS
N_EXPERTS = 32
TOP_K = 4
D_FF = D_MODEL
SWIGLU_LIMIT = 7.0
SWIGLU_ALPHA = 1.702

SUBLANES = 8
LANES = 128
LANE_TILES = D_MODEL // LANES
VMEM_LIMIT = 56 * 1024 * 1024

MOE_TM = 256
ROW_TM = 512


def _params(sem, vmem=VMEM_LIMIT):
    return pltpu.CompilerParams(dimension_semantics=sem, vmem_limit_bytes=vmem)


def _mm_body(x_ref, w_ref, o_ref, wbf_ref):
    @pl.when(pl.program_id(1) == 0)
    def _():
        wbf_ref[...] = w_ref[...].astype(BF16)

    o_ref[...] = jnp.dot(x_ref[...].astype(BF16), wbf_ref[...],
                         preferred_element_type=F32).astype(o_ref.dtype)


def _matmul(x, w_stack, idx, col0, ncols, out_dtype, name, tm=1024, tn=1024, col_block_stride=1):
    t, k = x.shape
    assert col0 % tn == 0 and ncols % tn == 0 and t % tm == 0
    return pl.pallas_call(
        _mm_body,
        grid=(ncols // tn, t // tm),
        in_specs=[pl.BlockSpec((tm, k), lambda n, m: (m, 0)),
                  pl.BlockSpec((None, k, tn), lambda n, m: (idx, 0, col0 // tn + n * col_block_stride))],
        out_specs=pl.BlockSpec((tm, tn), lambda n, m: (m, n)),
        out_shape=jax.ShapeDtypeStruct((t, ncols), out_dtype),
        scratch_shapes=[pltpu.VMEM((k, tn), BF16)],
        compiler_params=_params(("arbitrary", "arbitrary")),
        name=name,
    )(x, w_stack)


def _layer_norm(z, gain, bias):
    mu = jnp.mean(z, axis=-1, keepdims=True)
    zc = z - mu
    var = jnp.mean(zc * zc, axis=-1, keepdims=True)
    return zc * lax.rsqrt(var + LN_EPS) * gain + bias


def _store_token_tiles(ref, val):
    tm = val.shape[0]
    for c in range(LANE_TILES):
        ref[pl.ds(c, tm, stride=LANE_TILES), :] = val[:, c * LANES:(c + 1) * LANES]


def _load_token_tiles(ref, tm):
    return jnp.concatenate([ref[pl.ds(c, tm, stride=LANE_TILES), :] for c in range(LANE_TILES)], axis=-1)


def _post_body(y_ref, w_ref, x_ref, g_ref, b_ref, wr_ref, br_ref,
               xt_ref, rf_ref, ri_ref, cnt_ref, wbf_ref, wrh_ref, wrl_ref, run_ref):
    i = pl.program_id(0)
    tm = x_ref.shape[0]

    @pl.when(i == 0)
    def _():
        wbf_ref[...] = w_ref[...].astype(BF16)
        wr = wr_ref[...]
        wrh = wr.astype(BF16)
        wrh_ref[...] = wrh
        wrl_ref[...] = (wr - wrh.astype(F32)).astype(BF16)
        run_ref[...] = jnp.zeros_like(run_ref)

    z = DEEPNORM_ALPHA * x_ref[...] + jnp.dot(y_ref[...].astype(BF16), wbf_ref[...],
                                              preferred_element_type=F32)
    xn = _layer_norm(z, g_ref[...], b_ref[...])
    _store_token_tiles(xt_ref, xn)

    xh = xn.astype(BF16)
    xl = (xn - xh.astype(F32)).astype(BF16)
    logits_tm = (jnp.dot(xh, wrh_ref[...], preferred_element_type=F32)
                 + jnp.dot(xl, wrh_ref[...], preferred_element_type=F32)
                 + jnp.dot(xh, wrl_ref[...], preferred_element_type=F32) + br_ref[...])
    logits = jnp.transpose(logits_tm)[:N_EXPERTS, :]
    row_e = lax.broadcasted_iota(I32, (N_EXPERTS, tm), 0).astype(F32)
    work = logits
    top_l, top_e, onehots = [], [], []
    for _ in range(TOP_K):
        m = jnp.max(work, axis=0, keepdims=True)
        idx = jnp.min(jnp.where(work == m, row_e, float(N_EXPERTS)), axis=0, keepdims=True)
        oh = row_e == idx
        top_l.append(m)
        top_e.append(idx)
        onehots.append(oh)
        work = jnp.where(oh, -jnp.inf, work)
    exps = [jnp.exp(l - top_l[0]) for l in top_l]
    denom = exps[0] + exps[1] + exps[2] + exps[3]
    gates = [e / denom for e in exps]

    sel = jnp.zeros((N_EXPERTS, tm), F32)
    for oh in onehots:
        sel = sel + oh.astype(F32)
    ri = lax.broadcasted_iota(I32, (tm, tm), 0)
    ci = lax.broadcasted_iota(I32, (tm, tm), 1)
    earlier = (ri < ci).astype(BF16)
    before = jnp.dot(sel.astype(BF16), earlier, preferred_element_type=F32) + run_ref[...]
    ranks = [jnp.sum(jnp.where(oh, before, 0.0), axis=0, keepdims=True) for oh in onehots]
    run_ref[...] = run_ref[...] + jnp.sum(sel, axis=1, keepdims=True)
    cnt_ref[...] = run_ref[...]

    row8 = lax.broadcasted_iota(I32, (2 * TOP_K, tm), 0)
    gate_rows = jnp.zeros((2 * TOP_K, tm), F32)
    rint = jnp.zeros((2 * TOP_K, tm), I32)
    for k in range(TOP_K):
        gate_rows = jnp.where(row8 == k, gates[k], gate_rows)
        rint = jnp.where(row8 == k, top_e[k].astype(I32), rint)
        rint = jnp.where(row8 == TOP_K + k, ranks[k].astype(I32), rint)
    ri_ref[...] = rint
    pick = (lax.broadcasted_iota(I32, (2 * TOP_K, LANES), 0)
            == lax.broadcasted_iota(I32, (2 * TOP_K, LANES), 1)).astype(F32)
    rf_ref[...] = lax.dot_general(gate_rows, pick, (((0,), (0,)), ((), ())), preferred_element_type=F32,
                                  precision=lax.Precision.HIGHEST)


def _post_mixer(y, w_out_stack, idx, x, gain, bias, w_router, b_router, name):
    t, kd = y.shape
    tm = ROW_TM
    return pl.pallas_call(
        _post_body,
        grid=(t // tm,),
        in_specs=[pl.BlockSpec((tm, kd), lambda i: (i, 0)),
                  pl.BlockSpec((None, kd, D_MODEL), lambda i: (idx, 0, 0)),
                  pl.BlockSpec((tm, D_MODEL), lambda i: (i, 0)),
                  pl.BlockSpec((1, D_MODEL), lambda i: (0, 0)),
                  pl.BlockSpec((1, D_MODEL), lambda i: (0, 0)),
                  pl.BlockSpec((D_MODEL, LANES), lambda i: (0, 0)),
                  pl.BlockSpec((1, LANES), lambda i: (0, 0))],
        out_specs=[pl.BlockSpec((tm * LANE_TILES, LANES), lambda i: (i, 0)),
                   pl.BlockSpec((tm, LANES), lambda i: (i, 0)),
                   pl.BlockSpec((2 * TOP_K, tm), lambda i: (0, i)),
                   pl.BlockSpec((N_EXPERTS, 1), lambda i: (0, 0))],
        out_shape=[jax.ShapeDtypeStruct((t * LANE_TILES, LANES), F32),
                   jax.ShapeDtypeStruct((t, LANES), F32),
                   jax.ShapeDtypeStruct((2 * TOP_K, t), I32),
                   jax.ShapeDtypeStruct((N_EXPERTS, 1), F32)],
        scratch_shapes=[pltpu.VMEM((kd, D_MODEL), BF16), pltpu.VMEM((D_MODEL, LANES), BF16),
                        pltpu.VMEM((D_MODEL, LANES), BF16), pltpu.VMEM((N_EXPERTS, 1), F32)],
        compiler_params=_params(("arbitrary",)),
        name=name,
    )(y, w_out_stack, x, gain, bias,
      jnp.pad(w_router, ((0, 0), (0, LANES - N_EXPERTS))),
      jnp.pad(b_router.reshape(1, N_EXPERTS), ((0, 0), (0, LANES - N_EXPERTS))))


def _gather_copy(x_hbm, xbuf, sem, src_row, r):
    return pltpu.make_async_copy(
        x_hbm.at[pl.ds(pl.multiple_of(src_row, SUBLANES), SUBLANES), :],
        xbuf.at[pl.ds(pl.multiple_of(r * SUBLANES, SUBLANES), SUBLANES), :],
        sem)


def _scatter_copy(ybuf, y_hbm, sem, dst_row, r):
    return pltpu.make_async_copy(
        ybuf.at[pl.ds(pl.multiple_of(r * SUBLANES, SUBLANES), SUBLANES), :],
        y_hbm.at[pl.ds(pl.multiple_of(dst_row, SUBLANES), SUBLANES), :],
        sem)


def _moe_body(be_ref, nu_ref, src_ref, srcn_ref, dst_ref, x_hbm, wgu_ref, bgu_ref, wd_ref, bd_ref,
              y_hbm, xbuf, ybuf, wgu_bf, wd_bf, gsem, ssem):
    i = pl.program_id(0)
    nb = pl.num_programs(0)
    nused = nu_ref[0]
    slot = i % 2
    rows = MOE_TM * LANE_TILES
    group = 32

    def gather_all(idx_ref, s):
        def body(g, c):
            for q in range(group):
                r = g * group + q
                _gather_copy(x_hbm, xbuf.at[s], gsem.at[s], idx_ref[0, r], r).start(priority=q % 2)
            return c
        lax.fori_loop(0, MOE_TM // group, body, 0)

    def scatter_all(idx_ref, s):
        def body(g, c):
            for q in range(group):
                r = g * group + q
                _scatter_copy(ybuf.at[s], y_hbm, ssem.at[s], idx_ref[0, r], r).start(priority=q % 2)
            return c
        lax.fori_loop(0, MOE_TM // group, body, 0)

    def wait_scatter(s):
        pltpu.make_async_copy(ybuf.at[s], y_hbm.at[pl.ds(0, rows), :], ssem.at[s]).wait()

    @pl.when(i == 0)
    def _():
        ybuf[...] = jnp.zeros_like(ybuf)
        tail = y_hbm.at[pl.ds(y_hbm.shape[0] - 2 * rows, 2 * rows), :]
        for s in range(2):
            cp = pltpu.make_async_copy(ybuf.at[s], tail.at[pl.ds(s * rows, rows), :], ssem.at[s])
            cp.start()
            cp.wait()

    @pl.when(i < nused)
    def _():
        @pl.when(i == 0)
        def _():
            gather_all(src_ref, 0)

        @pl.when(i + 1 < nused)
        def _():
            gather_all(srcn_ref, 1 - slot)

        changed = jnp.logical_or(i == 0, be_ref[i] != be_ref[jnp.maximum(i - 1, 0)])

        @pl.when(changed)
        def _():
            wgu_bf[...] = wgu_ref[...].astype(BF16)
            wd_bf[...] = wd_ref[...].astype(BF16)

        pltpu.make_async_copy(x_hbm.at[pl.ds(0, rows), :], xbuf.at[slot], gsem.at[slot]).wait()
        x = _load_token_tiles(xbuf.at[slot], MOE_TM).astype(BF16)
        gu = jnp.dot(x, wgu_bf[...], preferred_element_type=F32) + bgu_ref[...]
        gate = jnp.minimum(gu[:, :D_FF], SWIGLU_LIMIT)
        up = jnp.clip(gu[:, D_FF:], -SWIGLU_LIMIT, SWIGLU_LIMIT)
        hidden = (up + 1.0) * gate * jax.nn.sigmoid(SWIGLU_ALPHA * gate)
        y = jnp.dot(hidden.astype(BF16), wd_bf[...], preferred_element_type=F32) + bd_ref[...]

        @pl.when(i >= 2)
        def _():
            wait_scatter(slot)

        _store_token_tiles(ybuf.at[slot], y)
        scatter_all(dst_ref, slot)

    @pl.when(i == nb - 1)
    def _():
        @pl.when(nused >= 1)
        def _():
            wait_scatter((nused - 1) % 2)

        @pl.when(nused >= 2)
        def _():
            wait_scatter(nused % 2)


def _moe_experts(xt, block_exp, n_used, src_rows, dst_rows, w_gu, b_gu, w_down, b_down, layer, n_tok):
    nb = block_exp.shape[0]
    rows = MOE_TM * LANE_TILES
    n_out_rows = (TOP_K * n_tok + 2 * MOE_TM) * LANE_TILES
    smem_blk = lambda f: pl.BlockSpec((None, 1, MOE_TM), f, memory_space=pltpu.SMEM)
    grid_spec = pltpu.PrefetchScalarGridSpec(
        num_scalar_prefetch=2,
        grid=(nb,),
        in_specs=[smem_blk(lambda i, be, nu: (i, 0, 0)),
                  smem_blk(lambda i, be, nu: (jnp.minimum(i + 1, nb - 1), 0, 0)),
                  smem_blk(lambda i, be, nu: (i, 0, 0)),
                  pl.BlockSpec(memory_space=pl.ANY),
                  pl.BlockSpec((None, None, D_MODEL, 2 * D_FF), lambda i, be, nu: (layer, be[i], 0, 0)),
                  pl.BlockSpec((None, None, 1, 2 * D_FF), lambda i, be, nu: (layer, be[i], 0, 0)),
                  pl.BlockSpec((None, None, D_FF, D_MODEL), lambda i, be, nu: (layer, be[i], 0, 0)),
                  pl.BlockSpec((None, None, 1, D_MODEL), lambda i, be, nu: (layer, be[i], 0, 0))],
        out_specs=pl.BlockSpec(memory_space=pl.ANY),
        scratch_shapes=[pltpu.VMEM((2, rows, LANES), F32),
                        pltpu.VMEM((2, rows, LANES), F32),
                        pltpu.VMEM((D_MODEL, 2 * D_FF), BF16),
                        pltpu.VMEM((D_FF, D_MODEL), BF16),
                        pltpu.SemaphoreType.DMA((2,)),
                        pltpu.SemaphoreType.DMA((2,))],
    )
    return pl.pallas_call(
        _moe_body,
        grid_spec=grid_spec,
        out_shape=jax.ShapeDtypeStruct((n_out_rows, LANES), F32),
        compiler_params=_params(("arbitrary",)),
        name=f"moe_experts_l{layer}",
    )(block_exp, n_used, src_rows, src_rows, dst_rows, xt, w_gu, b_gu, w_down, b_down)


def _route_tables(route_i, counts, n_tok):
    n_assign = n_tok * TOP_K
    nb = n_assign // MOE_TM + N_EXPERTS
    counts = counts.reshape(N_EXPERTS).astype(I32)
    padded = ((counts + MOE_TM - 1) // MOE_TM) * MOE_TM
    pad_end = jnp.cumsum(padded)
    pad_start = pad_end - padded
    eid = route_i[:TOP_K]
    rank = route_i[TOP_K:]
    expert_ids = jnp.arange(N_EXPERTS, dtype=I32)
    start_of = jnp.sum(jnp.where(eid[..., None] == expert_ids, pad_start, 0), axis=-1)
    dest = (start_of + rank).reshape(n_assign)
    pad_lo = jnp.concatenate([pad_start + counts, pad_end[-1:]]).astype(I32)
    pad_hi = jnp.concatenate([pad_end, jnp.full((1,), nb * MOE_TM, I32)]).astype(I32)
    n_rows = nb * MOE_TM
    pad_cnt = pad_hi - pad_lo
    pad_cum = jnp.cumsum(pad_cnt)
    pad_i = jnp.arange(n_rows - n_assign, dtype=I32)
    range_id = jnp.sum((pad_cum[None, :] <= pad_i[:, None]).astype(I32), axis=1)
    range_ids = jnp.arange(pad_lo.shape[0], dtype=I32)
    offset = jnp.sum(jnp.where(range_id[:, None] == range_ids, pad_lo - (pad_cum - pad_cnt), 0), axis=1)
    pad_rows = pad_i + offset
    keys = jnp.concatenate([dest, pad_rows])
    vals = jnp.concatenate([jnp.arange(n_assign, dtype=I32), n_assign + pad_rows % (2 * MOE_TM)])
    _, row_assign = lax.sort((keys, vals), num_keys=1)
    src_rows = jnp.where(row_assign < n_assign, row_assign % n_tok, 0) * LANE_TILES
    dst_rows = row_assign * LANE_TILES
    block_start = jnp.arange(nb, dtype=I32) * MOE_TM
    block_exp = jnp.minimum(jnp.sum((pad_end[None, :] <= block_start[:, None]).astype(I32), axis=1),
                            N_EXPERTS - 1)
    n_used = (pad_end[-1:] // MOE_TM).astype(I32)
    return block_exp, n_used, src_rows.reshape(nb, 1, MOE_TM), dst_rows.reshape(nb, 1, MOE_TM)


def _combine_body(y0_ref, y1_ref, y2_ref, y3_ref, rf_ref, xt_ref, g_ref, b_ref, o_ref, obf_ref):
    tm = o_ref.shape[0]
    rf = rf_ref[...]
    parts = []
    for c in range(LANE_TILES):
        acc = DEEPNORM_ALPHA * xt_ref[pl.ds(c, tm, stride=LANE_TILES), :]
        for k, yk in enumerate((y0_ref, y1_ref, y2_ref, y3_ref)):
            acc = acc + rf[:, k:k + 1] * yk[pl.ds(c, tm, stride=LANE_TILES), :]
        parts.append(acc)
    z = jnp.concatenate(parts, axis=-1)
    out = _layer_norm(z, g_ref[...], b_ref[...])
    o_ref[...] = out
    obf_ref[...] = out.astype(BF16)


def _combine(y_tk, route_f, xt, gain, bias, n_tok, name):
    tm = ROW_TM
    nblk = n_tok // tm
    yspec = lambda k: pl.BlockSpec((tm * LANE_TILES, LANES), lambda i: (k * nblk + i, 0))
    return pl.pallas_call(
        _combine_body,
        grid=(nblk,),
        in_specs=[yspec(0), yspec(1), yspec(2), yspec(3),
                  pl.BlockSpec((tm, LANES), lambda i: (i, 0)),
                  pl.BlockSpec((tm * LANE_TILES, LANES), lambda i: (i, 0)),
                  pl.BlockSpec((1, D_MODEL), lambda i: (0, 0)),
                  pl.BlockSpec((1, D_MODEL), lambda i: (0, 0))],
        out_specs=[pl.BlockSpec((tm, D_MODEL), lambda i: (i, 0)),
                   pl.BlockSpec((tm, D_MODEL), lambda i: (i, 0))],
        out_shape=[jax.ShapeDtypeStruct((n_tok, D_MODEL), F32),
                   jax.ShapeDtypeStruct((n_tok, D_MODEL), BF16)],
        compiler_params=_params(("parallel",)),
        name=name,
    )(y_tk, y_tk, y_tk, y_tk, route_f, xt, gain, bias)


def _gelu_tanh(x):
    return 0.5 * x * (1.0 + jnp.tanh(math.sqrt(2.0 / math.pi) * (x + 0.044715 * (x * x * x))))


LRU_ROWS = 256


def _lru_body(rec_ref, g_ref, cw_ref, cb_ref, wr_ref, br_ref, wi_ref, bi_ref, lam_ref, y_ref,
              ext_ref, a_ref, b_ref):
    seq, ct = rec_ref.shape
    heads = ct // LRU_BLOCK
    ext_ref[0:SUBLANES, :] = jnp.zeros((SUBLANES, ct), F32)
    ext_ref[SUBLANES:, :] = rec_ref[...]
    neg_lam = -lam_ref[...]
    softplus = jnp.maximum(neg_lam, 0.0) + jnp.log1p(jnp.exp(-jnp.abs(neg_lam)))
    wr = [wr_ref[h].astype(BF16) for h in range(heads)]
    wi = [wi_ref[h].astype(BF16) for h in range(heads)]

    for c in range(seq // LRU_ROWS):
        r0 = c * LRU_ROWS
        conv = cb_ref[...] + cw_ref[CONV_WIDTH - 1:CONV_WIDTH, :] * rec_ref[pl.ds(r0, LRU_ROWS), :]
        for j in range(CONV_WIDTH - 1):
            conv = conv + cw_ref[j:j + 1, :] * ext_ref[pl.ds(r0 + SUBLANES - (CONV_WIDTH - 1) + j, LRU_ROWS), :]
        xb = conv.astype(BF16)
        r_parts = [jnp.dot(xb[:, h * LRU_BLOCK:(h + 1) * LRU_BLOCK], wr[h], preferred_element_type=F32)
                   for h in range(heads)]
        i_parts = [jnp.dot(xb[:, h * LRU_BLOCK:(h + 1) * LRU_BLOCK], wi[h], preferred_element_type=F32)
                   for h in range(heads)]
        r = jax.nn.sigmoid(jnp.concatenate(r_parts, axis=-1) + br_ref[...])
        ig = jax.nn.sigmoid(jnp.concatenate(i_parts, axis=-1) + bi_ref[...])
        log_a = -LRU_C * r * softplus
        a = jnp.exp(log_a)
        a_ref[pl.ds(r0, LRU_ROWS), :] = a
        b_ref[pl.ds(r0, LRU_ROWS), :] = jnp.sqrt(-jnp.tanh(log_a) * (a * a + 1.0)) * (ig * conv)

    _lru_scan_body(a_ref, b_ref, g_ref, y_ref)


def _lru_scan_body(a_ref, b_ref, g_ref, y_ref):
    seq, ct = a_ref.shape
    row = lax.broadcasted_iota(I32, (SUBLANES, ct), 0)

    def step(gi, carry):
        off = pl.multiple_of(gi * SUBLANES, SUBLANES)
        a = a_ref[pl.ds(off, SUBLANES), :]
        b = b_ref[pl.ds(off, SUBLANES), :]
        for d in (1, 2, 4):
            keep = row >= d
            b = jnp.where(keep, a * pltpu.roll(b, d, 0) + b, b)
            a = jnp.where(keep, a * pltpu.roll(a, d, 0), a)
        h = a * carry + b
        y_ref[pl.ds(off, SUBLANES), :] = _gelu_tanh(g_ref[pl.ds(off, SUBLANES), :]) * h
        return jnp.broadcast_to(h[SUBLANES - 1:SUBLANES, :], (SUBLANES, ct))

    lax.fori_loop(0, seq // SUBLANES, step, jnp.zeros((SUBLANES, ct), F32), unroll=4)


def _rglru_mixer(x2, bsz, seq, idx, a_w_in, conv_w, conv_b, w_rgate, b_rgate, w_igate, b_igate, lam, layer):
    t = bsz * seq
    w = D_MODEL
    u = _matmul(x2, a_w_in, idx, 0, 2 * w, F32, f"lru_in_l{layer}")
    ct = 2 * LRU_BLOCK
    nct = w // ct
    hpt = ct // LRU_BLOCK
    vec = lambda: pl.BlockSpec((1, ct), lambda b, j: (0, j))
    return pl.pallas_call(
        _lru_body,
        grid=(bsz, nct),
        in_specs=[pl.BlockSpec((seq, ct), lambda b, j: (b, nct + j)),
                  pl.BlockSpec((seq, ct), lambda b, j: (b, j)),
                  pl.BlockSpec((CONV_WIDTH, ct), lambda b, j: (0, j)),
                  vec(),
                  pl.BlockSpec((hpt, LRU_BLOCK, LRU_BLOCK), lambda b, j: (j, 0, 0)),
                  vec(),
                  pl.BlockSpec((hpt, LRU_BLOCK, LRU_BLOCK), lambda b, j: (j, 0, 0)),
                  vec(), vec()],
        out_specs=pl.BlockSpec((seq, ct), lambda b, j: (b, j)),
        out_shape=jax.ShapeDtypeStruct((t, w), F32),
        scratch_shapes=[pltpu.VMEM((seq + SUBLANES, ct), F32),
                        pltpu.VMEM((seq, ct), F32),
                        pltpu.VMEM((seq, ct), F32)],
        compiler_params=_params(("parallel", "parallel")),
        name=f"lru_l{layer}",
    )(u, u, conv_w[idx], conv_b[idx][None], w_rgate[idx], b_rgate[idx][None],
      w_igate[idx], b_igate[idx][None], lam[idx][None])


def _ret_body(lg_ref, q_ref, k_ref, v_ref, g_ref, y_ref, state_ref):
    c = pl.program_id(1)
    cs = RET_CHUNK
    heads = range(RET_HEADS)

    @pl.when(c == 0)
    def _():
        state_ref[...] = jnp.zeros_like(state_ref)

    qi = lax.broadcasted_iota(I32, (cs, cs), 0)
    ki = lax.broadcasted_iota(I32, (cs, cs), 1)
    rel = (qi - ki).astype(F32)
    pos_col = lax.broadcasted_iota(I32, (cs, 1), 0).astype(F32)
    lgs = [lg_ref[h] for h in heads]
    intra_decay = [jnp.where(rel >= 0, jnp.exp(lg * jnp.maximum(rel, 0.0)), 0.0) for lg in lgs]
    query_decay = [jnp.exp(lg * (pos_col + 1.0)) for lg in lgs]
    key_decay = [jnp.exp(lg * (cs - 1.0 - pos_col)) for lg in lgs]
    chunk_decay = [jnp.exp(jnp.full((1, 1), lg * cs, F32)) for lg in lgs]

    scale = RET_QK ** -0.5
    dn_nt = (((1,), (1,)), ((), ()))
    dn_tn = (((0,), (0,)), ((), ()))
    qs = [q_ref[:, h * RET_QK:(h + 1) * RET_QK] for h in heads]
    ks = [k_ref[:, h * RET_QK:(h + 1) * RET_QK] for h in heads]
    vs = [v_ref[:, h * RET_V:(h + 1) * RET_V] for h in heads]
    intra = [lax.dot_general(q, k, dn_nt, preferred_element_type=F32) * scale * d
             for q, k, d in zip(qs, ks, intra_decay)]
    states = [state_ref[h] for h in heads]
    cross = [jnp.dot(q, s.astype(BF16), preferred_element_type=F32) * d
             for q, s, d in zip(qs, states, query_decay)]
    outs = [jnp.dot(a.astype(BF16), v, preferred_element_type=F32) + x for a, v, x in zip(intra, vs, cross)]
    kds = [(k.astype(F32) * (scale * d)).astype(BF16) for k, d in zip(ks, key_decay)]
    upd = [lax.dot_general(kd, v, dn_tn, preferred_element_type=F32) for kd, v in zip(kds, vs)]
    for h in heads:
        state_ref[h] = chunk_decay[h] * states[h] + upd[h]
    for h in heads:
        out = outs[h]
        o = out * lax.rsqrt(jnp.mean(out * out, axis=-1, keepdims=True) + RET_EPS)
        g = g_ref[:, h * RET_V:(h + 1) * RET_V]
        y_ref[:, h * RET_V:(h + 1) * RET_V] = (g * jax.nn.sigmoid(g) * o).astype(y_ref.dtype)


def _retention_mixer(x2, bsz, seq, idx, b_w_in, layer):
    t = bsz * seq
    qk_w = RET_HEADS * RET_QK
    v_w = RET_HEADS * RET_V
    qkv = _matmul(x2, b_w_in, idx, 0, 2 * qk_w + v_w, BF16, f"ret_qkv_l{layer}")
    gate = _matmul(x2, b_w_in, idx, 2 * qk_w + v_w, v_w, F32, f"ret_gate_l{layer}")
    nc = seq // RET_CHUNK
    log_gamma = jnp.log1p(-jnp.exp2(-5.0 - jnp.arange(RET_HEADS, dtype=F32)))
    grid_spec = pltpu.PrefetchScalarGridSpec(
        num_scalar_prefetch=1,
        grid=(bsz, nc),
        in_specs=[pl.BlockSpec((RET_CHUNK, qk_w), lambda b, c, lg: (b * nc + c, 0)),
                  pl.BlockSpec((RET_CHUNK, qk_w), lambda b, c, lg: (b * nc + c, 1)),
                  pl.BlockSpec((RET_CHUNK, v_w), lambda b, c, lg: (b * nc + c, 1)),
                  pl.BlockSpec((RET_CHUNK, v_w), lambda b, c, lg: (b * nc + c, 0))],
        out_specs=pl.BlockSpec((RET_CHUNK, v_w), lambda b, c, lg: (b * nc + c, 0)),
        scratch_shapes=[pltpu.VMEM((RET_HEADS, RET_QK, RET_V), F32)],
    )
    return pl.pallas_call(
        _ret_body,
        grid_spec=grid_spec,
        out_shape=jax.ShapeDtypeStruct((t, v_w), BF16),
        compiler_params=_params(("parallel", "arbitrary")),
        name=f"retention_l{layer}",
    )(log_gamma, qkv, qkv, qkv, gate)


def _attn_body(*refs, dilation, has_prev):
    if has_prev:
        q_ref, kc_ref, kp_ref, vc_ref, vp_ref, o_ref, st_ref = refs
    else:
        q_ref, kc_ref, vc_ref, o_ref, st_ref = refs
    n = pl.program_id(2)
    nw = q_ref.shape[0]
    dh = ATTN_HEAD_DIM
    scale = dh ** -0.5
    qi = lax.broadcasted_iota(I32, (nw, nw), 0)
    ki = lax.broadcasted_iota(I32, (nw, nw), 1)
    steps_c = qi - ki
    valid_c = steps_c >= 0
    dist_c = (steps_c * dilation).astype(F32)
    q = q_ref[...]
    kc = kc_ref[...]
    vc = vc_ref[...]
    lane = lax.broadcasted_iota(I32, (nw, LANES), 1)
    dn = (((1,), (1,)), ((), ()))
    heads = range(ATTN_HEADS)
    sls = [slice(h * dh, (h + 1) * dh) for h in heads]
    slopes = [2.0 ** (-8.0 * (h + 1) / ATTN_HEADS) for h in heads]
    sc_c = [lax.dot_general(q[:, sl], kc[:, sl], dn, preferred_element_type=F32) for sl in sls]
    sc_c = [jnp.where(valid_c, s * scale - sp * dist_c, -jnp.inf) for s, sp in zip(sc_c, slopes)]
    if not has_prev:
        ms = [jnp.max(a, axis=-1, keepdims=True) for a in sc_c]
    if has_prev:
        steps_p = steps_c + nw
        valid_p = steps_p <= jnp.where(n > 0, nw, -1)
        dist_p = (steps_p * dilation).astype(F32)
        kp = kp_ref[...]
        vp = vp_ref[...]
        sc_p = [lax.dot_general(q[:, sl], kp[:, sl], dn, preferred_element_type=F32) for sl in sls]
        sc_p = [jnp.where(valid_p, s * scale - sp * dist_p, -jnp.inf) for s, sp in zip(sc_p, slopes)]
        ms = [jnp.max(jnp.maximum(a, b), axis=-1, keepdims=True) for a, b in zip(sc_c, sc_p)]
    pc = [jnp.exp(a - m) for a, m in zip(sc_c, ms)]
    outs = [jnp.dot(a.astype(BF16), vc[:, sl], preferred_element_type=F32) for a, sl in zip(pc, sls)]
    if has_prev:
        pp = [jnp.exp(b - m) for b, m in zip(sc_p, ms)]
        sums = [jnp.sum(a + b, axis=-1, keepdims=True) for a, b in zip(pc, pp)]
        outs = [o + jnp.dot(b.astype(BF16), vp[:, sl], preferred_element_type=F32)
                for o, b, sl in zip(outs, pp, sls)]
    else:
        sums = [jnp.sum(a, axis=-1, keepdims=True) for a in pc]
    stats = jnp.zeros((nw, LANES), F32)
    for h, sl, o, m, sm in zip(heads, sls, outs, ms, sums):
        o_ref[:, sl] = o.astype(o_ref.dtype)
        stats = jnp.where(lane == h, m, stats)
        stats = jnp.where(lane == ATTN_HEADS + h, sm, stats)
    st_ref[...] = stats


def _merge_body(o0_ref, o1_ref, o2_ref, s0_ref, s1_ref, s2_ref, y_ref):
    tm = y_ref.shape[0]
    stats = [s0_ref[...], s1_ref[...], s2_ref[...]]
    lane = lax.broadcasted_iota(I32, (tm, LANES), 1)
    maxes = [s for s in stats]
    sums = [pltpu.roll(s, LANES - ATTN_HEADS, 1) for s in stats]
    m_all = jnp.maximum(jnp.maximum(maxes[0], maxes[1]), maxes[2])
    wgt = [jnp.exp(m - m_all) for m in maxes]
    den = wgt[0] * sums[0] + wgt[1] * sums[1] + wgt[2] * sums[2]
    er = lax.broadcasted_iota(I32, (LANES, D_MODEL), 0)
    ec = lax.broadcasted_iota(I32, (LANES, D_MODEL), 1)
    expand = (ec // ATTN_HEAD_DIM == er).astype(BF16)
    acc = jnp.zeros((tm, D_MODEL), F32)
    for w, o_ref in zip(wgt, (o0_ref, o1_ref, o2_ref)):
        coef = jnp.where(lane < ATTN_HEADS, w / den, 0.0)
        hi = coef.astype(BF16)
        lo = (coef - hi.astype(F32)).astype(BF16)
        spread = (jnp.dot(hi, expand, preferred_element_type=F32)
                  + jnp.dot(lo, expand, preferred_element_type=F32))
        acc = acc + spread * o_ref[...].astype(F32)
    y_ref[...] = acc.astype(y_ref.dtype)


def _attention_mixer(x2, bsz, seq, idx, c_w_in, layer):
    t = bsz * seq
    w = D_MODEL
    n_groups = len(ATTN_PATTERNS)
    outs, stats = [], []
    for g, (window, dilation) in enumerate(ATTN_PATTERNS):
        nw = window // dilation
        sub = seq // dilation
        assert nw == 128 and sub % nw == 0
        nb = sub // nw

        def to_sub(a):
            return a.reshape(bsz, sub, dilation, -1).transpose(0, 2, 1, 3).reshape(t, -1)

        def from_sub(a):
            return a.reshape(bsz, dilation, sub, -1).transpose(0, 2, 1, 3).reshape(t, -1)

        xs = x2 if dilation == 1 else to_sub(x2)
        qkv = _matmul(xs, c_w_in, idx, g * w, 3 * w, BF16, f"attn_qkv_g{g}_l{layer}",
                      col_block_stride=n_groups)
        cur = lambda slab: pl.BlockSpec((nw, w), lambda b, r, n, slab=slab: ((b * dilation + r) * nb + n, slab))
        prev = lambda slab: pl.BlockSpec(
            (nw, w), lambda b, r, n, slab=slab: ((b * dilation + r) * nb + jnp.maximum(n - 1, 0), slab))
        has_prev = nb > 1
        in_specs = [cur(0), cur(1), prev(1), cur(2), prev(2)] if has_prev else [cur(0), cur(1), cur(2)]
        o, st = pl.pallas_call(
            functools.partial(_attn_body, dilation=dilation, has_prev=has_prev),
            grid=(bsz, dilation, nb),
            in_specs=in_specs,
            out_specs=[pl.BlockSpec((nw, w), lambda b, r, n: ((b * dilation + r) * nb + n, 0)),
                       pl.BlockSpec((nw, LANES), lambda b, r, n: ((b * dilation + r) * nb + n, 0))],
            out_shape=[jax.ShapeDtypeStruct((t, w), BF16), jax.ShapeDtypeStruct((t, LANES), F32)],
            compiler_params=_params(("parallel", "parallel", "parallel")),
            name=f"attn_g{g}_l{layer}",
        )(*([qkv] * len(in_specs)))
        outs.append(o if dilation == 1 else from_sub(o))
        stats.append(st if dilation == 1 else from_sub(st))
    tm = ROW_TM
    return pl.pallas_call(
        _merge_body,
        grid=(t // tm,),
        in_specs=[pl.BlockSpec((tm, w), lambda i: (i, 0))] * 3 + [pl.BlockSpec((tm, LANES), lambda i: (i, 0))] * 3,
        out_specs=pl.BlockSpec((tm, w), lambda i: (i, 0)),
        out_shape=jax.ShapeDtypeStruct((t, w), BF16),
        compiler_params=_params(("parallel",)),
        name=f"attn_merge_l{layer}",
    )(*outs, *stats)


def kernel(x, a_w_in, a_conv_w, a_conv_b, a_w_rgate, a_b_rgate, a_w_igate, a_b_igate, a_lambda, a_w_out,
           b_w_in, b_w_out, c_w_in, c_w_out, ln_gain, ln_bias, moe_w_router, moe_b_router, moe_w_gu,
           moe_b_gu, moe_w_down, moe_b_down):
    bsz, seq, d = x.shape
    t = bsz * seq
    x2 = x.reshape(t, d)
    x_mxu = x2
    b_gu4 = moe_b_gu[:, :, None, :]
    b_down4 = moe_b_down[:, :, None, :]
    for layer in range(DEPTH):
        kind, idx = layer % N_MIXERS, layer // N_MIXERS
        if kind == 0:
            y = _rglru_mixer(x_mxu, bsz, seq, idx, a_w_in, a_conv_w, a_conv_b, a_w_rgate, a_b_rgate,
                             a_w_igate, a_b_igate, a_lambda, layer)
            w_out = a_w_out
        elif kind == 1:
            y = _retention_mixer(x_mxu, bsz, seq, idx, b_w_in, layer)
            w_out = b_w_out
        else:
            y = _attention_mixer(x_mxu, bsz, seq, idx, c_w_in, layer)
            w_out = c_w_out
        xt, route_f, route_i, counts = _post_mixer(
            y, w_out, idx, x2, ln_gain[layer, 0][None], ln_bias[layer, 0][None],
            moe_w_router[layer], moe_b_router[layer][None], f"post_mixer_l{layer}")
        block_exp, n_used, src_rows, dst_rows = _route_tables(route_i, counts, t)
        y_tk = _moe_experts(xt, block_exp, n_used, src_rows, dst_rows,
                            moe_w_gu, b_gu4, moe_w_down, b_down4, layer, t)
        x2, x_mxu = _combine(y_tk, route_f, xt, ln_gain[layer, 1][None], ln_bias[layer, 1][None], t,
                             f"combine_l{layer}")
    return x2.reshape(bsz, seq, d)
```

```python
import functools
import math

import jax
import jax.numpy as jnp
from jax import lax
from jax.experimental import pallas as pl
from jax.experimental.pallas import tpu as pltpu

F32 = jnp.float32
BF16 = jnp.bfloat16
I32 = jnp.int32

D_MODEL = 1024
DEPTH = 4
N_MIXERS = 3
DEEPNORM_ALPHA = (2.0 * DEPTH) ** 0.25
LN_EPS = 1e-5
LRU_HEADS = 8
LRU_BLOCK = D_MODEL // LRU_HEADS
CONV_WIDTH = 4
LRU_C = 8.0
RET_HEADS = 4
RET_QK = D_MODEL // RET_HEADS
RET_V = 2 * RET_QK
RET_CHUNK = 128
RET_EPS = 1e-6
ATTN_PATTERNS = ((128, 1), (512, 4), (2048, 16))
ATTN_HEADS = 16
ATTN_HEAD_DIM = D_MODEL // ATTN_HEADS
N_EXPERTS = 32
TOP_K = 4
D_FF = D_MODEL
SWIGLU_LIMIT = 7.0
SWIGLU_ALPHA = 1.702

SUBLANES = 8
LANES = 128
LANE_TILES = D_MODEL // LANES
VMEM_LIMIT = 56 * 1024 * 1024

MOE_TM = 256
ROW_TM = 512


def _params(sem, vmem=VMEM_LIMIT):
    return pltpu.CompilerParams(dimension_semantics=sem, vmem_limit_bytes=vmem)


def _mm_body(x_ref, w_ref, o_ref, wbf_ref):
    @pl.when(pl.program_id(1) == 0)
    def _():
        wbf_ref[...] = w_ref[...].astype(BF16)

    o_ref[...] = jnp.dot(x_ref[...].astype(BF16), wbf_ref[...],
                         preferred_element_type=F32).astype(o_ref.dtype)


def _matmul(x, w_stack, idx, col0, ncols, out_dtype, name, tm=1024, tn=1024, col_block_stride=1):
    t, k = x.shape
    assert col0 % tn == 0 and ncols % tn == 0 and t % tm == 0
    return pl.pallas_call(
        _mm_body,
        grid=(ncols // tn, t // tm),
        in_specs=[pl.BlockSpec((tm, k), lambda n, m: (m, 0)),
                  pl.BlockSpec((None, k, tn), lambda n, m: (idx, 0, col0 // tn + n * col_block_stride))],
        out_specs=pl.BlockSpec((tm, tn), lambda n, m: (m, n)),
        out_shape=jax.ShapeDtypeStruct((t, ncols), out_dtype),
        scratch_shapes=[pltpu.VMEM((k, tn), BF16)],
        compiler_params=_params(("arbitrary", "arbitrary")),
        name=name,
    )(x, w_stack)


def _layer_norm(z, gain, bias):
    mu = jnp.mean(z, axis=-1, keepdims=True)
    zc = z - mu
    var = jnp.mean(zc * zc, axis=-1, keepdims=True)
    return zc * lax.rsqrt(var + LN_EPS) * gain + bias


def _store_token_tiles(ref, val):
    tm = val.shape[0]
    for c in range(LANE_TILES):
        ref[pl.ds(c, tm, stride=LANE_TILES), :] = val[:, c * LANES:(c + 1) * LANES]


def _load_token_tiles(ref, tm):
    return jnp.concatenate([ref[pl.ds(c, tm, stride=LANE_TILES), :] for c in range(LANE_TILES)], axis=-1)


def _post_body(y_ref, w_ref, x_ref, g_ref, b_ref, wr_ref, br_ref,
               xt_ref, rf_ref, ri_ref, cnt_ref, wbf_ref, wrh_ref, wrl_ref, run_ref):
    i = pl.program_id(0)
    tm = x_ref.shape[0]

    @pl.when(i == 0)
    def _():
        wbf_ref[...] = w_ref[...].astype(BF16)
        wr = wr_ref[...]
        wrh = wr.astype(BF16)
        wrh_ref[...] = wrh
        wrl_ref[...] = (wr - wrh.astype(F32)).astype(BF16)
        run_ref[...] = jnp.zeros_like(run_ref)

    z = DEEPNORM_ALPHA * x_ref[...] + jnp.dot(y_ref[...].astype(BF16), wbf_ref[...],
                                              preferred_element_type=F32)
    xn = _layer_norm(z, g_ref[...], b_ref[...])
    _store_token_tiles(xt_ref, xn)

    xh = xn.astype(BF16)
    xl = (xn - xh.astype(F32)).astype(BF16)
    logits_tm = (jnp.dot(xh, wrh_ref[...], preferred_element_type=F32)
                 + jnp.dot(xl, wrh_ref[...], preferred_element_type=F32)
                 + jnp.dot(xh, wrl_ref[...], preferred_element_type=F32) + br_ref[...])
    logits = jnp.transpose(logits_tm)[:N_EXPERTS, :]
    row_e = lax.broadcasted_iota(I32, (N_EXPERTS, tm), 0).astype(F32)
    work = logits
    top_l, top_e, onehots = [], [], []
    for _ in range(TOP_K):
        m = jnp.max(work, axis=0, keepdims=True)
        idx = jnp.min(jnp.where(work == m, row_e, float(N_EXPERTS)), axis=0, keepdims=True)
        oh = row_e == idx
        top_l.append(m)
        top_e.append(idx)
        onehots.append(oh)
        work = jnp.where(oh, -jnp.inf, work)
    exps = [jnp.exp(l - top_l[0]) for l in top_l]
    denom = exps[0] + exps[1] + exps[2] + exps[3]
    gates = [e / denom for e in exps]

    sel = jnp.zeros((N_EXPERTS, tm), F32)
    for oh in onehots:
        sel = sel + oh.astype(F32)
    ri = lax.broadcasted_iota(I32, (tm, tm), 0)
    ci = lax.broadcasted_iota(I32, (tm, tm), 1)
    earlier = (ri < ci).astype(BF16)
    before = jnp.dot(sel.astype(BF16), earlier, preferred_element_type=F32) + run_ref[...]
    ranks = [jnp.sum(jnp.where(oh, before, 0.0), axis=0, keepdims=True) for oh in onehots]
    run_ref[...] = run_ref[...] + jnp.sum(sel, axis=1, keepdims=True)
    cnt_ref[...] = run_ref[...]

    row8 = lax.broadcasted_iota(I32, (2 * TOP_K, tm), 0)
    gate_rows = jnp.zeros((2 * TOP_K, tm), F32)
    rint = jnp.zeros((2 * TOP_K, tm), I32)
    for k in range(TOP_K):
        gate_rows = jnp.where(row8 == k, gates[k], gate_rows)
        rint = jnp.where(row8 == k, top_e[k].astype(I32), rint)
        rint = jnp.where(row8 == TOP_K + k, ranks[k].astype(I32), rint)
    ri_ref[...] = rint
    pick = (lax.broadcasted_iota(I32, (2 * TOP_K, LANES), 0)
            == lax.broadcasted_iota(I32, (2 * TOP_K, LANES), 1)).astype(F32)
    rf_ref[...] = lax.dot_general(gate_rows, pick, (((0,), (0,)), ((), ())), preferred_element_type=F32,
                                  precision=lax.Precision.HIGHEST)


def _post_mixer(y, w_out_stack, idx, x, gain, bias, w_router, b_router, name):
    t, kd = y.shape
    tm = ROW_TM
    return pl.pallas_call(
        _post_body,
        grid=(t // tm,),
        in_specs=[pl.BlockSpec((tm, kd), lambda i: (i, 0)),
                  pl.BlockSpec((None, kd, D_MODEL), lambda i: (idx, 0, 0)),
                  pl.BlockSpec((tm, D_MODEL), lambda i: (i, 0)),
                  pl.BlockSpec((1, D_MODEL), lambda i: (0, 0)),
                  pl.BlockSpec((1, D_MODEL), lambda i: (0, 0)),
                  pl.BlockSpec((D_MODEL, LANES), lambda i: (0, 0)),
                  pl.BlockSpec((1, LANES), lambda i: (0, 0))],
        out_specs=[pl.BlockSpec((tm * LANE_TILES, LANES), lambda i: (i, 0)),
                   pl.BlockSpec((tm, LANES), lambda i: (i, 0)),
                   pl.BlockSpec((2 * TOP_K, tm), lambda i: (0, i)),
                   pl.BlockSpec((N_EXPERTS, 1), lambda i: (0, 0))],
        out_shape=[jax.ShapeDtypeStruct((t * LANE_TILES, LANES), F32),
                   jax.ShapeDtypeStruct((t, LANES), F32),
                   jax.ShapeDtypeStruct((2 * TOP_K, t), I32),
                   jax.ShapeDtypeStruct((N_EXPERTS, 1), F32)],
        scratch_shapes=[pltpu.VMEM((kd, D_MODEL), BF16), pltpu.VMEM((D_MODEL, LANES), BF16),
                        pltpu.VMEM((D_MODEL, LANES), BF16), pltpu.VMEM((N_EXPERTS, 1), F32)],
        compiler_params=_params(("arbitrary",)),
        name=name,
    )(y, w_out_stack, x, gain, bias,
      jnp.pad(w_router, ((0, 0), (0, LANES - N_EXPERTS))),
      jnp.pad(b_router.reshape(1, N_EXPERTS), ((0, 0), (0, LANES - N_EXPERTS))))


def _gather_copy(x_hbm, xbuf, sem, src_row, r):
    return pltpu.make_async_copy(
        x_hbm.at[pl.ds(pl.multiple_of(src_row, SUBLANES), SUBLANES), :],
        xbuf.at[pl.ds(pl.multiple_of(r * SUBLANES, SUBLANES), SUBLANES), :],
        sem)


def _scatter_copy(ybuf, y_hbm, sem, dst_row, r):
    return pltpu.make_async_copy(
        ybuf.at[pl.ds(pl.multiple_of(r * SUBLANES, SUBLANES), SUBLANES), :],
        y_hbm.at[pl.ds(pl.multiple_of(dst_row, SUBLANES), SUBLANES), :],
        sem)


def _moe_body(be_ref, nu_ref, src_ref, srcn_ref, dst_ref, x_hbm, wgu_ref, bgu_ref, wd_ref, bd_ref,
              y_hbm, xbuf, ybuf, wgu_bf, wd_bf, gsem, ssem):
    i = pl.program_id(0)
    nb = pl.num_programs(0)
    nused = nu_ref[0]
    slot = i % 2
    rows = MOE_TM * LANE_TILES
    group = 32

    def gather_all(idx_ref, s):
        def body(g, c):
            for q in range(group):
                r = g * group + q
                _gather_copy(x_hbm, xbuf.at[s], gsem.at[s], idx_ref[0, r], r).start(priority=q % 2)
            return c
        lax.fori_loop(0, MOE_TM // group, body, 0)

    def scatter_all(idx_ref, s):
        def body(g, c):
            for q in range(group):
                r = g * group + q
                _scatter_copy(ybuf.at[s], y_hbm, ssem.at[s], idx_ref[0, r], r).start(priority=q % 2)
            return c
        lax.fori_loop(0, MOE_TM // group, body, 0)

    def wait_scatter(s):
        pltpu.make_async_copy(ybuf.at[s], y_hbm.at[pl.ds(0, rows), :], ssem.at[s]).wait()

    @pl.when(i == 0)
    def _():
        ybuf[...] = jnp.zeros_like(ybuf)
        tail = y_hbm.at[pl.ds(y_hbm.shape[0] - 2 * rows, 2 * rows), :]
        for s in range(2):
            cp = pltpu.make_async_copy(ybuf.at[s], tail.at[pl.ds(s * rows, rows), :], ssem.at[s])
            cp.start()
            cp.wait()

    @pl.when(i < nused)
    def _():
        @pl.when(i == 0)
        def _():
            gather_all(src_ref, 0)

        @pl.when(i + 1 < nused)
        def _():
            gather_all(srcn_ref, 1 - slot)

        changed = jnp.logical_or(i == 0, be_ref[i] != be_ref[jnp.maximum(i - 1, 0)])

        @pl.when(changed)
        def _():
            wgu_bf[...] = wgu_ref[...].astype(BF16)
            wd_bf[...] = wd_ref[...].astype(BF16)

        pltpu.make_async_copy(x_hbm.at[pl.ds(0, rows), :], xbuf.at[slot], gsem.at[slot]).wait()
        x = _load_token_tiles(xbuf.at[slot], MOE_TM).astype(BF16)
        gu = jnp.dot(x, wgu_bf[...], preferred_element_type=F32) + bgu_ref[...]
        gate = jnp.minimum(gu[:, :D_FF], SWIGLU_LIMIT)
        up = jnp.clip(gu[:, D_FF:], -SWIGLU_LIMIT, SWIGLU_LIMIT)
        hidden = (up + 1.0) * gate * jax.nn.sigmoid(SWIGLU_ALPHA * gate)
        y = jnp.dot(hidden.astype(BF16), wd_bf[...], preferred_element_type=F32) + bd_ref[...]

        @pl.when(i >= 2)
        def _():
            wait_scatter(slot)

        _store_token_tiles(ybuf.at[slot], y)
        scatter_all(dst_ref, slot)

    @pl.when(i == nb - 1)
    def _():
        @pl.when(nused >= 1)
        def _():
            wait_scatter((nused - 1) % 2)

        @pl.when(nused >= 2)
        def _():
            wait_scatter(nused % 2)


def _moe_experts(xt, block_exp, n_used, src_rows, dst_rows, w_gu, b_gu, w_down, b_down, layer, n_tok):
    nb = block_exp.shape[0]
    rows = MOE_TM * LANE_TILES
    n_out_rows = (TOP_K * n_tok + 2 * MOE_TM) * LANE_TILES
    smem_blk = lambda f: pl.BlockSpec((None, 1, MOE_TM), f, memory_space=pltpu.SMEM)
    grid_spec = pltpu.PrefetchScalarGridSpec(
        num_scalar_prefetch=2,
        grid=(nb,),
        in_specs=[smem_blk(lambda i, be, nu: (i, 0, 0)),
                  smem_blk(lambda i, be, nu: (jnp.minimum(i + 1, nb - 1), 0, 0)),
                  smem_blk(lambda i, be, nu: (i, 0, 0)),
                  pl.BlockSpec(memory_space=pl.ANY),
                  pl.BlockSpec((None, None, D_MODEL, 2 * D_FF), lambda i, be, nu: (layer, be[i], 0, 0)),
                  pl.BlockSpec((None, None, 1, 2 * D_FF), lambda i, be, nu: (layer, be[i], 0, 0)),
                  pl.BlockSpec((None, None, D_FF, D_MODEL), lambda i, be, nu: (layer, be[i], 0, 0)),
                  pl.BlockSpec((None, None, 1, D_MODEL), lambda i, be, nu: (layer, be[i], 0, 0))],
        out_specs=pl.BlockSpec(memory_space=pl.ANY),
        scratch_shapes=[pltpu.VMEM((2, rows, LANES), F32),
                        pltpu.VMEM((2, rows, LANES), F32),
                        pltpu.VMEM((D_MODEL, 2 * D_FF), BF16),
                        pltpu.VMEM((D_FF, D_MODEL), BF16),
                        pltpu.SemaphoreType.DMA((2,)),
                        pltpu.SemaphoreType.DMA((2,))],
    )
    return pl.pallas_call(
        _moe_body,
        grid_spec=grid_spec,
        out_shape=jax.ShapeDtypeStruct((n_out_rows, LANES), F32),
        compiler_params=_params(("arbitrary",)),
        name=f"moe_experts_l{layer}",
    )(block_exp, n_used, src_rows, src_rows, dst_rows, xt, w_gu, b_gu, w_down, b_down)


def _route_tables(route_i, counts, n_tok):
    n_assign = n_tok * TOP_K
    nb = n_assign // MOE_TM + N_EXPERTS
    counts = counts.reshape(N_EXPERTS).astype(I32)
    padded = ((counts + MOE_TM - 1) // MOE_TM) * MOE_TM
    pad_end = jnp.cumsum(padded)
    pad_start = pad_end - padded
    eid = route_i[:TOP_K]
    rank = route_i[TOP_K:]
    expert_ids = jnp.arange(N_EXPERTS, dtype=I32)
    start_of = jnp.sum(jnp.where(eid[..., None] == expert_ids, pad_start, 0), axis=-1)
    dest = (start_of + rank).reshape(n_assign)
    _, order = lax.sort((dest, jnp.arange(n_assign, dtype=I32)), num_keys=1)
    order = jnp.concatenate([order, jnp.zeros((MOE_TM,), I32)])
    block_start = jnp.arange(nb, dtype=I32) * MOE_TM
    block_exp = jnp.minimum(jnp.sum((pad_end[None, :] <= block_start[:, None]).astype(I32), axis=1),
                            N_EXPERTS - 1)
    n_used = (pad_end[-1:] // MOE_TM).astype(I32)
    mine = block_exp[:, None] == expert_ids
    in_expert = block_start - jnp.sum(jnp.where(mine, pad_start, 0), axis=1)
    n_valid = jnp.clip(jnp.sum(jnp.where(mine, counts, 0), axis=1) - in_expert, 0, MOE_TM)
    compact_start = jnp.sum(jnp.where(mine, jnp.cumsum(counts) - counts, 0), axis=1) + in_expert
    window = jax.vmap(lambda o: lax.dynamic_slice(order, (o,), (MOE_TM,)))(jnp.clip(compact_start, 0, n_assign))
    col = jnp.arange(MOE_TM, dtype=I32)[None, :]
    landing = n_assign + (block_start[:, None] + col) % (2 * MOE_TM)
    row_assign = jnp.where(col < n_valid[:, None], window, landing)
    src_rows = jnp.where(row_assign < n_assign, row_assign % n_tok, 0) * LANE_TILES
    dst_rows = row_assign * LANE_TILES
    return block_exp, n_used, src_rows.reshape(nb, 1, MOE_TM), dst_rows.reshape(nb, 1, MOE_TM)


def _combine_body(y0_ref, y1_ref, y2_ref, y3_ref, rf_ref, xt_ref, g_ref, b_ref, o_ref, obf_ref):
    tm = o_ref.shape[0]
    rf = rf_ref[...]
    parts = []
    for c in range(LANE_TILES):
        acc = DEEPNORM_ALPHA * xt_ref[pl.ds(c, tm, stride=LANE_TILES), :]
        for k, yk in enumerate((y0_ref, y1_ref, y2_ref, y3_ref)):
            acc = acc + rf[:, k:k + 1] * yk[pl.ds(c, tm, stride=LANE_TILES), :]
        parts.append(acc)
    z = jnp.concatenate(parts, axis=-1)
    out = _layer_norm(z, g_ref[...], b_ref[...])
    o_ref[...] = out
    obf_ref[...] = out.astype(BF16)


def _combine(y_tk, route_f, xt, gain, bias, n_tok, name):
    tm = ROW_TM
    nblk = n_tok // tm
    yspec = lambda k: pl.BlockSpec((tm * LANE_TILES, LANES), lambda i: (k * nblk + i, 0))
    return pl.pallas_call(
        _combine_body,
        grid=(nblk,),
        in_specs=[yspec(0), yspec(1), yspec(2), yspec(3),
                  pl.BlockSpec((tm, LANES), lambda i: (i, 0)),
                  pl.BlockSpec((tm * LANE_TILES, LANES), lambda i: (i, 0)),
                  pl.BlockSpec((1, D_MODEL), lambda i: (0, 0)),
                  pl.BlockSpec((1, D_MODEL), lambda i: (0, 0))],
        out_specs=[pl.BlockSpec((tm, D_MODEL), lambda i: (i, 0)),
                   pl.BlockSpec((tm, D_MODEL), lambda i: (i, 0))],
        out_shape=[jax.ShapeDtypeStruct((n_tok, D_MODEL), F32),
                   jax.ShapeDtypeStruct((n_tok, D_MODEL), BF16)],
        compiler_params=_params(("parallel",)),
        name=name,
    )(y_tk, y_tk, y_tk, y_tk, route_f, xt, gain, bias)


def _gelu_tanh(x):
    return 0.5 * x * (1.0 + jnp.tanh(math.sqrt(2.0 / math.pi) * (x + 0.044715 * (x * x * x))))


LRU_ROWS = 256


def _lru_body(rec_ref, g_ref, cw_ref, cb_ref, wr_ref, br_ref, wi_ref, bi_ref, lam_ref, y_ref,
              ext_ref, a_ref, b_ref):
    seq, ct = rec_ref.shape
    heads = ct // LRU_BLOCK
    ext_ref[0:SUBLANES, :] = jnp.zeros((SUBLANES, ct), F32)
    ext_ref[SUBLANES:, :] = rec_ref[...]
    neg_lam = -lam_ref[...]
    softplus = jnp.maximum(neg_lam, 0.0) + jnp.log1p(jnp.exp(-jnp.abs(neg_lam)))
    wr = [wr_ref[h].astype(BF16) for h in range(heads)]
    wi = [wi_ref[h].astype(BF16) for h in range(heads)]

    for c in range(seq // LRU_ROWS):
        r0 = c * LRU_ROWS
        conv = cb_ref[...] + cw_ref[CONV_WIDTH - 1:CONV_WIDTH, :] * rec_ref[pl.ds(r0, LRU_ROWS), :]
        for j in range(CONV_WIDTH - 1):
            conv = conv + cw_ref[j:j + 1, :] * ext_ref[pl.ds(r0 + SUBLANES - (CONV_WIDTH - 1) + j, LRU_ROWS), :]
        xb = conv.astype(BF16)
        r_parts = [jnp.dot(xb[:, h * LRU_BLOCK:(h + 1) * LRU_BLOCK], wr[h], preferred_element_type=F32)
                   for h in range(heads)]
        i_parts = [jnp.dot(xb[:, h * LRU_BLOCK:(h + 1) * LRU_BLOCK], wi[h], preferred_element_type=F32)
                   for h in range(heads)]
        r = jax.nn.sigmoid(jnp.concatenate(r_parts, axis=-1) + br_ref[...])
        ig = jax.nn.sigmoid(jnp.concatenate(i_parts, axis=-1) + bi_ref[...])
        log_a = -LRU_C * r * softplus
        a = jnp.exp(log_a)
        a_ref[pl.ds(r0, LRU_ROWS), :] = a
        b_ref[pl.ds(r0, LRU_ROWS), :] = jnp.sqrt(-jnp.tanh(log_a) * (a * a + 1.0)) * (ig * conv)

    _lru_scan_body(a_ref, b_ref, g_ref, y_ref)


def _lru_scan_body(a_ref, b_ref, g_ref, y_ref):
    seq, ct = a_ref.shape
    row = lax.broadcasted_iota(I32, (SUBLANES, ct), 0)

    def step(gi, carry):
        off = pl.multiple_of(gi * SUBLANES, SUBLANES)
        a = a_ref[pl.ds(off, SUBLANES), :]
        b = b_ref[pl.ds(off, SUBLANES), :]
        for d in (1, 2, 4):
            keep = row >= d
            b = jnp.where(keep, a * pltpu.roll(b, d, 0) + b, b)
            a = jnp.where(keep, a * pltpu.roll(a, d, 0), a)
        h = a * carry + b
        y_ref[pl.ds(off, SUBLANES), :] = _gelu_tanh(g_ref[pl.ds(off, SUBLANES), :]) * h
        return jnp.broadcast_to(h[SUBLANES - 1:SUBLANES, :], (SUBLANES, ct))

    lax.fori_loop(0, seq // SUBLANES, step, jnp.zeros((SUBLANES, ct), F32), unroll=4)


def _rglru_mixer(x2, bsz, seq, idx, a_w_in, conv_w, conv_b, w_rgate, b_rgate, w_igate, b_igate, lam, layer):
    t = bsz * seq
    w = D_MODEL
    u = _matmul(x2, a_w_in, idx, 0, 2 * w, F32, f"lru_in_l{layer}")
    ct = 2 * LRU_BLOCK
    nct = w // ct
    hpt = ct // LRU_BLOCK
    vec = lambda: pl.BlockSpec((1, ct), lambda b, j: (0, j))
    return pl.pallas_call(
        _lru_body,
        grid=(bsz, nct),
        in_specs=[pl.BlockSpec((seq, ct), lambda b, j: (b, nct + j)),
                  pl.BlockSpec((seq, ct), lambda b, j: (b, j)),
                  pl.BlockSpec((CONV_WIDTH, ct), lambda b, j: (0, j)),
                  vec(),
                  pl.BlockSpec((hpt, LRU_BLOCK, LRU_BLOCK), lambda b, j: (j, 0, 0)),
                  vec(),
                  pl.BlockSpec((hpt, LRU_BLOCK, LRU_BLOCK), lambda b, j: (j, 0, 0)),
                  vec(), vec()],
        out_specs=pl.BlockSpec((seq, ct), lambda b, j: (b, j)),
        out_shape=jax.ShapeDtypeStruct((t, w), F32),
        scratch_shapes=[pltpu.VMEM((seq + SUBLANES, ct), F32),
                        pltpu.VMEM((seq, ct), F32),
                        pltpu.VMEM((seq, ct), F32)],
        compiler_params=_params(("parallel", "parallel")),
        name=f"lru_l{layer}",
    )(u, u, conv_w[idx], conv_b[idx][None], w_rgate[idx], b_rgate[idx][None],
      w_igate[idx], b_igate[idx][None], lam[idx][None])


def _ret_body(lg_ref, q_ref, k_ref, v_ref, g_ref, y_ref, state_ref):
    c = pl.program_id(1)
    cs = RET_CHUNK
    heads = range(RET_HEADS)

    @pl.when(c == 0)
    def _():
        state_ref[...] = jnp.zeros_like(state_ref)

    qi = lax.broadcasted_iota(I32, (cs, cs), 0)
    ki = lax.broadcasted_iota(I32, (cs, cs), 1)
    rel = (qi - ki).astype(F32)
    pos_col = lax.broadcasted_iota(I32, (cs, 1), 0).astype(F32)
    lgs = [lg_ref[h] for h in heads]
    intra_decay = [jnp.where(rel >= 0, jnp.exp(lg * jnp.maximum(rel, 0.0)), 0.0) for lg in lgs]
    query_decay = [jnp.exp(lg * (pos_col + 1.0)) for lg in lgs]
    key_decay = [jnp.exp(lg * (cs - 1.0 - pos_col)) for lg in lgs]
    chunk_decay = [jnp.exp(jnp.full((1, 1), lg * cs, F32)) for lg in lgs]

    scale = RET_QK ** -0.5
    dn_nt = (((1,), (1,)), ((), ()))
    dn_tn = (((0,), (0,)), ((), ()))
    qs = [q_ref[:, h * RET_QK:(h + 1) * RET_QK] for h in heads]
    ks = [k_ref[:, h * RET_QK:(h + 1) * RET_QK] for h in heads]
    vs = [v_ref[:, h * RET_V:(h + 1) * RET_V] for h in heads]
    intra = [lax.dot_general(q, k, dn_nt, preferred_element_type=F32) * scale * d
             for q, k, d in zip(qs, ks, intra_decay)]
    states = [state_ref[h] for h in heads]
    cross = [jnp.dot(q, s.astype(BF16), preferred_element_type=F32) * d
             for q, s, d in zip(qs, states, query_decay)]
    outs = [jnp.dot(a.astype(BF16), v, preferred_element_type=F32) + x for a, v, x in zip(intra, vs, cross)]
    kds = [(k.astype(F32) * (scale * d)).astype(BF16) for k, d in zip(ks, key_decay)]
    upd = [lax.dot_general(kd, v, dn_tn, preferred_element_type=F32) for kd, v in zip(kds, vs)]
    for h in heads:
        state_ref[h] = chunk_decay[h] * states[h] + upd[h]
    for h in heads:
        out = outs[h]
        o = out * lax.rsqrt(jnp.mean(out * out, axis=-1, keepdims=True) + RET_EPS)
        g = g_ref[:, h * RET_V:(h + 1) * RET_V]
        y_ref[:, h * RET_V:(h + 1) * RET_V] = (g * jax.nn.sigmoid(g) * o).astype(y_ref.dtype)


def _retention_mixer(x2, bsz, seq, idx, b_w_in, layer):
    t = bsz * seq
    qk_w = RET_HEADS * RET_QK
    v_w = RET_HEADS * RET_V
    qkv = _matmul(x2, b_w_in, idx, 0, 2 * qk_w + v_w, BF16, f"ret_qkv_l{layer}")
    gate = _matmul(x2, b_w_in, idx, 2 * qk_w + v_w, v_w, F32, f"ret_gate_l{layer}")
    nc = seq // RET_CHUNK
    log_gamma = jnp.log1p(-jnp.exp2(-5.0 - jnp.arange(RET_HEADS, dtype=F32)))
    grid_spec = pltpu.PrefetchScalarGridSpec(
        num_scalar_prefetch=1,
        grid=(bsz, nc),
        in_specs=[pl.BlockSpec((RET_CHUNK, qk_w), lambda b, c, lg: (b * nc + c, 0)),
                  pl.BlockSpec((RET_CHUNK, qk_w), lambda b, c, lg: (b * nc + c, 1)),
                  pl.BlockSpec((RET_CHUNK, v_w), lambda b, c, lg: (b * nc + c, 1)),
                  pl.BlockSpec((RET_CHUNK, v_w), lambda b, c, lg: (b * nc + c, 0))],
        out_specs=pl.BlockSpec((RET_CHUNK, v_w), lambda b, c, lg: (b * nc + c, 0)),
        scratch_shapes=[pltpu.VMEM((RET_HEADS, RET_QK, RET_V), F32)],
    )
    return pl.pallas_call(
        _ret_body,
        grid_spec=grid_spec,
        out_shape=jax.ShapeDtypeStruct((t, v_w), BF16),
        compiler_params=_params(("parallel", "arbitrary")),
        name=f"retention_l{layer}",
    )(log_gamma, qkv, qkv, qkv, gate)


def _attn_body(*refs, dilation, has_prev):
    if has_prev:
        q_ref, kc_ref, kp_ref, vc_ref, vp_ref, o_ref, st_ref = refs
    else:
        q_ref, kc_ref, vc_ref, o_ref, st_ref = refs
    n = pl.program_id(2)
    nw = q_ref.shape[0]
    dh = ATTN_HEAD_DIM
    scale = dh ** -0.5
    qi = lax.broadcasted_iota(I32, (nw, nw), 0)
    ki = lax.broadcasted_iota(I32, (nw, nw), 1)
    steps_c = qi - ki
    valid_c = steps_c >= 0
    dist_c = (steps_c * dilation).astype(F32)
    q = q_ref[...]
    kc = kc_ref[...]
    vc = vc_ref[...]
    lane = lax.broadcasted_iota(I32, (nw, LANES), 1)
    dn = (((1,), (1,)), ((), ()))
    heads = range(ATTN_HEADS)
    sls = [slice(h * dh, (h + 1) * dh) for h in heads]
    slopes = [2.0 ** (-8.0 * (h + 1) / ATTN_HEADS) for h in heads]
    sc_c = [lax.dot_general(q[:, sl], kc[:, sl], dn, preferred_element_type=F32) for sl in sls]
    sc_c = [jnp.where(valid_c, s * scale - sp * dist_c, -jnp.inf) for s, sp in zip(sc_c, slopes)]
    if not has_prev:
        ms = [jnp.max(a, axis=-1, keepdims=True) for a in sc_c]
    if has_prev:
        steps_p = steps_c + nw
        valid_p = steps_p <= jnp.where(n > 0, nw, -1)
        dist_p = (steps_p * dilation).astype(F32)
        kp = kp_ref[...]
        vp = vp_ref[...]
        sc_p = [lax.dot_general(q[:, sl], kp[:, sl], dn, preferred_element_type=F32) for sl in sls]
        sc_p = [jnp.where(valid_p, s * scale - sp * dist_p, -jnp.inf) for s, sp in zip(sc_p, slopes)]
        ms = [jnp.max(jnp.maximum(a, b), axis=-1, keepdims=True) for a, b in zip(sc_c, sc_p)]
    pc = [jnp.exp(a - m) for a, m in zip(sc_c, ms)]
    outs = [jnp.dot(a.astype(BF16), vc[:, sl], preferred_element_type=F32) for a, sl in zip(pc, sls)]
    if has_prev:
        pp = [jnp.exp(b - m) for b, m in zip(sc_p, ms)]
        sums = [jnp.sum(a + b, axis=-1, keepdims=True) for a, b in zip(pc, pp)]
        outs = [o + jnp.dot(b.astype(BF16), vp[:, sl], preferred_element_type=F32)
                for o, b, sl in zip(outs, pp, sls)]
    else:
        sums = [jnp.sum(a, axis=-1, keepdims=True) for a in pc]
    stats = jnp.zeros((nw, LANES), F32)
    for h, sl, o, m, sm in zip(heads, sls, outs, ms, sums):
        o_ref[:, sl] = o.astype(o_ref.dtype)
        stats = jnp.where(lane == h, m, stats)
        stats = jnp.where(lane == ATTN_HEADS + h, sm, stats)
    st_ref[...] = stats


def _merge_body(o0_ref, o1_ref, o2_ref, s0_ref, s1_ref, s2_ref, y_ref):
    tm = y_ref.shape[0]
    stats = [s0_ref[...], s1_ref[...], s2_ref[...]]
    lane = lax.broadcasted_iota(I32, (tm, LANES), 1)
    maxes = [s for s in stats]
    sums = [pltpu.roll(s, LANES - ATTN_HEADS, 1) for s in stats]
    m_all = jnp.maximum(jnp.maximum(maxes[0], maxes[1]), maxes[2])
    wgt = [jnp.exp(m - m_all) for m in maxes]
    den = wgt[0] * sums[0] + wgt[1] * sums[1] + wgt[2] * sums[2]
    er = lax.broadcasted_iota(I32, (LANES, D_MODEL), 0)
    ec = lax.broadcasted_iota(I32, (LANES, D_MODEL), 1)
    expand = (ec // ATTN_HEAD_DIM == er).astype(BF16)
    acc = jnp.zeros((tm, D_MODEL), F32)
    for w, o_ref in zip(wgt, (o0_ref, o1_ref, o2_ref)):
        coef = jnp.where(lane < ATTN_HEADS, w / den, 0.0)
        hi = coef.astype(BF16)
        lo = (coef - hi.astype(F32)).astype(BF16)
        spread = (jnp.dot(hi, expand, preferred_element_type=F32)
                  + jnp.dot(lo, expand, preferred_element_type=F32))
        acc = acc + spread * o_ref[...].astype(F32)
    y_ref[...] = acc.astype(y_ref.dtype)


def _attention_mixer(x2, bsz, seq, idx, c_w_in, layer):
    t = bsz * seq
    w = D_MODEL
    n_groups = len(ATTN_PATTERNS)
    outs, stats = [], []
    for g, (window, dilation) in enumerate(ATTN_PATTERNS):
        nw = window // dilation
        sub = seq // dilation
        assert nw == 128 and sub % nw == 0
        nb = sub // nw

        def to_sub(a):
            return a.reshape(bsz, sub, dilation, -1).transpose(0, 2, 1, 3).reshape(t, -1)

        def from_sub(a):
            return a.reshape(bsz, dilation, sub, -1).transpose(0, 2, 1, 3).reshape(t, -1)

        xs = x2 if dilation == 1 else to_sub(x2)
        qkv = _matmul(xs, c_w_in, idx, g * w, 3 * w, BF16, f"attn_qkv_g{g}_l{layer}",
                      col_block_stride=n_groups)
        cur = lambda slab: pl.BlockSpec((nw, w), lambda b, r, n, slab=slab: ((b * dilation + r) * nb + n, slab))
        prev = lambda slab: pl.BlockSpec(
            (nw, w), lambda b, r, n, slab=slab: ((b * dilation + r) * nb + jnp.maximum(n - 1, 0), slab))
        has_prev = nb > 1
        in_specs = [cur(0), cur(1), prev(1), cur(2), prev(2)] if has_prev else [cur(0), cur(1), cur(2)]
        o, st = pl.pallas_call(
            functools.partial(_attn_body, dilation=dilation, has_prev=has_prev),
            grid=(bsz, dilation, nb),
            in_specs=in_specs,
            out_specs=[pl.BlockSpec((nw, w), lambda b, r, n: ((b * dilation + r) * nb + n, 0)),
                       pl.BlockSpec((nw, LANES), lambda b, r, n: ((b * dilation + r) * nb + n, 0))],
            out_shape=[jax.ShapeDtypeStruct((t, w), BF16), jax.ShapeDtypeStruct((t, LANES), F32)],
            compiler_params=_params(("parallel", "parallel", "parallel")),
            name=f"attn_g{g}_l{layer}",
        )(*([qkv] * len(in_specs)))
        outs.append(o if dilation == 1 else from_sub(o))
        stats.append(st if dilation == 1 else from_sub(st))
    tm = ROW_TM
    return pl.pallas_call(
        _merge_body,
        grid=(t // tm,),
        in_specs=[pl.BlockSpec((tm, w), lambda i: (i, 0))] * 3 + [pl.BlockSpec((tm, LANES), lambda i: (i, 0))] * 3,
        out_specs=pl.BlockSpec((tm, w), lambda i: (i, 0)),
        out_shape=jax.ShapeDtypeStruct((t, w), BF16),
        compiler_params=_params(("parallel",)),
        name=f"attn_merge_l{layer}",
    )(*outs, *stats)


def kernel(x, a_w_in, a_conv_w, a_conv_b, a_w_rgate, a_b_rgate, a_w_igate, a_b_igate, a_lambda, a_w_out,
           b_w_in, b_w_out, c_w_in, c_w_out, ln_gain, ln_bias, moe_w_router, moe_b_router, moe_w_gu,
           moe_b_gu, moe_w_down, moe_b_down):
    bsz, seq, d = x.shape
    t = bsz * seq
    x2 = x.reshape(t, d)
    x_mxu = x2
    b_gu4 = moe_b_gu[:, :, None, :]
    b_down4 = moe_b_down[:, :, None, :]
    for layer in range(DEPTH):
        kind, idx = layer % N_MIXERS, layer // N_MIXERS
        if kind == 0:
            y = _rglru_mixer(x_mxu, bsz, seq, idx, a_w_in, a_conv_w, a_conv_b, a_w_rgate, a_b_rgate,
                             a_w_igate, a_b_igate, a_lambda, layer)
            w_out = a_w_out
        elif kind == 1:
            y = _retention_mixer(x_mxu, bsz, seq, idx, b_w_in, layer)
            w_out = b_w_out
        else:
            y = _attention_mixer(x_mxu, bsz, seq, idx, c_w_in, layer)
            w_out = c_w_out
        xt, route_f, route_i, counts = _post_mixer(
            y, w_out, idx, x2, ln_gain[layer, 0][None], ln_bias[layer, 0][None],
            moe_w_router[layer], moe_b_router[layer][None], f"post_mixer_l{layer}")
        block_exp, n_used, src_rows, dst_rows = _route_tables(route_i, counts, t)
        y_tk = _moe_experts(xt, block_exp, n_used, src_rows, dst_rows,
                            moe_w_gu, b_gu4, moe_w_down, b_down4, layer, t)
        x2, x_mxu = _combine(y_tk, route_f, xt, ln_gain[layer, 1][None], ln_bias[layer, 1][None], t,
                             f"combine_l{layer}")
    return x2.reshape(bsz, seq, d)
```
